```python
import jax, jax.numpy as jnp
from jax import lax
import numpy as np

D_MODEL = 1024
BATCH = 2
SEQ = 8192
DEPTH = 1
DEC_BATCH = 32
DEC_SEQ = 64
PAST_LEN = 4096

CHUNK = 64
SUB = 16
N_MEM = 256
EPS = 1e-6
ROPE_BASE = 10000.0
HG_H = 8
HG_DK = 128
HG_DV = 128
HG_KW = HG_H * HG_DK
HG_W = HG_H * HG_DV
RT_H = 4
RT_DK = 64
RT_DV = 128
RT_QK = RT_H * RT_DK
RT_W = RT_H * RT_DV
XA_H = 4
XA_DH = 128
XA_W = XA_H * XA_DH
D_MIX = HG_W + RT_W + XA_W
SPLIT_SIZES = (HG_KW, HG_KW, HG_W, HG_W, RT_QK, RT_QK, RT_W, RT_W, XA_W, XA_W)
D_IN = 2 * HG_KW + 2 * HG_W + 2 * RT_QK + 2 * RT_W + 2 * XA_W

kernel_name = "hgrn2_retention_memxattn_parallel_stream_step"


def _split_points():
    pts, acc = [], 0
    for s in SPLIT_SIZES[:-1]:
        acc += s
        pts.append(acc)
    return pts


def rms_norm(x, g):
    x32 = x.astype(jnp.float32)
    r = x32 * lax.rsqrt(jnp.mean(x32 * x32, axis=-1, keepdims=True) + EPS)
    return r * g.astype(jnp.float32)


def head_rms_norm(o, g):
    h, d = o.shape[-2], o.shape[-1]
    r = o * lax.rsqrt(jnp.mean(o * o, axis=-1, keepdims=True) + EPS)
    return r * g.astype(jnp.float32).reshape(h, d)


def head_group_norm(o, g):
    h, d = o.shape[-2], o.shape[-1]
    mu = jnp.mean(o, axis=-1, keepdims=True)
    c = o - mu
    r = c * lax.rsqrt(jnp.mean(c * c, axis=-1, keepdims=True) + EPS)
    return r * g.astype(jnp.float32).reshape(h, d)


def apply_rope(x, pos):
    half = x.shape[-1] // 2
    inv_freq = ROPE_BASE ** (-jnp.arange(half, dtype=jnp.float32) / half)
    ang = pos.astype(jnp.float32)[:, None] * inv_freq[None, :]
    cos, sin = jnp.cos(ang)[None, :, None, :], jnp.sin(ang)[None, :, None, :]
    x1, x2 = x[..., :half], x[..., half:]
    return jnp.concatenate([x1 * cos - x2 * sin, x1 * sin + x2 * cos], axis=-1)


def hgrn2_chunkwise(q, k, v, g, s0):
    b, t, h, dk = q.shape
    dv = v.shape[-1]
    nc, nsb = t // CHUNK, CHUNK // SUB

    def blk(a):
        return a.reshape(b, nc, nsb, SUB, h, a.shape[-1]).transpose(0, 4, 1, 2, 3, 5)

    q, k, v, g = blk(q), blk(k), blk(v), blk(g)
    gc = jnp.cumsum(g.reshape(b, h, nc, CHUNK, dk), axis=3).reshape(b, h, nc, nsb, SUB, dk)
    g_end = gc[..., -1, :]
    g_ref = jnp.concatenate([jnp.zeros_like(g_end[:, :, :, :1]), g_end[:, :, :, :-1]], axis=3)
    g_chunk = g_end[:, :, :, -1]
    q_ref = q * jnp.exp(gc - g_ref[..., None, :])
    k_ref = k * jnp.exp(g_ref[..., None, :] - gc)
    causal = jnp.tril(jnp.ones((SUB, SUB), dtype=bool))
    a_diag = jnp.where(causal, jnp.einsum('bhcaid,bhcajd->bhcaij', q_ref, k_ref), 0.0)
    o = jnp.einsum('bhcaij,bhcajv->bhcaiv', a_diag, v)
    k_end = k * jnp.exp(g_end[..., None, :] - gc)
    sb_lower = jnp.arange(nsb)[:, None] > jnp.arange(nsb)[None, :]
    mid = jnp.exp(jnp.where(sb_lower[..., None],
                            g_ref[:, :, :, :, None, :] - g_end[:, :, :, None, :, :], -jnp.inf))
    a_off = jnp.einsum('bhcaid,bhcaed,bhcejd->bhcaeij', q_ref, mid, k_end)
    o = o + jnp.einsum('bhcaeij,bhcejv->bhcaiv', a_off, v)
    k_chunk = k * jnp.exp(g_chunk[:, :, :, None, None, :] - gc)
    ds = jnp.einsum('bhcaid,bhcaiv->bhcdv', k_chunk, v)

    def step(s, inp):
        decay, d = inp
        return decay[..., None] * s + d, s

    s_fin, s_starts = lax.scan(step, s0, (jnp.moveaxis(jnp.exp(g_chunk), 2, 0), jnp.moveaxis(ds, 2, 0)))
    s_starts = jnp.moveaxis(s_starts, 0, 2)
    o = o + jnp.einsum('bhcaid,bhcdv->bhcaiv', q * jnp.exp(gc), s_starts)
    o = o.transpose(0, 2, 3, 4, 1, 5).reshape(b, t, h, dv)
    return o, s_fin


def retention_chunkwise(q, k, v, r0, chunk):
    b, t, h, dk = q.shape
    dv = v.shape[-1]
    nc = t // chunk
    log_gamma = jnp.log1p(-jnp.exp2(-5.0 - jnp.arange(h, dtype=jnp.float32)))
    pos = jnp.arange(chunk, dtype=jnp.float32)
    lg = log_gamma[:, None]
    intra = jnp.exp(lg[..., None] * jnp.abs(pos[:, None] - pos[None, :]))
    q_decay = jnp.exp(lg * (pos + 1.0))
    k_decay = jnp.exp(lg * (chunk - 1.0 - pos))
    chunk_decay = jnp.exp(log_gamma * chunk)
    qc = q.reshape(b, nc, chunk, h, dk)
    kc = k.reshape(b, nc, chunk, h, dk)
    vc = v.reshape(b, nc, chunk, h, dv)
    scores = jnp.einsum('bcihd,bcjhd->bchij', qc, kc) * intra
    o = jnp.einsum('bchij,bcjhv->bcihv', scores, vc)
    dr = jnp.einsum('bcjhd,hj,bcjhv->bchdv', kc, k_decay, vc)

    def step(r, d):
        return chunk_decay[:, None, None] * r + d, r

    r_fin, r_starts = lax.scan(step, r0, jnp.moveaxis(dr, 1, 0))
    r_starts = jnp.moveaxis(r_starts, 0, 1)
    o = o + jnp.einsum('bcihd,hi,bchdv->bcihv', qc, q_decay, r_starts)
    return o.reshape(b, t, h, dv), r_fin


def memory_kv(mem, mem_norm_g, w_mem_k, w_mem_v):
    m = rms_norm(mem, mem_norm_g)
    b = mem.shape[0]
    mk = (m @ w_mem_k.astype(jnp.float32)).reshape(b, N_MEM, XA_H, XA_DH)
    mv = (m @ w_mem_v.astype(jnp.float32)).reshape(b, N_MEM, XA_H, XA_DH)
    return mk, mv


def mixer_layer(x, pos, s_hg, s_rt, mem_k, mem_v, norm_g, w_in, lb, hg_norm_g, rt_norm_g, w_out):
    b, t, _ = x.shape
    f32 = jnp.float32
    h = rms_norm(x, norm_g)
    proj = h @ w_in.astype(f32)
    hg_q, hg_f, hg_i, hg_gate, rt_q, rt_k, rt_v, rt_gate, xa_q, xa_gate = jnp.split(proj, _split_points(), axis=-1)

    lb = lb.astype(f32)
    sig = jax.nn.sigmoid(hg_f)
    log_f = jnp.log(lb + (1.0 - lb) * sig)
    k_in = (1.0 - lb) * jax.nn.sigmoid(-hg_f)
    qa = jax.nn.silu(hg_q).reshape(b, t, HG_H, HG_DK)
    ka = k_in.reshape(b, t, HG_H, HG_DK)
    ga = log_f.reshape(b, t, HG_H, HG_DK)
    va = hg_i.reshape(b, t, HG_H, HG_DV)
    pad = (-t) % CHUNK
    pw = ((0, 0), (0, pad), (0, 0), (0, 0))
    o_hg, s_hg_new = hgrn2_chunkwise(jnp.pad(qa, pw), jnp.pad(ka, pw), jnp.pad(va, pw), jnp.pad(ga, pw),
                                     s_hg.astype(f32))
    o_hg = head_rms_norm(o_hg[:, :t], hg_norm_g).reshape(b, t, HG_W) * jax.nn.silu(hg_gate)

    rq = apply_rope(rt_q.reshape(b, t, RT_H, RT_DK), pos)
    rk = apply_rope(rt_k.reshape(b, t, RT_H, RT_DK), pos) * (RT_DK ** -0.5)
    rv = rt_v.reshape(b, t, RT_H, RT_DV)
    o_rt, s_rt_new = retention_chunkwise(rq, rk, rv, s_rt.astype(f32), min(CHUNK, t))
    o_rt = head_group_norm(o_rt, rt_norm_g).reshape(b, t, RT_W) * jax.nn.silu(rt_gate)

    xq = xa_q.reshape(b, t, XA_H, XA_DH) * (XA_DH ** -0.5)
    att = jax.nn.softmax(jnp.einsum('bthd,bmhd->bhtm', xq, mem_k.astype(f32)), axis=-1)
    o_xa = jnp.einsum('bhtm,bmhd->bthd', att, mem_v.astype(f32)).reshape(b, t, XA_W) * jax.nn.silu(xa_gate)

    mixed = jnp.concatenate([o_hg, o_rt, o_xa], axis=-1)
    out = x.astype(f32) + mixed @ w_out.astype(f32)
    return out, s_hg_new, s_rt_new


def setup_inputs(seed: int = 0) -> dict:
    key = jax.random.key(seed)
    ks = jax.random.split(key, 18)
    f32 = jnp.float32

    def nrm(k, shape, scale=1.0):
        return scale * jax.random.normal(k, shape, f32)

    return {
        "x_prompt": nrm(ks[0], (BATCH, SEQ, D_MODEL)),
        "x_sample": nrm(ks[1], (DEC_BATCH, DEC_SEQ, D_MODEL)),
        "mem_prompt": nrm(ks[2], (BATCH, N_MEM, D_MODEL)),
        "state_hgrn": nrm(ks[3], (DEPTH, DEC_BATCH, HG_H, HG_DK, HG_DV)),
        "state_ret": nrm(ks[4], (DEPTH, DEC_BATCH, RT_H, RT_DK, RT_DV), 4.0),
        "cache_mem_k": nrm(ks[5], (DEPTH, DEC_BATCH, N_MEM, XA_H, XA_DH)),
        "cache_mem_v": nrm(ks[6], (DEPTH, DEC_BATCH, N_MEM, XA_H, XA_DH)),
        "norm_g": 1.0 + nrm(ks[7], (DEPTH, D_MODEL), 0.02),
        "w_in": nrm(ks[8], (DEPTH, D_MODEL, D_IN), D_MODEL ** -0.5),
        "lb_logits": nrm(ks[9], (DEPTH + 1, HG_KW), 0.1),
        "hg_norm_g": 1.0 + nrm(ks[10], (DEPTH, HG_W), 0.02),
        "rt_norm_g": 1.0 + nrm(ks[11], (DEPTH, RT_W), 0.02),
        "mem_norm_g": 1.0 + nrm(ks[12], (DEPTH, D_MODEL), 0.02),
        "w_mem_k": nrm(ks[13], (DEPTH, D_MODEL, XA_W), D_MODEL ** -0.5),
        "w_mem_v": nrm(ks[14], (DEPTH, D_MODEL, XA_W), D_MODEL ** -0.5),
        "w_out": nrm(ks[15], (DEPTH, D_MIX, D_MODEL), D_MIX ** -0.5),
        "final_norm_g": 1.0 + nrm(ks[16], (D_MODEL,), 0.02),
    }


def reference(x_prompt, x_sample, mem_prompt, state_hgrn, state_ret, cache_mem_k, cache_mem_v,
              norm_g, w_in, lb_logits, hg_norm_g, rt_norm_g, mem_norm_g, w_mem_k, w_mem_v, w_out,
              final_norm_g):
    f32 = jnp.float32
    bp, tp, _ = x_prompt.shape
    ts = x_sample.shape[1]
    pos_p = jnp.arange(tp)
    pos_s = PAST_LEN + jnp.arange(ts)
    lb_all = jnp.cumsum(jax.nn.softmax(lb_logits.astype(f32), axis=0), axis=0)
    hp, hs = x_prompt, x_sample
    hg_p, rt_p, mk_p, mv_p, hg_s, rt_s = [], [], [], [], [], []
    for l in range(DEPTH):
        mk, mv = memory_kv(mem_prompt, mem_norm_g[l], w_mem_k[l], w_mem_v[l])
        hp, shg, srt = mixer_layer(hp, pos_p, jnp.zeros((bp, HG_H, HG_DK, HG_DV), f32),
                                   jnp.zeros((bp, RT_H, RT_DK, RT_DV), f32), mk, mv,
                                   norm_g[l], w_in[l], lb_all[l], hg_norm_g[l], rt_norm_g[l], w_out[l])
        hg_p.append(shg); rt_p.append(srt); mk_p.append(mk); mv_p.append(mv)
        hs, shg2, srt2 = mixer_layer(hs, pos_s, state_hgrn[l], state_ret[l], cache_mem_k[l], cache_mem_v[l],
                                     norm_g[l], w_in[l], lb_all[l], hg_norm_g[l], rt_norm_g[l], w_out[l])
        hg_s.append(shg2); rt_s.append(srt2)
    dt = x_prompt.dtype
    y_prompt = rms_norm(hp, final_norm_g).astype(dt)
    y_sample = rms_norm(hs, final_norm_g).astype(x_sample.dtype)
    new_state_hgrn_prompt = jnp.stack(hg_p).astype(dt)
    new_state_ret_prompt = jnp.stack(rt_p).astype(dt)
    new_mem_k_prompt = jnp.stack(mk_p).astype(dt)
    new_mem_v_prompt = jnp.stack(mv_p).astype(dt)
    new_state_hgrn_sample = jnp.stack(hg_s).astype(state_hgrn.dtype)
    new_state_ret_sample = jnp.stack(rt_s).astype(state_ret.dtype)
    return (y_prompt, y_sample, new_state_hgrn_prompt, new_state_ret_prompt, new_mem_k_prompt, new_mem_v_prompt,
            new_state_hgrn_sample, new_state_ret_sample)
```

```python
import functools

import numpy as np
import jax
import jax.numpy as jnp
from jax import lax
from jax.experimental import pallas as pl
from jax.experimental.pallas import tpu as pltpu

F32 = jnp.float32
BF16 = jnp.bfloat16

D_MODEL = 1024
PAST_LEN = 4096
CHUNK = 64
N_MEM = 256
EPS = 1e-6
ROPE_BASE = 10000.0
HG_H, HG_DK, HG_DV = 8, 128, 128
HG_W = HG_H * HG_DV
RT_H, RT_DK, RT_DV = 4, 64, 128
RT_QK = RT_H * RT_DK
RT_W = RT_H * RT_DV
XA_H, XA_DH = 4, 128
XA_W = XA_H * XA_DH
D_MIX = HG_W + RT_W + XA_W
OFF_HG_Q = 0
OFF_HG_F = OFF_HG_Q + HG_W
OFF_HG_I = OFF_HG_F + HG_W
OFF_HG_GATE = OFF_HG_I + HG_W
OFF_RT_Q = OFF_HG_GATE + HG_W
OFF_RT_K = OFF_RT_Q + RT_QK
OFF_RT_V = OFF_RT_K + RT_QK
OFF_RT_GATE = OFF_RT_V + RT_W
OFF_XA_Q = OFF_RT_GATE + RT_W
OFF_XA_GATE = OFF_XA_Q + XA_W
D_IN = OFF_XA_GATE + XA_W

LANES = 128
TOKEN_BLOCK = 256
PROJ_N_TILE = 512
VMEM_LIMIT_BYTES = 56 * 1024 * 1024

_NT = (((1,), (1,)), ((), ()))
_TN = (((0,), (0,)), ((), ()))


def _silu(x):
    return x * jax.nn.sigmoid(x)


def _rms(x, g):
    return x * lax.rsqrt(jnp.mean(x * x, axis=-1, keepdims=True) + EPS) * g


def _rope_tables(pos):
    half = RT_DK // 2
    inv_freq = ROPE_BASE ** (-jnp.arange(half, dtype=F32) / half)
    ang = pos.astype(F32)[:, None] * inv_freq[None, :]
    cos, sin = jnp.cos(ang), jnp.sin(ang)
    cos_t = jnp.concatenate([cos, cos, cos, cos], axis=-1)
    sin_t = jnp.concatenate([-sin, sin, -sin, sin], axis=-1)
    return cos_t, sin_t


def _retention_tables(ts):
    h = np.arange(RT_H, dtype=np.float64)
    log_gamma = np.log1p(-np.exp2(-5.0 - h))
    n = np.arange(ts, dtype=np.float64)
    scale = RT_DK ** -0.5
    dist = np.abs(n[:, None] - n[None, :])
    visible = (n[None, :] // CHUNK) <= (n[:, None] // CHUNK)
    mdec = scale * np.exp(log_gamma[:, None, None] * dist[None]) * visible[None]
    qd = np.exp(log_gamma[:, None] * (n[None, :] + 1.0))
    kd = scale * np.exp(log_gamma[:, None] * (ts - 1.0 - n[None, :]))
    rd = np.exp(log_gamma * ts)

    def pair_lanes(a):
        a = np.repeat(a[:, :, None], RT_DK, axis=2)
        return np.concatenate([a[0::2], a[1::2]], axis=2)

    rdec = np.repeat(np.repeat(rd[:, None], RT_DK, axis=1).reshape(RT_H // 2, 2 * RT_DK, 1), LANES, axis=2)
    return (jnp.asarray(mdec, F32), jnp.asarray(pair_lanes(qd), F32), jnp.asarray(pair_lanes(kd), F32),
            jnp.asarray(rdec, F32))


def _chunk_tril(tb):
    n = np.arange(tb)
    t = (n[None, :] <= n[:, None]) & (n[None, :] // CHUNK == n[:, None] // CHUNK)
    return jnp.asarray(t, BF16)


def _layer_block(n_streams, ts, x, cos_t, sin_t, mdec_ref, qdec_ref, kdec_ref, rdec_ref, tril_ref,
                 get_mem, norm_g_ref, w_in_ref, lb_ref, hg_g_ref, rt_g_ref, w_out_ref, fin_g_ref,
                 proj_ref, gc_ref, mixed_ref, get_st, set_st, get_r, set_r):
    tb = n_streams * ts
    xn = _rms(x, norm_g_ref[...]).astype(BF16)
    for n0 in range(0, D_IN, PROJ_N_TILE):
        proj_ref[:, n0:n0 + PROJ_N_TILE] = jnp.dot(xn, w_in_ref[:, n0:n0 + PROJ_N_TILE],
                                                   preferred_element_type=F32)

    lg = lb_ref[...]
    e = jnp.exp(lg - jnp.max(lg, axis=0, keepdims=True))
    lb = e[0:1, :] / jnp.sum(e, axis=0, keepdims=True)
    tril = tril_ref[...]
    for c0 in range(0, HG_W, 2 * LANES):
        cols = slice(OFF_HG_F + c0, OFF_HG_F + c0 + 2 * LANES)
        lbc = lb[:, c0:c0 + 2 * LANES]
        sig = jax.nn.sigmoid(proj_ref[:, cols])
        g = jnp.log(lbc + (1.0 - lbc) * sig)
        proj_ref[:, cols] = (1.0 - lbc) * (1.0 - sig)
        g_hi = g.astype(BF16)
        r1 = g - g_hi.astype(F32)
        g_md = r1.astype(BF16)
        g_lo = (r1 - g_md.astype(F32)).astype(BF16)
        gc_ref[:, c0:c0 + 2 * LANES] = (jnp.dot(tril, g_hi, preferred_element_type=F32)
                                       + jnp.dot(tril, g_md, preferred_element_type=F32)
                                       + jnp.dot(tril, g_lo, preferred_element_type=F32))

    row = lax.broadcasted_iota(jnp.int32, (CHUNK, CHUNK), 0)
    col = lax.broadcasted_iota(jnp.int32, (CHUNK, CHUNK), 1)
    causal = col <= row
    mid = CHUNK // 2 - 1

    for s in range(n_streams):
        for c in range(ts // CHUNK):
            rows = slice(s * ts + c * CHUNK, s * ts + (c + 1) * CHUNK)
            for h in range(HG_H):
                hc = slice(h * HG_DK, (h + 1) * HG_DK)
                q = _silu(proj_ref[rows, OFF_HG_Q + h * HG_DK:OFF_HG_Q + (h + 1) * HG_DK])
                k = proj_ref[rows, OFF_HG_F + h * HG_DK:OFF_HG_F + (h + 1) * HG_DK]
                v = proj_ref[rows, OFF_HG_I + h * HG_DV:OFF_HG_I + (h + 1) * HG_DV].astype(BF16)
                gc = gc_ref[rows, hc]
                g_mid = gc[mid:mid + 1, :]
                g_end = gc[CHUNK - 1:CHUNK, :]
                q_rel = (q * jnp.exp(gc - g_mid)).astype(BF16)
                k_rel = (k * jnp.exp(g_mid - gc)).astype(BF16)
                q_abs = (q * jnp.exp(gc)).astype(BF16)
                k_end = (k * jnp.exp(g_end - gc)).astype(BF16)
                a = lax.dot_general(q_rel, k_rel, _NT, preferred_element_type=F32)
                a = jnp.where(causal, a, 0.0).astype(BF16)
                st = get_st(s, h)
                o = (jnp.dot(a, v, preferred_element_type=F32)
                     + lax.dot_general(q_abs, st.astype(BF16), _NT, preferred_element_type=F32))
                set_st(s, h, jnp.exp(g_end) * st + lax.dot_general(v, k_end, _TN, preferred_element_type=F32))
                gate = _silu(proj_ref[rows, OFF_HG_GATE + h * HG_DV:OFF_HG_GATE + (h + 1) * HG_DV])
                o = _rms(o, hg_g_ref[:, hc]) * gate
                mixed_ref[rows, hc] = o.astype(BF16)

    lane = lax.broadcasted_iota(jnp.int32, (ts, LANES), 1)
    first_half = (lane % RT_DK) < (RT_DK // 2)
    low_head = lane < RT_DK
    row2 = lax.broadcasted_iota(jnp.int32, (2 * RT_DK, RT_DV), 0)
    low_rows = row2 < RT_DK

    def rope(xs, cs, sn):
        swapped = jnp.where(first_half, pltpu.roll(xs, LANES - RT_DK // 2, 1), pltpu.roll(xs, RT_DK // 2, 1))
        return xs * cs + swapped * sn

    for s in range(n_streams):
        rows = slice(s * ts, (s + 1) * ts)
        cs, sn = cos_t[rows, :], sin_t[rows, :]
        for p in range(RT_H // 2):
            pc = slice(p * LANES, (p + 1) * LANES)
            qp = rope(proj_ref[rows, OFF_RT_Q + p * LANES:OFF_RT_Q + (p + 1) * LANES], cs, sn)
            kp = rope(proj_ref[rows, OFF_RT_K + p * LANES:OFF_RT_K + (p + 1) * LANES], cs, sn)
            kp_b = kp.astype(BF16)
            qd = qp * qdec_ref[p]
            kd_b = (kp * kdec_ref[p]).astype(BF16)
            r_pair = get_r(s, p)
            r_b = r_pair.astype(BF16)
            v_pair = proj_ref[rows, OFF_RT_V + 2 * p * RT_DV:OFF_RT_V + 2 * (p + 1) * RT_DV].astype(BF16)
            for i in range(2):
                h = 2 * p + i
                sel = low_head if i == 0 else jnp.logical_not(low_head)
                q_h = jnp.where(sel, qp, 0.0).astype(BF16)
                qd_h = jnp.where(sel, qd, 0.0).astype(BF16)
                v_h = v_pair[:, i * RT_DV:(i + 1) * RT_DV]
                sc = lax.dot_general(q_h, kp_b, _NT, preferred_element_type=F32) * mdec_ref[h]
                o = (jnp.dot(sc.astype(BF16), v_h, preferred_element_type=F32)
                     + jnp.dot(qd_h, r_b, preferred_element_type=F32))
                mu = jnp.mean(o, axis=-1, keepdims=True)
                cen = o - mu
                o = cen * lax.rsqrt(jnp.mean(cen * cen, axis=-1, keepdims=True) + EPS)
                gate = _silu(proj_ref[rows, OFF_RT_GATE + h * RT_DV:OFF_RT_GATE + (h + 1) * RT_DV])
                o = o * rt_g_ref[:, h * RT_DV:(h + 1) * RT_DV] * gate
                mixed_ref[rows, HG_W + h * RT_DV:HG_W + (h + 1) * RT_DV] = o.astype(BF16)
            kv = lax.dot_general(kd_b, v_pair, _TN, preferred_element_type=F32)
            dr = jnp.where(low_rows, kv[:, :RT_DV], kv[:, RT_DV:])
            set_r(s, p, rdec_ref[p] * r_pair + dr)

    xa_scale = XA_DH ** -0.5
    for s in range(n_streams):
        rows = slice(s * ts, (s + 1) * ts)
        for h in range(XA_H):
            mk_h, mv_h = get_mem(s, h)
            xq = proj_ref[rows, OFF_XA_Q + h * XA_DH:OFF_XA_Q + (h + 1) * XA_DH].astype(BF16)
            sc = lax.dot_general(xq, mk_h, _NT, preferred_element_type=F32) * xa_scale
            pr = jnp.exp(sc - jnp.max(sc, axis=-1, keepdims=True))
            den = jnp.sum(pr, axis=-1, keepdims=True)
            o = jnp.dot(pr.astype(BF16), mv_h, preferred_element_type=F32) / den
            gate = _silu(proj_ref[rows, OFF_XA_GATE + h * XA_DH:OFF_XA_GATE + (h + 1) * XA_DH])
            mixed_ref[rows, HG_W + RT_W + h * XA_DH:HG_W + RT_W + (h + 1) * XA_DH] = (o * gate).astype(BF16)

    out = x + jnp.dot(mixed_ref[...], w_out_ref[...], preferred_element_type=F32)
    return _rms(out, fin_g_ref[...])


def _memory_kv_kernel(mem_ref, g_ref, wk_ref, wv_ref, k_ref, v_ref, kb_ref, vb_ref):
    m = _rms(mem_ref[0], g_ref[...]).astype(BF16)
    k = jnp.dot(m, wk_ref[...], preferred_element_type=F32)
    v = jnp.dot(m, wv_ref[...], preferred_element_type=F32)
    k_ref[0] = k
    v_ref[0] = v
    kb_ref[0] = k.astype(BF16)
    vb_ref[0] = v.astype(BF16)


def _prompt_kernel(x_ref, cos_ref, sin_ref, mdec_ref, qdec_ref, kdec_ref, rdec_ref, tril_ref, mk_ref, mv_ref,
                   norm_g_ref, w_in_ref, lb_ref, hg_g_ref, rt_g_ref, w_out_ref, fin_g_ref,
                   y_ref, s_out_ref, r_out_ref, proj_ref, gc_ref, mixed_ref, st_ref):
    j = pl.program_id(1)

    @pl.when(j == 0)
    def _():
        st_ref[...] = jnp.zeros_like(st_ref)
        r_out_ref[...] = jnp.zeros_like(r_out_ref)

    def set_st(s, h, val):
        st_ref[h] = val

    def set_r(s, p, val):
        r_out_ref[0, p] = val

    def get_mem(s, h):
        hs = slice(h * XA_DH, (h + 1) * XA_DH)
        return mk_ref[0, :, hs], mv_ref[0, :, hs]

    y_ref[0] = _layer_block(
        1, TOKEN_BLOCK, x_ref[0], cos_ref[...], sin_ref[...], mdec_ref, qdec_ref, kdec_ref, rdec_ref, tril_ref,
        get_mem, norm_g_ref, w_in_ref, lb_ref, hg_g_ref, rt_g_ref, w_out_ref, fin_g_ref,
        proj_ref, gc_ref, mixed_ref,
        lambda s, h: st_ref[h], set_st, lambda s, p: r_out_ref[0, p], set_r)

    @pl.when(j == pl.num_programs(1) - 1)
    def _():
        for h in range(HG_H):
            s_out_ref[0, h] = st_ref[h].T


def _sample_kernel(x_ref, cos_ref, sin_ref, mdec_ref, qdec_ref, kdec_ref, rdec_ref, tril_ref, mk_ref, mv_ref,
                   s_in_ref, r_in_ref,
                   norm_g_ref, w_in_ref, lb_ref, hg_g_ref, rt_g_ref, w_out_ref, fin_g_ref,
                   y_ref, s_out_ref, r_out_ref, proj_ref, gc_ref, mixed_ref):
    n_streams = TOKEN_BLOCK // CHUNK

    def set_st(s, h, val):
        s_out_ref[s, h] = val.T

    def set_r(s, p, val):
        r_out_ref[s, p] = val

    def get_mem(s, h):
        hs = slice(h * XA_DH, (h + 1) * XA_DH)
        return mk_ref[s, :, hs].astype(BF16), mv_ref[s, :, hs].astype(BF16)

    cos_t = jnp.concatenate([cos_ref[...]] * n_streams, axis=0)
    sin_t = jnp.concatenate([sin_ref[...]] * n_streams, axis=0)
    y_ref[...] = _layer_block(
        n_streams, CHUNK, x_ref[...], cos_t, sin_t, mdec_ref, qdec_ref, kdec_ref, rdec_ref, tril_ref,
        get_mem, norm_g_ref, w_in_ref, lb_ref, hg_g_ref, rt_g_ref, w_out_ref, fin_g_ref,
        proj_ref, gc_ref, mixed_ref,
        lambda s, h: s_in_ref[s, h].T, set_st, lambda s, p: r_in_ref[s, p], set_r)


def _const_spec(shape):
    zeros = (0,) * len(shape)
    return pl.BlockSpec(shape, lambda *_: zeros, pipeline_mode=pl.Buffered(1))


def _layer_scratch():
    return [pltpu.VMEM((TOKEN_BLOCK, D_IN), F32),
            pltpu.VMEM((TOKEN_BLOCK, HG_W), F32),
            pltpu.VMEM((TOKEN_BLOCK, D_MIX), BF16)]


def _memory_kv(mem, g, wk, wv):
    b = mem.shape[0]
    out_f = jax.ShapeDtypeStruct((b, N_MEM, XA_W), F32)
    out_b = jax.ShapeDtypeStruct((b, N_MEM, XA_W), BF16)
    blk = pl.BlockSpec((1, N_MEM, XA_W), lambda i: (i, 0, 0))
    return pl.pallas_call(
        _memory_kv_kernel,
        grid=(b,),
        in_specs=[pl.BlockSpec((1, N_MEM, D_MODEL), lambda i: (i, 0, 0)),
                  pl.BlockSpec((1, D_MODEL), lambda i: (0, 0)),
                  pl.BlockSpec((D_MODEL, XA_W), lambda i: (0, 0)),
                  pl.BlockSpec((D_MODEL, XA_W), lambda i: (0, 0))],
        out_specs=[blk, blk, blk, blk],
        out_shape=[out_f, out_f, out_b, out_b],
        name="memory_kv",
    )(mem, g.reshape(1, D_MODEL), wk.astype(BF16), wv.astype(BF16))


def _prompt_layer(x, mk_b, mv_b, weights):
    b, t, _ = x.shape
    tb = TOKEN_BLOCK
    cos_t, sin_t = _rope_tables(jnp.arange(t))
    mdec, qdec, kdec, rdec = _retention_tables(tb)
    tril = _chunk_tril(tb)
    consts = (mdec, qdec, kdec, rdec, tril)
    in_specs = ([pl.BlockSpec((1, tb, D_MODEL), lambda i, j: (i, j, 0)),
                 pl.BlockSpec((tb, LANES), lambda i, j: (j, 0)),
                 pl.BlockSpec((tb, LANES), lambda i, j: (j, 0))]
                + [_const_spec(c.shape) for c in consts]
                + [pl.BlockSpec((1, N_MEM, XA_W), lambda i, j: (i, 0, 0)),
                   pl.BlockSpec((1, N_MEM, XA_W), lambda i, j: (i, 0, 0))]
                + [_const_spec(w.shape) for w in weights])
    out_shape = [jax.ShapeDtypeStruct((b, t, D_MODEL), F32),
                 jax.ShapeDtypeStruct((b, HG_H, HG_DK, HG_DV), F32),
                 jax.ShapeDtypeStruct((b, RT_H // 2, 2 * RT_DK, RT_DV), F32)]
    out_specs = [pl.BlockSpec((1, tb, D_MODEL), lambda i, j: (i, j, 0)),
                 pl.BlockSpec((1, HG_H, HG_DK, HG_DV), lambda i, j: (i, 0, 0, 0)),
                 pl.BlockSpec((1, RT_H // 2, 2 * RT_DK, RT_DV), lambda i, j: (i, 0, 0, 0))]
    return pl.pallas_call(
        _prompt_kernel,
        grid=(b, t // tb),
        in_specs=in_specs,
        out_specs=out_specs,
        out_shape=out_shape,
        scratch_shapes=_layer_scratch() + [pltpu.VMEM((HG_H, HG_DV, HG_DK), F32)],
        compiler_params=pltpu.CompilerParams(dimension_semantics=("arbitrary", "arbitrary"),
                                             vmem_limit_bytes=VMEM_LIMIT_BYTES),
        name="prompt_layer",
    )(x, cos_t, sin_t, *consts, mk_b, mv_b, *weights)


def _sample_layer(x, s_hg, s_rt, mem_k, mem_v, weights):
    b, t, _ = x.shape
    assert t == CHUNK
    tb = TOKEN_BLOCK
    nsb = tb // t
    cos_t, sin_t = _rope_tables(PAST_LEN + jnp.arange(t))
    mdec, qdec, kdec, rdec = _retention_tables(t)
    tril = _chunk_tril(tb)
    consts = (mdec, qdec, kdec, rdec, tril)
    s_rt2 = s_rt.reshape(b, RT_H // 2, 2 * RT_DK, RT_DV)
    in_specs = ([pl.BlockSpec((tb, D_MODEL), lambda i: (i, 0)),
                 _const_spec(cos_t.shape), _const_spec(sin_t.shape)]
                + [_const_spec(c.shape) for c in consts]
                + [pl.BlockSpec((nsb, N_MEM, XA_W), lambda i: (i, 0, 0)),
                   pl.BlockSpec((nsb, N_MEM, XA_W), lambda i: (i, 0, 0)),
                   pl.BlockSpec((nsb, HG_H, HG_DK, HG_DV), lambda i: (i, 0, 0, 0)),
                   pl.BlockSpec((nsb, RT_H // 2, 2 * RT_DK, RT_DV), lambda i: (i, 0, 0, 0))]
                + [_const_spec(w.shape) for w in weights])
    out_shape = [jax.ShapeDtypeStruct((b * t, D_MODEL), F32),
                 jax.ShapeDtypeStruct((b, HG_H, HG_DK, HG_DV), F32),
                 jax.ShapeDtypeStruct((b, RT_H // 2, 2 * RT_DK, RT_DV), F32)]
    out_specs = [pl.BlockSpec((tb, D_MODEL), lambda i: (i, 0)),
                 pl.BlockSpec((nsb, HG_H, HG_DK, HG_DV), lambda i: (i, 0, 0, 0)),
                 pl.BlockSpec((nsb, RT_H // 2, 2 * RT_DK, RT_DV), lambda i: (i, 0, 0, 0))]
    y, s_new, r_new = pl.pallas_call(
        _sample_kernel,
        grid=(b // nsb,),
        in_specs=in_specs,
        out_specs=out_specs,
        out_shape=out_shape,
        scratch_shapes=_layer_scratch(),
        compiler_params=pltpu.CompilerParams(dimension_semantics=("arbitrary",),
                                             vmem_limit_bytes=VMEM_LIMIT_BYTES),
        name="sample_layer",
    )(x.reshape(b * t, D_MODEL), cos_t, sin_t, *consts,
      mem_k.reshape(b, N_MEM, XA_W), mem_v.reshape(b, N_MEM, XA_W), s_hg, s_rt2, *weights)
    return y.reshape(b, t, D_MODEL), s_new, r_new


def kernel(x_prompt, x_sample, mem_prompt, state_hgrn, state_ret, cache_mem_k, cache_mem_v, norm_g, w_in,
           lb_logits, hg_norm_g, rt_norm_g, mem_norm_g, w_mem_k, w_mem_v, w_out, final_norm_g):
    depth = w_in.shape[0]
    assert depth == 1, "single-layer step"
    bp = x_prompt.shape[0]
    bs = x_sample.shape[0]
    weights = (norm_g[0].reshape(1, D_MODEL), w_in[0].astype(BF16), lb_logits,
               hg_norm_g[0].reshape(1, HG_W), rt_norm_g[0].reshape(1, RT_W), w_out[0].astype(BF16),
               final_norm_g.reshape(1, D_MODEL))
    mk, mv, mk_b, mv_b = _memory_kv(mem_prompt, mem_norm_g[0], w_mem_k[0], w_mem_v[0])
    y_p, s_p, r_p = _prompt_layer(x_prompt, mk_b, mv_b, weights)
    y_s, s_s, r_s = _sample_layer(x_sample, state_hgrn[0], state_ret[0], cache_mem_k[0], cache_mem_v[0], weights)
    return (y_p, y_s,
            s_p.reshape(1, bp, HG_H, HG_DK, HG_DV),
            r_p.reshape(1, bp, RT_H, RT_DK, RT_DV),
            mk.reshape(1, bp, N_MEM, XA_H, XA_DH),
            mv.reshape(1, bp, N_MEM, XA_H, XA_DH),
            s_s.reshape(1, bs, HG_H, HG_DK, HG_DV),
            r_s.reshape(1, bs, RT_H, RT_DK, RT_DV))
```

```python
import numpy as np
import jax
import jax.numpy as jnp
from jax import lax
from jax.experimental import pallas as pl
from jax.experimental.pallas import tpu as pltpu

F32 = jnp.float32
BF16 = jnp.bfloat16

D_MODEL = 1024
PAST_LEN = 4096
CHUNK = 64
N_MEM = 256
EPS = 1e-6
ROPE_BASE = 10000.0
HG_H, HG_DK, HG_DV = 8, 128, 128
HG_W = HG_H * HG_DV
RT_H, RT_DK, RT_DV = 4, 64, 128
RT_QK = RT_H * RT_DK
RT_W = RT_H * RT_DV
XA_H, XA_DH = 4, 128
XA_W = XA_H * XA_DH
D_MIX = HG_W + RT_W + XA_W
OFF_HG_Q = 0
OFF_HG_F = OFF_HG_Q + HG_W
OFF_HG_I = OFF_HG_F + HG_W
OFF_HG_GATE = OFF_HG_I + HG_W
OFF_RT_Q = OFF_HG_GATE + HG_W
OFF_RT_K = OFF_RT_Q + RT_QK
OFF_RT_V = OFF_RT_K + RT_QK
OFF_RT_GATE = OFF_RT_V + RT_W
OFF_XA_Q = OFF_RT_GATE + RT_W
OFF_XA_GATE = OFF_XA_Q + XA_W
D_IN = OFF_XA_GATE + XA_W

LANES = 128
TOKEN_BLOCK = 256
PROJ_N_TILE = 512
VMEM_LIMIT_BYTES = 56 * 1024 * 1024

_NT = (((1,), (1,)), ((), ()))
_TN = (((0,), (0,)), ((), ()))


def _silu(x):
    return x * jax.nn.sigmoid(x)


def _rms(x, g):
    return x * lax.rsqrt(jnp.mean(x * x, axis=-1, keepdims=True) + EPS) * g


def _cols(off, i, width):
    return slice(off + i * width, off + (i + 1) * width)


def _rope_tables(pos):
    half = RT_DK // 2
    inv_freq = ROPE_BASE ** (-np.arange(half, dtype=np.float64) / half)
    ang = np.asarray(pos, np.float64)[:, None] * inv_freq[None, :]
    cos, sin = np.cos(ang), np.sin(ang)
    cos_t = np.concatenate([cos, cos, cos, cos], axis=-1)
    sin_t = np.concatenate([-sin, sin, -sin, sin], axis=-1)
    return jnp.asarray(cos_t, F32), jnp.asarray(sin_t, F32)


def _retention_tables(ts):
    h = np.arange(RT_H, dtype=np.float64)
    log_gamma = np.log1p(-np.exp2(-5.0 - h))
    n = np.arange(ts, dtype=np.float64)
    scale = RT_DK ** -0.5
    dist = np.abs(n[:, None] - n[None, :])
    visible = (n[None, :] // CHUNK) <= (n[:, None] // CHUNK)
    mdec = scale * np.exp(log_gamma[:, None, None] * dist[None]) * visible[None]
    qd = np.exp(log_gamma[:, None] * (n[None, :] + 1.0))
    kd = scale * np.exp(log_gamma[:, None] * (ts - 1.0 - n[None, :]))
    rd = np.exp(log_gamma * ts)

    def pair_lanes(a):
        a = np.repeat(a[:, :, None], RT_DK, axis=2)
        return np.concatenate([a[0::2], a[1::2]], axis=2)

    rdec = np.repeat(np.repeat(rd[:, None], RT_DK, axis=1).reshape(RT_H // 2, 2 * RT_DK, 1), LANES, axis=2)
    return (jnp.asarray(mdec, F32), jnp.asarray(pair_lanes(qd), F32), jnp.asarray(pair_lanes(kd), F32),
            jnp.asarray(rdec, F32))


def _chunk_tril(tb):
    n = np.arange(tb)
    t = (n[None, :] <= n[:, None]) & (n[None, :] // CHUNK == n[:, None] // CHUNK)
    return jnp.asarray(t, BF16)


def _hgrn_chunk(proj_ref, gc_ref, mixed_ref, hg_g_ref, rows, causal, get_st, set_st):
    mid = CHUNK // 2 - 1
    heads = range(HG_H)
    q_rel, k_rel, q_abs, k_end, v_b, dec = [], [], [], [], [], []
    for h in heads:
        q = _silu(proj_ref[rows, _cols(OFF_HG_Q, h, HG_DK)])
        k = proj_ref[rows, _cols(OFF_HG_F, h, HG_DK)]
        gc = gc_ref[rows, _cols(0, h, HG_DK)]
        g_mid = gc[mid:mid + 1, :]
        g_end = gc[CHUNK - 1:CHUNK, :]
        q_rel.append((q * jnp.exp(gc - g_mid)).astype(BF16))
        k_rel.append((k * jnp.exp(g_mid - gc)).astype(BF16))
        q_abs.append((q * jnp.exp(gc)).astype(BF16))
        k_end.append((k * jnp.exp(g_end - gc)).astype(BF16))
        v_b.append(proj_ref[rows, _cols(OFF_HG_I, h, HG_DV)].astype(BF16))
        dec.append(jnp.exp(g_end))
    a = [lax.dot_general(q_rel[h], k_rel[h], _NT, preferred_element_type=F32) for h in heads]
    st = [get_st(h) for h in heads]
    inter = [lax.dot_general(q_abs[h], st[h].astype(BF16), _NT, preferred_element_type=F32) for h in heads]
    kv = [lax.dot_general(v_b[h], k_end[h], _TN, preferred_element_type=F32) for h in heads]
    for h in heads:
        set_st(h, dec[h] * st[h] + kv[h])
    a_m = [jnp.where(causal, a[h], 0.0).astype(BF16) for h in heads]
    o = [jnp.dot(a_m[h], v_b[h], preferred_element_type=F32) + inter[h] for h in heads]
    ms = [jnp.mean(o[h] * o[h], axis=-1, keepdims=True) for h in heads]
    for h in heads:
        gate = _silu(proj_ref[rows, _cols(OFF_HG_GATE, h, HG_DV)])
        on = o[h] * lax.rsqrt(ms[h] + EPS) * hg_g_ref[:, _cols(0, h, HG_DV)] * gate
        mixed_ref[rows, _cols(0, h, HG_DV)] = on.astype(BF16)


def _retention(n_streams, ts, proj_ref, mixed_ref, cos_t, sin_t, mdec_ref, qdec_ref, kdec_ref, rdec_ref,
               rt_g_ref, get_r, set_r):
    lane = lax.broadcasted_iota(jnp.int32, (ts, LANES), 1)
    first_half = (lane % RT_DK) < (RT_DK // 2)
    low_head = lane < RT_DK
    low_rows = lax.broadcasted_iota(jnp.int32, (2 * RT_DK, RT_DV), 0) < RT_DK

    def rope(xs, cs, sn):
        swapped = jnp.where(first_half, pltpu.roll(xs, LANES - RT_DK // 2, 1), pltpu.roll(xs, RT_DK // 2, 1))
        return xs * cs + swapped * sn

    pairs = [(s, p) for s in range(n_streams) for p in range(RT_H // 2)]
    insts = [(s, p, i) for (s, p) in pairs for i in range(2)]
    rows = {s: slice(s * ts, (s + 1) * ts) for s in range(n_streams)}
    qp, kp_b, qd, kd_b, r_pair, v_pair = {}, {}, {}, {}, {}, {}
    for (s, p) in pairs:
        cs, sn = cos_t[rows[s], :], sin_t[rows[s], :]
        q = rope(proj_ref[rows[s], _cols(OFF_RT_Q, p, LANES)], cs, sn)
        k = rope(proj_ref[rows[s], _cols(OFF_RT_K, p, LANES)], cs, sn)
        qp[s, p] = q
        qd[s, p] = q * qdec_ref[p]
        kp_b[s, p] = k.astype(BF16)
        kd_b[s, p] = (k * kdec_ref[p]).astype(BF16)
        r_pair[s, p] = get_r(s, p)
        v_pair[s, p] = proj_ref[rows[s], _cols(OFF_RT_V, p, 2 * RT_DV)].astype(BF16)
    sel = {0: low_head, 1: jnp.logical_not(low_head)}
    q_h = {(s, p, i): jnp.where(sel[i], qp[s, p], 0.0).astype(BF16) for (s, p, i) in insts}
    qd_h = {(s, p, i): jnp.where(sel[i], qd[s, p], 0.0).astype(BF16) for (s, p, i) in insts}
    sc = {(s, p, i): lax.dot_general(q_h[s, p, i], kp_b[s, p], _NT, preferred_element_type=F32)
          for (s, p, i) in insts}
    inter = {(s, p, i): jnp.dot(qd_h[s, p, i], r_pair[s, p].astype(BF16), preferred_element_type=F32)
             for (s, p, i) in insts}
    kv = {(s, p): lax.dot_general(kd_b[s, p], v_pair[s, p], _TN, preferred_element_type=F32) for (s, p) in pairs}
    for (s, p) in pairs:
        dr = jnp.where(low_rows, kv[s, p][:, :RT_DV], kv[s, p][:, RT_DV:])
        set_r(s, p, rdec_ref[p] * r_pair[s, p] + dr)
    sc_b = {(s, p, i): (sc[s, p, i] * mdec_ref[2 * p + i]).astype(BF16) for (s, p, i) in insts}
    o = {(s, p, i): jnp.dot(sc_b[s, p, i], v_pair[s, p][:, i * RT_DV:(i + 1) * RT_DV],
                            preferred_element_type=F32) + inter[s, p, i] for (s, p, i) in insts}
    mu = {k: jnp.mean(o[k], axis=-1, keepdims=True) for k in insts}
    cen = {k: o[k] - mu[k] for k in insts}
    var = {k: jnp.mean(cen[k] * cen[k], axis=-1, keepdims=True) for k in insts}
    for (s, p, i) in insts:
        h = 2 * p + i
        gate = _silu(proj_ref[rows[s], _cols(OFF_RT_GATE, h, RT_DV)])
        on = cen[s, p, i] * lax.rsqrt(var[s, p, i] + EPS) * rt_g_ref[:, _cols(0, h, RT_DV)] * gate
        mixed_ref[rows[s], _cols(HG_W, h, RT_DV)] = on.astype(BF16)


def _cross_attention(n_streams, ts, proj_ref, mixed_ref, get_mem):
    xa_scale = XA_DH ** -0.5
    insts = [(s, h) for s in range(n_streams) for h in range(XA_H)]
    rows = {s: slice(s * ts, (s + 1) * ts) for s in range(n_streams)}
    mem = {k: get_mem(*k) for k in insts}
    sc = {(s, h): lax.dot_general(proj_ref[rows[s], _cols(OFF_XA_Q, h, XA_DH)].astype(BF16), mem[s, h][0], _NT,
                                  preferred_element_type=F32) * xa_scale for (s, h) in insts}
    mx = {k: jnp.max(sc[k], axis=-1, keepdims=True) for k in insts}
    pr = {k: jnp.exp(sc[k] - mx[k]) for k in insts}
    den = {k: jnp.sum(pr[k], axis=-1, keepdims=True) for k in insts}
    o = {k: jnp.dot(pr[k].astype(BF16), mem[k][1], preferred_element_type=F32) for k in insts}
    for (s, h) in insts:
        gate = _silu(proj_ref[rows[s], _cols(OFF_XA_GATE, h, XA_DH)])
        mixed_ref[rows[s], _cols(HG_W + RT_W, h, XA_DH)] = (o[s, h] / den[s, h] * gate).astype(BF16)


def _layer_block(n_streams, ts, x, cos_t, sin_t, mdec_ref, qdec_ref, kdec_ref, rdec_ref, tril_ref,
                 get_mem, norm_g_ref, w_in_ref, lb_ref, hg_g_ref, rt_g_ref, w_out_ref, fin_g_ref,
                 proj_ref, gc_ref, mixed_ref, get_st, set_st, get_r, set_r):
    xn = _rms(x, norm_g_ref[...]).astype(BF16)
    for n0 in range(0, D_IN, PROJ_N_TILE):
        proj_ref[:, n0:n0 + PROJ_N_TILE] = jnp.dot(xn, w_in_ref[:, n0:n0 + PROJ_N_TILE],
                                                   preferred_element_type=F32)

    lg = lb_ref[...]
    e = jnp.exp(lg - jnp.max(lg, axis=0, keepdims=True))
    lb = e[0:1, :] / jnp.sum(e, axis=0, keepdims=True)
    tril = tril_ref[...]
    for c0 in range(0, HG_W, 2 * LANES):
        cols = slice(OFF_HG_F + c0, OFF_HG_F + c0 + 2 * LANES)
        lbc = lb[:, c0:c0 + 2 * LANES]
        sig = jax.nn.sigmoid(proj_ref[:, cols])
        g = jnp.log(lbc + (1.0 - lbc) * sig)
        proj_ref[:, cols] = (1.0 - lbc) * (1.0 - sig)
        g_hi = g.astype(BF16)
        r1 = g - g_hi.astype(F32)
        g_md = r1.astype(BF16)
        g_lo = (r1 - g_md.astype(F32)).astype(BF16)
        gc_ref[:, c0:c0 + 2 * LANES] = (jnp.dot(tril, g_hi, preferred_element_type=F32)
                                       + jnp.dot(tril, g_md, preferred_element_type=F32)
                                       + jnp.dot(tril, g_lo, preferred_element_type=F32))

    causal = (lax.broadcasted_iota(jnp.int32, (CHUNK, CHUNK), 1)
              <= lax.broadcasted_iota(jnp.int32, (CHUNK, CHUNK), 0))
    for s in range(n_streams):
        for c in range(ts // CHUNK):
            rows = slice(s * ts + c * CHUNK, s * ts + (c + 1) * CHUNK)
            _hgrn_chunk(proj_ref, gc_ref, mixed_ref, hg_g_ref, rows, causal,
                        lambda h, s=s: get_st(s, h), lambda h, val, s=s: set_st(s, h, val))

    _retention(n_streams, ts, proj_ref, mixed_ref, cos_t, sin_t, mdec_ref, qdec_ref, kdec_ref, rdec_ref,
               rt_g_ref, get_r, set_r)
    _cross_attention(n_streams, ts, proj_ref, mixed_ref, get_mem)

    out = x + jnp.dot(mixed_ref[...], w_out_ref[...], preferred_element_type=F32)
    return _rms(out, fin_g_ref[...])


def _memory_kv_kernel(mem_ref, g_ref, wk_ref, wv_ref, k_ref, v_ref, kb_ref, vb_ref):
    m = _rms(mem_ref[0], g_ref[...]).astype(BF16)
    k = jnp.dot(m, wk_ref[...], preferred_element_type=F32)
    v = jnp.dot(m, wv_ref[...], preferred_element_type=F32)
    for h in range(XA_H):
        k_ref[0, pl.ds(h, N_MEM, stride=XA_H), :] = k[:, _cols(0, h, XA_DH)]
        v_ref[0, pl.ds(h, N_MEM, stride=XA_H), :] = v[:, _cols(0, h, XA_DH)]
    kb_ref[0] = k.astype(BF16)
    vb_ref[0] = v.astype(BF16)


def _prompt_kernel(x_ref, cos_ref, sin_ref, mdec_ref, qdec_ref, kdec_ref, rdec_ref, tril_ref, mk_ref, mv_ref,
                   norm_g_ref, w_in_ref, lb_ref, hg_g_ref, rt_g_ref, w_out_ref, fin_g_ref,
                   y_ref, s_out_ref, r_out_ref, proj_ref, gc_ref, mixed_ref, st_ref):
    j = pl.program_id(1)

    @pl.when(j == 0)
    def _():
        st_ref[...] = jnp.zeros_like(st_ref)
        r_out_ref[...] = jnp.zeros_like(r_out_ref)

    def set_st(s, h, val):
        st_ref[h] = val

    def set_r(s, p, val):
        r_out_ref[0, p] = val

    def get_mem(s, h):
        return mk_ref[0, :, _cols(0, h, XA_DH)], mv_ref[0, :, _cols(0, h, XA_DH)]

    y_ref[0] = _layer_block(
        1, TOKEN_BLOCK, x_ref[0], cos_ref[...], sin_ref[...], mdec_ref, qdec_ref, kdec_ref, rdec_ref, tril_ref,
        get_mem, norm_g_ref, w_in_ref, lb_ref, hg_g_ref, rt_g_ref, w_out_ref, fin_g_ref,
        proj_ref, gc_ref, mixed_ref,
        lambda s, h: st_ref[h], set_st, lambda s, p: r_out_ref[0, p], set_r)

    @pl.when(j == pl.num_programs(1) - 1)
    def _():
        for h in range(HG_H):
            s_out_ref[0, h] = st_ref[h].T


def _sample_kernel(x_ref, cos_ref, sin_ref, mdec_ref, qdec_ref, kdec_ref, rdec_ref, tril_ref, mk_ref, mv_ref,
                   s_in_ref, r_in_ref,
                   norm_g_ref, w_in_ref, lb_ref, hg_g_ref, rt_g_ref, w_out_ref, fin_g_ref,
                   y_ref, s_out_ref, r_out_ref, proj_ref, gc_ref, mixed_ref):
    n_streams = TOKEN_BLOCK // CHUNK

    def set_st(s, h, val):
        s_out_ref[s, h] = val.T

    def set_r(s, p, val):
        r_out_ref[s, p] = val

    def get_mem(s, h):
        head_rows = pl.ds(h, N_MEM, stride=XA_H)
        return mk_ref[s, head_rows, :].astype(BF16), mv_ref[s, head_rows, :].astype(BF16)

    cos_t = jnp.concatenate([cos_ref[...]] * n_streams, axis=0)
    sin_t = jnp.concatenate([sin_ref[...]] * n_streams, axis=0)
    y_ref[...] = _layer_block(
        n_streams, CHUNK, x_ref[...], cos_t, sin_t, mdec_ref, qdec_ref, kdec_ref, rdec_ref, tril_ref,
        get_mem, norm_g_ref, w_in_ref, lb_ref, hg_g_ref, rt_g_ref, w_out_ref, fin_g_ref,
        proj_ref, gc_ref, mixed_ref,
        lambda s, h: s_in_ref[s, h].T, set_st, lambda s, p: r_in_ref[s, p], set_r)


def _const_spec(shape):
    zeros = (0,) * len(shape)
    return pl.BlockSpec(shape, lambda *_: zeros, pipeline_mode=pl.Buffered(1))


def _layer_scratch():
    return [pltpu.VMEM((TOKEN_BLOCK, D_IN), F32),
            pltpu.VMEM((TOKEN_BLOCK, HG_W), F32),
            pltpu.VMEM((TOKEN_BLOCK, D_MIX), BF16)]


def _memory_kv(mem, g, wk, wv):
    b = mem.shape[0]
    out_f = jax.ShapeDtypeStruct((b, N_MEM * XA_H, XA_DH), F32)
    out_b = jax.ShapeDtypeStruct((b, N_MEM, XA_W), BF16)
    blk_f = pl.BlockSpec((1, N_MEM * XA_H, XA_DH), lambda i: (i, 0, 0))
    blk_b = pl.BlockSpec((1, N_MEM, XA_W), lambda i: (i, 0, 0))
    return pl.pallas_call(
        _memory_kv_kernel,
        grid=(b,),
        in_specs=[pl.BlockSpec((1, N_MEM, D_MODEL), lambda i: (i, 0, 0)),
                  pl.BlockSpec((1, D_MODEL), lambda i: (0, 0)),
                  pl.BlockSpec((D_MODEL, XA_W), lambda i: (0, 0)),
                  pl.BlockSpec((D_MODEL, XA_W), lambda i: (0, 0))],
        out_specs=[blk_f, blk_f, blk_b, blk_b],
        out_shape=[out_f, out_f, out_b, out_b],
        name="memory_kv",
    )(mem, g.reshape(1, D_MODEL), wk.astype(BF16), wv.astype(BF16))


def _prompt_layer(x, mk_b, mv_b, weights):
    b, t, _ = x.shape
    tb = TOKEN_BLOCK
    cos_t, sin_t = _rope_tables(np.arange(t))
    mdec, qdec, kdec, rdec = _retention_tables(tb)
    tril = _chunk_tril(tb)
    consts = (mdec, qdec, kdec, rdec, tril)
    in_specs = ([pl.BlockSpec((1, tb, D_MODEL), lambda i, j: (i, j, 0)),
                 pl.BlockSpec((tb, LANES), lambda i, j: (j, 0)),
                 pl.BlockSpec((tb, LANES), lambda i, j: (j, 0))]
                + [_const_spec(c.shape) for c in consts]
                + [pl.BlockSpec((1, N_MEM, XA_W), lambda i, j: (i, 0, 0)),
                   pl.BlockSpec((1, N_MEM, XA_W), lambda i, j: (i, 0, 0))]
                + [_const_spec(w.shape) for w in weights])
    out_shape = [jax.ShapeDtypeStruct((b, t, D_MODEL), F32),
                 jax.ShapeDtypeStruct((b, HG_H, HG_DK, HG_DV), F32),
                 jax.ShapeDtypeStruct((b, RT_H // 2, 2 * RT_DK, RT_DV), F32)]
    out_specs = [pl.BlockSpec((1, tb, D_MODEL), lambda i, j: (i, j, 0)),
                 pl.BlockSpec((1, HG_H, HG_DK, HG_DV), lambda i, j: (i, 0, 0, 0)),
                 pl.BlockSpec((1, RT_H // 2, 2 * RT_DK, RT_DV), lambda i, j: (i, 0, 0, 0))]
    return pl.pallas_call(
        _prompt_kernel,
        grid=(b, t // tb),
        in_specs=in_specs,
        out_specs=out_specs,
        out_shape=out_shape,
        scratch_shapes=_layer_scratch() + [pltpu.VMEM((HG_H, HG_DV, HG_DK), F32)],
        compiler_params=pltpu.CompilerParams(dimension_semantics=("arbitrary", "arbitrary"),
                                             vmem_limit_bytes=VMEM_LIMIT_BYTES),
        name="prompt_layer",
    )(x, cos_t, sin_t, *consts, mk_b, mv_b, *weights)


def _sample_layer(x, s_hg, s_rt, mem_k, mem_v, weights):
    b, t, _ = x.shape
    assert t == CHUNK
    tb = TOKEN_BLOCK
    nsb = tb // t
    cos_t, sin_t = _rope_tables(PAST_LEN + np.arange(t))
    mdec, qdec, kdec, rdec = _retention_tables(t)
    tril = _chunk_tril(tb)
    consts = (mdec, qdec, kdec, rdec, tril)
    s_rt2 = s_rt.reshape(b, RT_H // 2, 2 * RT_DK, RT_DV)
    mem_spec = pl.BlockSpec((nsb, N_MEM * XA_H, XA_DH), lambda i: (i, 0, 0))
    in_specs = ([pl.BlockSpec((tb, D_MODEL), lambda i: (i, 0)),
                 _const_spec(cos_t.shape), _const_spec(sin_t.shape)]
                + [_const_spec(c.shape) for c in consts]
                + [mem_spec, mem_spec,
                   pl.BlockSpec((nsb, HG_H, HG_DK, HG_DV), lambda i: (i, 0, 0, 0)),
                   pl.BlockSpec((nsb, RT_H // 2, 2 * RT_DK, RT_DV), lambda i: (i, 0, 0, 0))]
                + [_const_spec(w.shape) for w in weights])
    out_shape = [jax.ShapeDtypeStruct((b * t, D_MODEL), F32),
                 jax.ShapeDtypeStruct((b, HG_H, HG_DK, HG_DV), F32),
                 jax.ShapeDtypeStruct((b, RT_H // 2, 2 * RT_DK, RT_DV), F32)]
    out_specs = [pl.BlockSpec((tb, D_MODEL), lambda i: (i, 0)),
                 pl.BlockSpec((nsb, HG_H, HG_DK, HG_DV), lambda i: (i, 0, 0, 0)),
                 pl.BlockSpec((nsb, RT_H // 2, 2 * RT_DK, RT_DV), lambda i: (i, 0, 0, 0))]
    y, s_new, r_new = pl.pallas_call(
        _sample_kernel,
        grid=(b // nsb,),
        in_specs=in_specs,
        out_specs=out_specs,
        out_shape=out_shape,
        scratch_shapes=_layer_scratch(),
        compiler_params=pltpu.CompilerParams(dimension_semantics=("arbitrary",),
                                             vmem_limit_bytes=VMEM_LIMIT_BYTES),
        name="sample_layer",
    )(x.reshape(b * t, D_MODEL), cos_t, sin_t, *consts,
      mem_k.reshape(b, N_MEM * XA_H, XA_DH), mem_v.reshape(b, N_MEM * XA_H, XA_DH), s_hg, s_rt2, *weights)
    return y.reshape(b, t, D_MODEL), s_new, r_new


def kernel(x_prompt, x_sample, mem_prompt, state_hgrn, state_ret, cache_mem_k, cache_mem_v, norm_g, w_in,
           lb_logits, hg_norm_g, rt_norm_g, mem_norm_g, w_mem_k, w_mem_v, w_out, final_norm_g):
    depth = w_in.shape[0]
    assert depth == 1, "single-layer step"
    bp = x_prompt.shape[0]
    bs = x_sample.shape[0]
    weights = (norm_g[0].reshape(1, D_MODEL), w_in[0].astype(BF16), lb_logits,
               hg_norm_g[0].reshape(1, HG_W), rt_norm_g[0].reshape(1, RT_W), w_out[0].astype(BF16),
               final_norm_g.reshape(1, D_MODEL))
    mk, mv, mk_b, mv_b = _memory_kv(mem_prompt, mem_norm_g[0], w_mem_k[0], w_mem_v[0])
    y_p, s_p, r_p = _prompt_layer(x_prompt, mk_b, mv_b, weights)
    y_s, s_s, r_s = _sample_layer(x_sample, state_hgrn[0], state_ret[0], cache_mem_k[0], cache_mem_v[0], weights)
    return (y_p, y_s,
            s_p.reshape(1, bp, HG_H, HG_DK, HG_DV),
            r_p.reshape(1, bp, RT_H, RT_DK, RT_DV),
            mk.reshape(1, bp, N_MEM, XA_H, XA_DH),
            mv.reshape(1, bp, N_MEM, XA_H, XA_DH),
            s_s.reshape(1, bs, HG_H, HG_DK, HG_DV),
            r_s.reshape(1, bs, RT_H, RT_DK, RT_DV))
```

```python
import functools

import numpy as np
import jax
import jax.numpy as jnp
from jax import lax
from jax.experimental import pallas as pl
from jax.experimental.pallas import tpu as pltpu

F32 = jnp.float32
BF16 = jnp.bfloat16

D_MODEL = 1024
PAST_LEN = 4096
CHUNK = 64
N_MEM = 256
EPS = 1e-6
ROPE_BASE = 10000.0
HG_H, HG_DK, HG_DV = 8, 128, 128
HG_W = HG_H * HG_DV
RT_H, RT_DK, RT_DV = 4, 64, 128
RT_QK = RT_H * RT_DK
RT_W = RT_H * RT_DV
XA_H, XA_DH = 4, 128
XA_W = XA_H * XA_DH
D_MIX = HG_W + RT_W + XA_W
OFF_HG_Q = 0
OFF_HG_F = OFF_HG_Q + HG_W
OFF_HG_I = OFF_HG_F + HG_W
OFF_HG_GATE = OFF_HG_I + HG_W
OFF_RT_Q = OFF_HG_GATE + HG_W
OFF_RT_K = OFF_RT_Q + RT_QK
OFF_RT_V = OFF_RT_K + RT_QK
OFF_RT_GATE = OFF_RT_V + RT_W
OFF_XA_Q = OFF_RT_GATE + RT_W
OFF_XA_GATE = OFF_XA_Q + XA_W
D_IN = OFF_XA_GATE + XA_W

LANES = 128
SUBLANES = 8
TOKEN_BLOCK = 256
PROJ_N_TILE = 512
PROJ_TILES_PER_GATE_GROUP = 2
PROJ_TILES_PER_HGRN_STAGE = 1
OUT_N_TILE = 512
VMEM_LIMIT_BYTES = 56 * 1024 * 1024

_NT = (((1,), (1,)), ((), ()))
_TN = (((0,), (0,)), ((), ()))


def _silu(x):
    h = 0.5 * x
    return h + h * jnp.tanh(h)


def _rms(x, g):
    return x * lax.rsqrt(jnp.mean(x * x, axis=-1, keepdims=True) + EPS) * g


def _cols(off, i, width):
    return slice(off + i * width, off + (i + 1) * width)


def _rope_tables(pos):
    half = RT_DK // 2
    inv_freq = ROPE_BASE ** (-np.arange(half, dtype=np.float64) / half)
    ang = np.asarray(pos, np.float64)[:, None] * inv_freq[None, :]
    cos, sin = np.cos(ang), np.sin(ang)
    cos_t = np.concatenate([cos, cos, cos, cos], axis=-1)
    sin_t = np.concatenate([-sin, sin, -sin, sin], axis=-1)
    return jnp.asarray(cos_t, F32), jnp.asarray(sin_t, F32)


def _retention_tables(ts):
    h = np.arange(RT_H, dtype=np.float64)
    log_gamma = np.log1p(-np.exp2(-5.0 - h))
    n = np.arange(ts, dtype=np.float64)
    scale = RT_DK ** -0.5
    dist = np.abs(n[:, None] - n[None, :])
    visible = (n[None, :] // CHUNK) <= (n[:, None] // CHUNK)
    mdec = scale * np.exp(log_gamma[:, None, None] * dist[None]) * visible[None]
    qd = np.exp(log_gamma[:, None] * (n[None, :] + 1.0))
    kd = scale * np.exp(log_gamma[:, None] * (ts - 1.0 - n[None, :]))
    rd = np.exp(log_gamma * ts)

    def pair_lanes(a):
        a = np.repeat(a[:, :, None], RT_DK, axis=2)
        return np.concatenate([a[0::2], a[1::2]], axis=2)

    rdec = np.repeat(np.repeat(rd[:, None], RT_DK, axis=1).reshape(RT_H // 2, 2 * RT_DK, 1), LANES, axis=2)
    return (jnp.asarray(mdec, F32), jnp.asarray(pair_lanes(qd), F32), jnp.asarray(pair_lanes(kd), F32),
            jnp.asarray(rdec, F32))


def _hgrn_chunk(proj_ref, gc_ref, mixed_ref, hg_g_ref, rows, causal, get_st, set_st, proj):
    mid = CHUNK // 2 - 1
    heads = range(HG_H)
    q_rel, k_rel, q_abs, k_end, v_b, dec = [], [], [], [], [], []
    for h in heads:
        q = _silu(proj_ref[rows, _cols(OFF_HG_Q, h, HG_DK)])
        k = proj_ref[rows, _cols(OFF_HG_F, h, HG_DK)]
        gc = gc_ref[rows, _cols(0, h, HG_DK)]
        g_mid = gc[mid:mid + 1, :]
        g_end = gc[CHUNK - 1:CHUNK, :]
        q_rel.append((q * jnp.exp(gc - g_mid)).astype(BF16))
        k_rel.append((k * jnp.exp(g_mid - gc)).astype(BF16))
        q_abs.append((q * jnp.exp(gc)).astype(BF16))
        k_end.append((k * jnp.exp(g_end - gc)).astype(BF16))
        v_b.append(proj_ref[rows, _cols(OFF_HG_I, h, HG_DV)].astype(BF16))
        dec.append(jnp.exp(g_end))
    proj.emit(PROJ_TILES_PER_HGRN_STAGE)
    k_pad = jnp.zeros((LANES - CHUNK, HG_DK), BF16)
    a = [lax.dot_general(q_rel[h], jnp.concatenate([k_rel[h], k_pad], axis=0), _NT, preferred_element_type=F32)
         for h in heads]
    st = [get_st(h) for h in heads]
    v_t = [jnp.concatenate([v_b[h], jnp.zeros((LANES - CHUNK, HG_DV), BF16)], axis=0).T for h in heads]
    kv = [jnp.dot(v_t[h][:, :CHUNK], k_end[h], preferred_element_type=F32) for h in heads]
    for h in heads:
        set_st(h, dec[h] * st[h] + kv[h])
    a_m = [jnp.where(causal, a[h], 0.0).astype(BF16) for h in heads]
    o = [lax.dot_general(jnp.concatenate([q_abs[h], a_m[h]], axis=1),
                         jnp.concatenate([st[h].astype(BF16), v_t[h]], axis=1), _NT, preferred_element_type=F32)
         for h in heads]
    ms = [jnp.mean(o[h] * o[h], axis=-1, keepdims=True) for h in heads]
    proj.emit(PROJ_TILES_PER_HGRN_STAGE)
    for h in heads:
        gate = _silu(proj_ref[rows, _cols(OFF_HG_GATE, h, HG_DV)])
        on = o[h] * lax.rsqrt(ms[h] + EPS) * hg_g_ref[:, _cols(0, h, HG_DV)] * gate
        mixed_ref[rows, _cols(0, h, HG_DV)] = on.astype(BF16)


def _retention(n_streams, ts, proj_ref, mixed_ref, cos_t, sin_t, mdec_ref, qdec_ref, kdec_ref, rdec_ref,
               rt_g_ref, get_r, set_r):
    lane = lax.broadcasted_iota(jnp.int32, (ts, LANES), 1)
    first_half = (lane % RT_DK) < (RT_DK // 2)
    low_head = lane < RT_DK
    low_rows = lax.broadcasted_iota(jnp.int32, (2 * RT_DK, RT_DV), 0) < RT_DK

    def rope(xs, cs, sn):
        swapped = jnp.where(first_half, pltpu.roll(xs, LANES - RT_DK // 2, 1), pltpu.roll(xs, RT_DK // 2, 1))
        return xs * cs + swapped * sn

    pairs = [(s, p) for s in range(n_streams) for p in range(RT_H // 2)]
    insts = [(s, p, i) for (s, p) in pairs for i in range(2)]
    rows = {s: slice(s * ts, (s + 1) * ts) for s in range(n_streams)}
    qp, kp_b, qd, kd_b, r_pair, v_pair = {}, {}, {}, {}, {}, {}
    for (s, p) in pairs:
        cs, sn = cos_t[rows[s], :], sin_t[rows[s], :]
        q = rope(proj_ref[rows[s], _cols(OFF_RT_Q, p, LANES)], cs, sn)
        k = rope(proj_ref[rows[s], _cols(OFF_RT_K, p, LANES)], cs, sn)
        qp[s, p] = q
        qd[s, p] = q * qdec_ref[p]
        kp_b[s, p] = k.astype(BF16)
        kd_b[s, p] = (k * kdec_ref[p]).astype(BF16)
        r_pair[s, p] = get_r(s, p)
        v_pair[s, p] = proj_ref[rows[s], _cols(OFF_RT_V, p, 2 * RT_DV)].astype(BF16)
    yield
    sel = {0: low_head, 1: jnp.logical_not(low_head)}
    q_h = {(s, p, i): jnp.where(sel[i], qp[s, p], 0.0).astype(BF16) for (s, p, i) in insts}
    qd_h = {(s, p, i): jnp.where(sel[i], qd[s, p], 0.0).astype(BF16) for (s, p, i) in insts}
    sc = {(s, p, i): lax.dot_general(q_h[s, p, i], kp_b[s, p], _NT, preferred_element_type=F32)
          for (s, p, i) in insts}
    inter = {(s, p, i): jnp.dot(qd_h[s, p, i], r_pair[s, p].astype(BF16), preferred_element_type=F32)
             for (s, p, i) in insts}
    kv = {(s, p): lax.dot_general(kd_b[s, p], v_pair[s, p], _TN, preferred_element_type=F32) for (s, p) in pairs}
    yield
    for (s, p) in pairs:
        dr = jnp.where(low_rows, kv[s, p][:, :RT_DV], kv[s, p][:, RT_DV:])
        set_r(s, p, rdec_ref[p] * r_pair[s, p] + dr)
    sc_b = {(s, p, i): (sc[s, p, i] * mdec_ref[2 * p + i]).astype(BF16) for (s, p, i) in insts}
    yield
    o = {(s, p, i): jnp.dot(sc_b[s, p, i], v_pair[s, p][:, i * RT_DV:(i + 1) * RT_DV],
                            preferred_element_type=F32) + inter[s, p, i] for (s, p, i) in insts}
    yield
    mu = {k: jnp.mean(o[k], axis=-1, keepdims=True) for k in insts}
    cen = {k: o[k] - mu[k] for k in insts}
    var = {k: jnp.mean(cen[k] * cen[k], axis=-1, keepdims=True) for k in insts}
    yield
    for (s, p, i) in insts:
        h = 2 * p + i
        gate = _silu(proj_ref[rows[s], _cols(OFF_RT_GATE, h, RT_DV)])
        on = cen[s, p, i] * lax.rsqrt(var[s, p, i] + EPS) * rt_g_ref[:, _cols(0, h, RT_DV)] * gate
        mixed_ref[rows[s], _cols(HG_W, h, RT_DV)] = on.astype(BF16)


def _cross_attention(n_streams, ts, proj_ref, mixed_ref, get_mem):
    xa_scale = XA_DH ** -0.5
    insts = [(s, h) for s in range(n_streams) for h in range(XA_H)]
    rows = {s: slice(s * ts, (s + 1) * ts) for s in range(n_streams)}
    mem = {k: get_mem(*k) for k in insts}
    sc = {(s, h): lax.dot_general(proj_ref[rows[s], _cols(OFF_XA_Q, h, XA_DH)].astype(BF16), mem[s, h][0], _NT,
                                  preferred_element_type=F32) * xa_scale for (s, h) in insts}
    yield
    mx = {k: jnp.max(sc[k], axis=-1, keepdims=True) for k in insts}
    yield
    pr = {k: jnp.exp(sc[k] - mx[k]) for k in insts}
    den = {k: jnp.sum(pr[k], axis=-1, keepdims=True) for k in insts}
    yield
    o = {k: jnp.dot(pr[k].astype(BF16), mem[k][1], preferred_element_type=F32) for k in insts}
    yield
    for (s, h) in insts:
        gate = _silu(proj_ref[rows[s], _cols(OFF_XA_GATE, h, XA_DH)])
        mixed_ref[rows[s], _cols(HG_W + RT_W, h, XA_DH)] = (o[s, h] / den[s, h] * gate).astype(BF16)


def _chunk_cumsum(g):
    tokens, c = g.shape
    x = g.reshape(tokens // SUBLANES, SUBLANES, c)
    sub = lax.broadcasted_iota(jnp.int32, x.shape, 1)
    shift = 1
    while shift < SUBLANES:
        x = x + jnp.where(sub >= shift, pltpu.roll(x, shift, 1), 0.0)
        shift *= 2
    tiles = CHUNK // SUBLANES
    x = x.reshape(tokens // CHUNK, tiles, SUBLANES, c)
    out = [x[:, 0]]
    for j in range(1, tiles):
        total = out[-1][:, SUBLANES - 1:SUBLANES, :]
        out.append(x[:, j] + total)
    return jnp.stack(out, axis=1).reshape(tokens, c)


class _Projection:
    SEGMENTS = ((OFF_HG_F, OFF_HG_I), (OFF_HG_Q, OFF_HG_F), (OFF_HG_I, OFF_RT_Q), (OFF_RT_Q, D_IN))
    TILES_GATES = HG_W // PROJ_N_TILE
    TILES_HGRN = 4 * HG_W // PROJ_N_TILE
    TILES_ALL = D_IN // PROJ_N_TILE

    def __init__(self, x, norm_g_ref, w_in_ref, xn_ref, proj_ref):
        xn_ref[...] = _rms(x, norm_g_ref[...]).astype(BF16)
        self._refs = (xn_ref, w_in_ref, proj_ref)
        self._pending = [n0 for lo, hi in self.SEGMENTS for n0 in range(lo, hi, PROJ_N_TILE)]
        assert len(self._pending) == self.TILES_ALL
        self._done = 0

    def emit(self, n=1):
        xn_ref, w_in_ref, proj_ref = self._refs
        for _ in range(min(n, len(self._pending))):
            cols = slice(self._pending[0], self._pending.pop(0) + PROJ_N_TILE)
            proj_ref[:, cols] = jnp.dot(xn_ref[...], w_in_ref[:, cols], preferred_element_type=F32)
            self._done += 1

    def require(self, n_tiles):
        self.emit(n_tiles - self._done)


def _mix(n_streams, ts, x_ref, y_ref, cos_t, sin_t, mdec_ref, qdec_ref, kdec_ref, rdec_ref,
         get_mem, lb_ref, hg_g_ref, rt_g_ref, w_out_ref, fin_g_ref,
         proj_ref, gc_ref, mixed_ref, get_st, set_st, get_r, set_r, proj):
    proj.require(proj.TILES_GATES)
    lg = lb_ref[...]
    e = jnp.exp(lg - jnp.max(lg, axis=0, keepdims=True))
    lb = e[0:1, :] / jnp.sum(e, axis=0, keepdims=True)
    for c0 in range(0, HG_W, 2 * LANES):
        cols = slice(OFF_HG_F + c0, OFF_HG_F + c0 + 2 * LANES)
        lbc = lb[:, c0:c0 + 2 * LANES]
        half_span = 0.5 * (1.0 - lbc)
        bt = half_span * jnp.tanh(0.5 * proj_ref[:, cols])
        g = jnp.log(lbc + (half_span + bt))
        proj_ref[:, cols] = half_span - bt
        gc_ref[:, c0:c0 + 2 * LANES] = _chunk_cumsum(g)
        proj.emit(PROJ_TILES_PER_GATE_GROUP)

    proj.require(proj.TILES_HGRN)
    causal = (lax.broadcasted_iota(jnp.int32, (CHUNK, LANES), 1)
              <= lax.broadcasted_iota(jnp.int32, (CHUNK, LANES), 0))
    for s in range(n_streams):
        for c in range(ts // CHUNK):
            rows = slice(s * ts + c * CHUNK, s * ts + (c + 1) * CHUNK)
            _hgrn_chunk(proj_ref, gc_ref, mixed_ref, hg_g_ref, rows, causal,
                        lambda h, s=s: get_st(s, h), lambda h, val, s=s: set_st(s, h, val), proj)

    proj.require(proj.TILES_ALL)
    _run_interleaved(
        _retention(n_streams, ts, proj_ref, mixed_ref, cos_t, sin_t, mdec_ref, qdec_ref, kdec_ref, rdec_ref,
                   rt_g_ref, get_r, set_r),
        _cross_attention(n_streams, ts, proj_ref, mixed_ref, get_mem),
        _out_projection(mixed_ref, w_out_ref, y_ref, 0, HG_W, x_ref))
    sumsq = []
    _run_interleaved(_out_projection(mixed_ref, w_out_ref, y_ref, HG_W, D_MIX, y_ref, sumsq))

    inv = lax.rsqrt(sum(sumsq) * (1.0 / D_MODEL) + EPS)
    y_ref[...] = y_ref[...] * inv * fin_g_ref[...]


def _out_projection(mixed_ref, w_out_ref, acc_ref, k_lo, k_hi, base_ref, sumsq=None):
    for n0 in range(0, D_MODEL, OUT_N_TILE):
        cols = slice(n0, n0 + OUT_N_TILE)
        out = base_ref[:, cols] + jnp.dot(mixed_ref[:, k_lo:k_hi], w_out_ref[k_lo:k_hi, cols],
                                          preferred_element_type=F32)
        acc_ref[:, cols] = out
        if sumsq is not None:
            sumsq.append(jnp.sum(out * out, axis=-1, keepdims=True))
        yield


def _run_interleaved(*stage_generators):
    pending = list(stage_generators)
    while pending:
        for g in list(pending):
            try:
                next(g)
            except StopIteration:
                pending.remove(g)


def _memory_kv_kernel(mem_ref, g_ref, wk_ref, wv_ref, k_ref, v_ref, kb_ref, vb_ref):
    m = _rms(mem_ref[0], g_ref[...]).astype(BF16)
    k = jnp.dot(m, wk_ref[...], preferred_element_type=F32)
    v = jnp.dot(m, wv_ref[...], preferred_element_type=F32)
    for h in range(XA_H):
        k_ref[0, pl.ds(h, N_MEM, stride=XA_H), :] = k[:, _cols(0, h, XA_DH)]
        v_ref[0, pl.ds(h, N_MEM, stride=XA_H), :] = v[:, _cols(0, h, XA_DH)]
    kb_ref[0] = k.astype(BF16)
    vb_ref[0] = v.astype(BF16)


def _prompt_kernel(x_ref, cos_ref, sin_ref, mdec_ref, qdec_ref, kdec_ref, rdec_ref, mk_ref, mv_ref,
                   norm_g_ref, w_in_ref, lb_ref, hg_g_ref, rt_g_ref, w_out_ref, fin_g_ref,
                   y_ref, s_out_ref, r_out_ref, proj_ref, xn_ref, gc_ref, mixed_ref, st_ref):
    j = pl.program_id(1)

    @pl.when(j == 0)
    def _():
        st_ref[...] = jnp.zeros_like(st_ref)
        r_out_ref[...] = jnp.zeros_like(r_out_ref)

    proj = _Projection(x_ref[0], norm_g_ref, w_in_ref, xn_ref, proj_ref)

    def set_st(s, h, val):
        st_ref[h] = val

    def set_r(s, p, val):
        r_out_ref[0, p] = val

    def get_mem(s, h):
        return mk_ref[0, :, _cols(0, h, XA_DH)], mv_ref[0, :, _cols(0, h, XA_DH)]

    _mix(
        1, TOKEN_BLOCK, x_ref.at[0], y_ref.at[0], cos_ref[...], sin_ref[...], mdec_ref, qdec_ref, kdec_ref, rdec_ref,
        get_mem, lb_ref, hg_g_ref, rt_g_ref, w_out_ref, fin_g_ref,
        proj_ref, gc_ref, mixed_ref,
        lambda s, h: st_ref[h], set_st, lambda s, p: r_out_ref[0, p], set_r, proj)

    @pl.when(j == pl.num_programs(1) - 1)
    def _():
        for h in range(HG_H):
            s_out_ref[0, h] = st_ref[h].T


def _sample_kernel(x_ref, cos_ref, sin_ref, mdec_ref, qdec_ref, kdec_ref, rdec_ref, mk_ref, mv_ref,
                   s_in_ref, r_in_ref,
                   norm_g_ref, w_in_ref, lb_ref, hg_g_ref, rt_g_ref, w_out_ref, fin_g_ref,
                   y_ref, s_out_ref, r_out_ref, proj_ref, xn_ref, gc_ref, mixed_ref):
    n_streams = TOKEN_BLOCK // CHUNK

    def set_st(s, h, val):
        s_out_ref[s, h] = val.T

    def set_r(s, p, val):
        r_out_ref[s, p] = val

    def get_mem(s, h):
        head_rows = pl.ds(h, N_MEM, stride=XA_H)
        return mk_ref[s, head_rows, :].astype(BF16), mv_ref[s, head_rows, :].astype(BF16)

    cos_t = jnp.concatenate([cos_ref[...]] * n_streams, axis=0)
    sin_t = jnp.concatenate([sin_ref[...]] * n_streams, axis=0)
    proj = _Projection(x_ref[...], norm_g_ref, w_in_ref, xn_ref, proj_ref)
    _mix(
        n_streams, CHUNK, x_ref, y_ref, cos_t, sin_t, mdec_ref, qdec_ref, kdec_ref, rdec_ref,
        get_mem, lb_ref, hg_g_ref, rt_g_ref, w_out_ref, fin_g_ref,
        proj_ref, gc_ref, mixed_ref,
        lambda s, h: s_in_ref[s, h].T, set_st, lambda s, p: r_in_ref[s, p], set_r, proj)


def _const_spec(shape):
    zeros = (0,) * len(shape)
    return pl.BlockSpec(shape, lambda *_: zeros, pipeline_mode=pl.Buffered(1))


def _layer_scratch():
    return [pltpu.VMEM((TOKEN_BLOCK, D_IN), F32),
            pltpu.VMEM((TOKEN_BLOCK, D_MODEL), BF16),
            pltpu.VMEM((TOKEN_BLOCK, HG_W), F32),
            pltpu.VMEM((TOKEN_BLOCK, D_MIX), BF16)]


def _memory_kv(mem, g, wk, wv):
    b = mem.shape[0]
    out_f = jax.ShapeDtypeStruct((b, N_MEM * XA_H, XA_DH), F32)
    out_b = jax.ShapeDtypeStruct((b, N_MEM, XA_W), BF16)
    blk_f = pl.BlockSpec((1, N_MEM * XA_H, XA_DH), lambda i: (i, 0, 0))
    blk_b = pl.BlockSpec((1, N_MEM, XA_W), lambda i: (i, 0, 0))
    return pl.pallas_call(
        _memory_kv_kernel,
        grid=(b,),
        in_specs=[pl.BlockSpec((1, N_MEM, D_MODEL), lambda i: (i, 0, 0)),
                  pl.BlockSpec((1, D_MODEL), lambda i: (0, 0)),
                  pl.BlockSpec((D_MODEL, XA_W), lambda i: (0, 0)),
                  pl.BlockSpec((D_MODEL, XA_W), lambda i: (0, 0))],
        out_specs=[blk_f, blk_f, blk_b, blk_b],
        out_shape=[out_f, out_f, out_b, out_b],
        name="memory_kv",
    )(mem, g.reshape(1, D_MODEL), wk.astype(BF16), wv.astype(BF16))


def _prompt_layer(x, mk_b, mv_b, weights):
    b, t, _ = x.shape
    tb = TOKEN_BLOCK
    cos_t, sin_t = _rope_tables(np.arange(t))
    mdec, qdec, kdec, rdec = _retention_tables(tb)
    consts = (mdec, qdec, kdec, rdec)
    in_specs = ([pl.BlockSpec((1, tb, D_MODEL), lambda i, j: (i, j, 0)),
                 pl.BlockSpec((tb, LANES), lambda i, j: (j, 0)),
                 pl.BlockSpec((tb, LANES), lambda i, j: (j, 0))]
                + [_const_spec(c.shape) for c in consts]
                + [pl.BlockSpec((1, N_MEM, XA_W), lambda i, j: (i, 0, 0)),
                   pl.BlockSpec((1, N_MEM, XA_W), lambda i, j: (i, 0, 0))]
                + [_const_spec(w.shape) for w in weights])
    out_shape = [jax.ShapeDtypeStruct((b, t, D_MODEL), F32),
                 jax.ShapeDtypeStruct((b, HG_H, HG_DK, HG_DV), F32),
                 jax.ShapeDtypeStruct((b, RT_H // 2, 2 * RT_DK, RT_DV), F32)]
    out_specs = [pl.BlockSpec((1, tb, D_MODEL), lambda i, j: (i, j, 0)),
                 pl.BlockSpec((1, HG_H, HG_DK, HG_DV), lambda i, j: (i, 0, 0, 0)),
                 pl.BlockSpec((1, RT_H // 2, 2 * RT_DK, RT_DV), lambda i, j: (i, 0, 0, 0))]
    return pl.pallas_call(
        _prompt_kernel,
        grid=(b, t // tb),
        in_specs=in_specs,
        out_specs=out_specs,
        out_shape=out_shape,
        scratch_shapes=_layer_scratch() + [pltpu.VMEM((HG_H, HG_DV, HG_DK), F32)],
        compiler_params=pltpu.CompilerParams(dimension_semantics=("arbitrary", "arbitrary"),
                                             vmem_limit_bytes=VMEM_LIMIT_BYTES),
        name="prompt_layer",
    )(x, cos_t, sin_t, *consts, mk_b, mv_b, *weights)


def _sample_layer(x, s_hg, s_rt, mem_k, mem_v, weights):
    b, t, _ = x.shape
    assert t == CHUNK
    tb = TOKEN_BLOCK
    nsb = tb // t
    cos_t, sin_t = _rope_tables(PAST_LEN + np.arange(t))
    mdec, qdec, kdec, rdec = _retention_tables(t)
    consts = (mdec, qdec, kdec, rdec)
    s_rt2 = s_rt.reshape(b, RT_H // 2, 2 * RT_DK, RT_DV)
    mem_spec = pl.BlockSpec((nsb, N_MEM * XA_H, XA_DH), lambda i: (i, 0, 0))
    in_specs = ([pl.BlockSpec((tb, D_MODEL), lambda i: (i, 0)),
                 _const_spec(cos_t.shape), _const_spec(sin_t.shape)]
                + [_const_spec(c.shape) for c in consts]
                + [mem_spec, mem_spec,
                   pl.BlockSpec((nsb, HG_H, HG_DK, HG_DV), lambda i: (i, 0, 0, 0)),
                   pl.BlockSpec((nsb, RT_H // 2, 2 * RT_DK, RT_DV), lambda i: (i, 0, 0, 0))]
                + [_const_spec(w.shape) for w in weights])
    out_shape = [jax.ShapeDtypeStruct((b * t, D_MODEL), F32),
                 jax.ShapeDtypeStruct((b, HG_H, HG_DK, HG_DV), F32),
                 jax.ShapeDtypeStruct((b, RT_H // 2, 2 * RT_DK, RT_DV), F32)]
    out_specs = [pl.BlockSpec((tb, D_MODEL), lambda i: (i, 0)),
                 pl.BlockSpec((nsb, HG_H, HG_DK, HG_DV), lambda i: (i, 0, 0, 0)),
                 pl.BlockSpec((nsb, RT_H // 2, 2 * RT_DK, RT_DV), lambda i: (i, 0, 0, 0))]
    y, s_new, r_new = pl.pallas_call(
        _sample_kernel,
        grid=(b // nsb,),
        in_specs=in_specs,
        out_specs=out_specs,
        out_shape=out_shape,
        scratch_shapes=_layer_scratch(),
        compiler_params=pltpu.CompilerParams(dimension_semantics=("arbitrary",),
                                             vmem_limit_bytes=VMEM_LIMIT_BYTES),
        name="sample_layer",
    )(x.reshape(b * t, D_MODEL), cos_t, sin_t, *consts,
      mem_k.reshape(b, N_MEM * XA_H, XA_DH), mem_v.reshape(b, N_MEM * XA_H, XA_DH), s_hg, s_rt2, *weights)
    return y.reshape(b, t, D_MODEL), s_new, r_new


def kernel(x_prompt, x_sample, mem_prompt, state_hgrn, state_ret, cache_mem_k, cache_mem_v, norm_g, w_in,
           lb_logits, hg_norm_g, rt_norm_g, mem_norm_g, w_mem_k, w_mem_v, w_out, final_norm_g):
    depth = w_in.shape[0]
    assert depth == 1, "single-layer step"
    bp = x_prompt.shape[0]
    bs = x_sample.shape[0]
    weights = (norm_g[0].reshape(1, D_MODEL), w_in[0].astype(BF16), lb_logits,
               hg_norm_g[0].reshape(1, HG_W), rt_norm_g[0].reshape(1, RT_W), w_out[0].astype(BF16),
               final_norm_g.reshape(1, D_MODEL))
    mk, mv, mk_b, mv_b = _memory_kv(mem_prompt, mem_norm_g[0], w_mem_k[0], w_mem_v[0])
    y_p, s_p, r_p = _prompt_layer(x_prompt, mk_b, mv_b, weights)
    y_s, s_s, r_s = _sample_layer(x_sample, state_hgrn[0], state_ret[0], cache_mem_k[0], cache_mem_v[0], weights)
    return (y_p, y_s,
            s_p.reshape(1, bp, HG_H, HG_DK, HG_DV),
            r_p.reshape(1, bp, RT_H, RT_DK, RT_DV),
            mk.reshape(1, bp, N_MEM, XA_H, XA_DH),
            mv.reshape(1, bp, N_MEM, XA_H, XA_DH),
            s_s.reshape(1, bs, HG_H, HG_DK, HG_DV),
            r_s.reshape(1, bs, RT_H, RT_DK, RT_DV))
```

```python
import functools

import numpy as np
import jax
import jax.numpy as jnp
from jax import lax
from jax.experimental import pallas as pl
from jax.experimental.pallas import tpu as pltpu

F32 = jnp.float32
BF16 = jnp.bfloat16

D_MODEL = 1024
PAST_LEN = 4096
CHUNK = 64
N_MEM = 256
EPS = 1e-6
ROPE_BASE = 10000.0
HG_H, HG_DK, HG_DV = 8, 128, 128
HG_W = HG_H * HG_DV
RT_H, RT_DK, RT_DV = 4, 64, 128
RT_QK = RT_H * RT_DK
RT_W = RT_H * RT_DV
XA_H, XA_DH = 4, 128
XA_W = XA_H * XA_DH
D_MIX = HG_W + RT_W + XA_W
OFF_HG_Q = 0
OFF_HG_F = OFF_HG_Q + HG_W
OFF_HG_I = OFF_HG_F + HG_W
OFF_HG_GATE = OFF_HG_I + HG_W
OFF_RT_Q = OFF_HG_GATE + HG_W
OFF_RT_K = OFF_RT_Q + RT_QK
OFF_RT_V = OFF_RT_K + RT_QK
OFF_RT_GATE = OFF_RT_V + RT_W
OFF_XA_Q = OFF_RT_GATE + RT_W
OFF_XA_GATE = OFF_XA_Q + XA_W
D_IN = OFF_XA_GATE + XA_W

LANES = 128
SUBLANES = 8
TOKEN_BLOCK = 256
PROJ_N_TILE = 512
PROJ_TILES_PER_GATE_GROUP = 2
PROJ_TILES_PER_HGRN_STAGE = 1
OUT_N_TILE = 512
VMEM_LIMIT_BYTES = 56 * 1024 * 1024

_NT = (((1,), (1,)), ((), ()))
_TN = (((0,), (0,)), ((), ()))


def _silu(x):
    h = 0.5 * x
    return h + h * jnp.tanh(h)


def _rms(x, g):
    return x * lax.rsqrt(jnp.mean(x * x, axis=-1, keepdims=True) + EPS) * g


def _cols(off, i, width):
    return slice(off + i * width, off + (i + 1) * width)


def _rope_tables(pos):
    half = RT_DK // 2
    inv_freq = ROPE_BASE ** (-np.arange(half, dtype=np.float64) / half)
    ang = np.asarray(pos, np.float64)[:, None] * inv_freq[None, :]
    cos, sin = np.cos(ang), np.sin(ang)
    cos_t = np.concatenate([cos, cos, cos, cos], axis=-1)
    sin_t = np.concatenate([-sin, sin, -sin, sin], axis=-1)
    return jnp.asarray(cos_t, F32), jnp.asarray(sin_t, F32)


def _retention_tables(ts):
    h = np.arange(RT_H, dtype=np.float64)
    log_gamma = np.log1p(-np.exp2(-5.0 - h))
    n = np.arange(ts, dtype=np.float64)
    scale = RT_DK ** -0.5
    dist = np.abs(n[:, None] - n[None, :])
    visible = (n[None, :] // CHUNK) <= (n[:, None] // CHUNK)
    mdec = scale * np.exp(log_gamma[:, None, None] * dist[None]) * visible[None]
    qd = np.exp(log_gamma[:, None] * (n[None, :] + 1.0))
    kd = scale * np.exp(log_gamma[:, None] * (ts - 1.0 - n[None, :]))
    rd = np.exp(log_gamma * ts)

    def pair_lanes(a):
        a = np.repeat(a[:, :, None], RT_DK, axis=2)
        return np.concatenate([a[0::2], a[1::2]], axis=2)

    rdec = np.repeat(np.repeat(rd[:, None], RT_DK, axis=1).reshape(RT_H // 2, 2 * RT_DK, 1), LANES, axis=2)
    return (jnp.asarray(mdec, F32), jnp.asarray(pair_lanes(qd), F32), jnp.asarray(pair_lanes(kd), F32),
            jnp.asarray(rdec, F32))


def _hgrn_chunk(proj_ref, gc_ref, mixed_ref, hg_g_ref, rows, causal, get_st, set_st, proj):
    mid = CHUNK // 2 - 1
    heads = range(HG_H)
    q_rel, k_rel, q_abs, k_end, v_b, dec = [], [], [], [], [], []
    for h in heads:
        q = _silu(proj_ref[rows, _cols(OFF_HG_Q, h, HG_DK)])
        k = proj_ref[rows, _cols(OFF_HG_F, h, HG_DK)]
        gc = gc_ref[rows, _cols(0, h, HG_DK)]
        g_mid = gc[mid:mid + 1, :]
        g_end = gc[CHUNK - 1:CHUNK, :]
        q_rel.append((q * jnp.exp(gc - g_mid)).astype(BF16))
        k_rel.append((k * jnp.exp(g_mid - gc)).astype(BF16))
        q_abs.append((q * jnp.exp(gc)).astype(BF16))
        k_end.append((k * jnp.exp(g_end - gc)).astype(BF16))
        v_b.append(proj_ref[rows, _cols(OFF_HG_I, h, HG_DV)].astype(BF16))
        dec.append(jnp.exp(g_end))
    proj.emit(PROJ_TILES_PER_HGRN_STAGE)
    k_pad = jnp.zeros((LANES - CHUNK, HG_DK), BF16)
    a = [lax.dot_general(q_rel[h], jnp.concatenate([k_rel[h], k_pad], axis=0), _NT, preferred_element_type=F32)
         for h in heads]
    st = [get_st(h) for h in heads]
    v_t = [jnp.concatenate([v_b[h], jnp.zeros((LANES - CHUNK, HG_DV), BF16)], axis=0).T for h in heads]
    kv = [jnp.dot(v_t[h][:, :CHUNK], k_end[h], preferred_element_type=F32) for h in heads]
    for h in heads:
        set_st(h, dec[h] * st[h] + kv[h])
    a_m = [jnp.where(causal, a[h], 0.0).astype(BF16) for h in heads]
    o = [lax.dot_general(jnp.concatenate([q_abs[h], a_m[h]], axis=1),
                         jnp.concatenate([st[h].astype(BF16), v_t[h]], axis=1), _NT, preferred_element_type=F32)
         for h in heads]
    ms = [jnp.mean(o[h] * o[h], axis=-1, keepdims=True) for h in heads]
    proj.emit(PROJ_TILES_PER_HGRN_STAGE)
    for h in heads:
        gate = _silu(proj_ref[rows, _cols(OFF_HG_GATE, h, HG_DV)])
        on = o[h] * lax.rsqrt(ms[h] + EPS) * hg_g_ref[:, _cols(0, h, HG_DV)] * gate
        mixed_ref[rows, _cols(0, h, HG_DV)] = on.astype(BF16)


def _retention(n_streams, ts, proj_ref, mixed_ref, cos_t, sin_t, mdec_ref, qdec_ref, kdec_ref, rdec_ref,
               rt_g_ref, get_r, set_r):
    lane = lax.broadcasted_iota(jnp.int32, (ts, LANES), 1)
    first_half = (lane % RT_DK) < (RT_DK // 2)
    low_head = lane < RT_DK
    low_rows = lax.broadcasted_iota(jnp.int32, (2 * RT_DK, RT_DV), 0) < RT_DK

    def rope(xs, cs, sn):
        swapped = jnp.where(first_half, pltpu.roll(xs, LANES - RT_DK // 2, 1), pltpu.roll(xs, RT_DK // 2, 1))
        return xs * cs + swapped * sn

    pairs = [(s, p) for s in range(n_streams) for p in range(RT_H // 2)]
    insts = [(s, p, i) for (s, p) in pairs for i in range(2)]
    rows = {s: slice(s * ts, (s + 1) * ts) for s in range(n_streams)}
    qp, kp_b, qd, kd_b, r_pair, v_pair = {}, {}, {}, {}, {}, {}
    for (s, p) in pairs:
        cs, sn = cos_t[rows[s], :], sin_t[rows[s], :]
        q = rope(proj_ref[rows[s], _cols(OFF_RT_Q, p, LANES)], cs, sn)
        k = rope(proj_ref[rows[s], _cols(OFF_RT_K, p, LANES)], cs, sn)
        qp[s, p] = q
        qd[s, p] = q * qdec_ref[p]
        kp_b[s, p] = k.astype(BF16)
        kd_b[s, p] = (k * kdec_ref[p]).astype(BF16)
        r_pair[s, p] = get_r(s, p)
        v_pair[s, p] = proj_ref[rows[s], _cols(OFF_RT_V, p, 2 * RT_DV)].astype(BF16)
    yield
    sel = {0: low_head, 1: jnp.logical_not(low_head)}
    q_h = {(s, p, i): jnp.where(sel[i], qp[s, p], 0.0).astype(BF16) for (s, p, i) in insts}
    qd_h = {(s, p, i): jnp.where(sel[i], qd[s, p], 0.0).astype(BF16) for (s, p, i) in insts}
    sc = {(s, p, i): lax.dot_general(q_h[s, p, i], kp_b[s, p], _NT, preferred_element_type=F32)
          for (s, p, i) in insts}
    inter = {(s, p, i): jnp.dot(qd_h[s, p, i], r_pair[s, p].astype(BF16), preferred_element_type=F32)
             for (s, p, i) in insts}
    kv = {(s, p): lax.dot_general(kd_b[s, p], v_pair[s, p], _TN, preferred_element_type=F32) for (s, p) in pairs}
    yield
    for (s, p) in pairs:
        dr = jnp.where(low_rows, kv[s, p][:, :RT_DV], kv[s, p][:, RT_DV:])
        set_r(s, p, rdec_ref[p] * r_pair[s, p] + dr)
    sc_b = {(s, p, i): (sc[s, p, i] * mdec_ref[2 * p + i]).astype(BF16) for (s, p, i) in insts}
    yield
    o = {(s, p, i): jnp.dot(sc_b[s, p, i], v_pair[s, p][:, i * RT_DV:(i + 1) * RT_DV],
                            preferred_element_type=F32) + inter[s, p, i] for (s, p, i) in insts}
    yield
    mu = {k: jnp.mean(o[k], axis=-1, keepdims=True) for k in insts}
    cen = {k: o[k] - mu[k] for k in insts}
    var = {k: jnp.mean(cen[k] * cen[k], axis=-1, keepdims=True) for k in insts}
    yield
    for (s, p, i) in insts:
        h = 2 * p + i
        gate = _silu(proj_ref[rows[s], _cols(OFF_RT_GATE, h, RT_DV)])
        on = cen[s, p, i] * lax.rsqrt(var[s, p, i] + EPS) * rt_g_ref[:, _cols(0, h, RT_DV)] * gate
        mixed_ref[rows[s], _cols(HG_W, h, RT_DV)] = on.astype(BF16)


def _cross_attention(n_streams, ts, proj_ref, mixed_ref, get_mem):
    xa_scale = XA_DH ** -0.5
    insts = [(s, h) for s in range(n_streams) for h in range(XA_H)]
    rows = {s: slice(s * ts, (s + 1) * ts) for s in range(n_streams)}
    mem = {k: get_mem(*k) for k in insts}
    sc = {(s, h): lax.dot_general(proj_ref[rows[s], _cols(OFF_XA_Q, h, XA_DH)].astype(BF16), mem[s, h][0], _NT,
                                  preferred_element_type=F32) * xa_scale for (s, h) in insts}
    yield
    mx = {k: jnp.max(sc[k], axis=-1, keepdims=True) for k in insts}
    yield
    pr = {k: jnp.exp(sc[k] - mx[k]) for k in insts}
    den = {k: jnp.sum(pr[k], axis=-1, keepdims=True) for k in insts}
    yield
    o = {k: jnp.dot(pr[k].astype(BF16), mem[k][1], preferred_element_type=F32) for k in insts}
    yield
    for (s, h) in insts:
        gate = _silu(proj_ref[rows[s], _cols(OFF_XA_GATE, h, XA_DH)])
        mixed_ref[rows[s], _cols(HG_W + RT_W, h, XA_DH)] = (o[s, h] / den[s, h] * gate).astype(BF16)


def _chunk_cumsum(g):
    tokens, c = g.shape
    x = g.reshape(tokens // SUBLANES, SUBLANES, c)
    sub = lax.broadcasted_iota(jnp.int32, x.shape, 1)
    shift = 1
    while shift < SUBLANES:
        x = x + jnp.where(sub >= shift, pltpu.roll(x, shift, 1), 0.0)
        shift *= 2
    tiles = CHUNK // SUBLANES
    x = x.reshape(tokens // CHUNK, tiles, SUBLANES, c)
    out = [x[:, 0]]
    for j in range(1, tiles):
        total = out[-1][:, SUBLANES - 1:SUBLANES, :]
        out.append(x[:, j] + total)
    return jnp.stack(out, axis=1).reshape(tokens, c)


def _normalize_input(x_ref, norm_g_ref, xn_ref):
    for r0 in range(0, xn_ref.shape[0], CHUNK):
        xn_ref[r0:r0 + CHUNK, :] = _rms(x_ref[r0:r0 + CHUNK, :], norm_g_ref[...]).astype(BF16)
        yield


class _Projection:
    SEGMENTS = ((OFF_HG_F, OFF_HG_I), (OFF_HG_Q, OFF_HG_F), (OFF_HG_I, OFF_RT_Q), (OFF_RT_Q, D_IN))
    TILES_GATES = HG_W // PROJ_N_TILE
    TILES_HGRN = 4 * HG_W // PROJ_N_TILE
    TILES_ALL = D_IN // PROJ_N_TILE

    def __init__(self, xn_ref, w_in_ref, proj_ref):
        self._refs = (xn_ref, w_in_ref, proj_ref)
        self._pending = [n0 for lo, hi in self.SEGMENTS for n0 in range(lo, hi, PROJ_N_TILE)]
        assert len(self._pending) == self.TILES_ALL
        self._done = 0

    def emit(self, n=1):
        xn_ref, w_in_ref, proj_ref = self._refs
        for _ in range(min(n, len(self._pending))):
            cols = slice(self._pending[0], self._pending.pop(0) + PROJ_N_TILE)
            proj_ref[:, cols] = jnp.dot(xn_ref[...], w_in_ref[:, cols], preferred_element_type=F32)
            self._done += 1

    def require(self, n_tiles):
        self.emit(n_tiles - self._done)


def _mix(n_streams, ts, x_ref, acc_ref, cos_t, sin_t, mdec_ref, qdec_ref, kdec_ref, rdec_ref,
         get_mem, lb_ref, hg_g_ref, rt_g_ref, w_out_ref,
         proj_ref, gc_ref, mixed_ref, get_st, set_st, get_r, set_r, proj, early_work, late_stages):
    proj.require(proj.TILES_GATES)
    early_work()
    lg = lb_ref[...]
    e = jnp.exp(lg - jnp.max(lg, axis=0, keepdims=True))
    lb = e[0:1, :] / jnp.sum(e, axis=0, keepdims=True)
    for c0 in range(0, HG_W, 2 * LANES):
        cols = slice(OFF_HG_F + c0, OFF_HG_F + c0 + 2 * LANES)
        lbc = lb[:, c0:c0 + 2 * LANES]
        half_span = 0.5 * (1.0 - lbc)
        bt = half_span * jnp.tanh(0.5 * proj_ref[:, cols])
        g = jnp.log(lbc + (half_span + bt))
        proj_ref[:, cols] = half_span - bt
        gc_ref[:, c0:c0 + 2 * LANES] = _chunk_cumsum(g)
        proj.emit(PROJ_TILES_PER_GATE_GROUP)

    proj.require(proj.TILES_HGRN)
    causal = (lax.broadcasted_iota(jnp.int32, (CHUNK, LANES), 1)
              <= lax.broadcasted_iota(jnp.int32, (CHUNK, LANES), 0))
    for s in range(n_streams):
        for c in range(ts // CHUNK):
            rows = slice(s * ts + c * CHUNK, s * ts + (c + 1) * CHUNK)
            _hgrn_chunk(proj_ref, gc_ref, mixed_ref, hg_g_ref, rows, causal,
                        lambda h, s=s: get_st(s, h), lambda h, val, s=s: set_st(s, h, val), proj)

    proj.require(proj.TILES_ALL)
    _run_interleaved(
        _retention(n_streams, ts, proj_ref, mixed_ref, cos_t, sin_t, mdec_ref, qdec_ref, kdec_ref, rdec_ref,
                   rt_g_ref, get_r, set_r),
        _cross_attention(n_streams, ts, proj_ref, mixed_ref, get_mem),
        _out_projection(mixed_ref, w_out_ref, acc_ref, 0, HG_W, x_ref))
    sumsq = []
    _run_interleaved(_out_projection(mixed_ref, w_out_ref, acc_ref, HG_W, D_MIX, acc_ref, sumsq), late_stages)

    return lax.rsqrt(sum(sumsq) * (1.0 / D_MODEL) + EPS)


def _out_projection(mixed_ref, w_out_ref, acc_ref, k_lo, k_hi, base_ref, sumsq=None):
    for n0 in range(0, D_MODEL, OUT_N_TILE):
        cols = slice(n0, n0 + OUT_N_TILE)
        out = base_ref[:, cols] + jnp.dot(mixed_ref[:, k_lo:k_hi], w_out_ref[k_lo:k_hi, cols],
                                          preferred_element_type=F32)
        acc_ref[:, cols] = out
        if sumsq is not None:
            sumsq.append(jnp.sum(out * out, axis=-1, keepdims=True))
        yield


def _run_interleaved(*stage_generators):
    pending = list(stage_generators)
    while pending:
        for g in list(pending):
            try:
                next(g)
            except StopIteration:
                pending.remove(g)


def _memory_kv_kernel(mem_ref, g_ref, wk_ref, wv_ref, k_ref, v_ref, kb_ref, vb_ref):
    m = _rms(mem_ref[0], g_ref[...]).astype(BF16)
    k = jnp.dot(m, wk_ref[...], preferred_element_type=F32)
    v = jnp.dot(m, wv_ref[...], preferred_element_type=F32)
    for h in range(XA_H):
        k_ref[0, pl.ds(h, N_MEM, stride=XA_H), :] = k[:, _cols(0, h, XA_DH)]
        v_ref[0, pl.ds(h, N_MEM, stride=XA_H), :] = v[:, _cols(0, h, XA_DH)]
    kb_ref[0] = k.astype(BF16)
    vb_ref[0] = v.astype(BF16)


def _prompt_kernel(blocks_per_stream, x_ref, x_next_ref, cos_ref, sin_ref, mdec_ref, qdec_ref, kdec_ref, rdec_ref,
                   mk_ref, mv_ref, norm_g_ref, w_in_ref, lb_ref, hg_g_ref, rt_g_ref, w_out_ref, fin_g_ref,
                   y_ref, s_out_ref, r_out_ref, proj_ref, xn_ref, gc_ref, mixed_ref, st_ref, acc_ref, inv_ref):
    t = pl.program_id(0)
    n_blocks = pl.num_programs(0) - 1
    j = t % blocks_per_stream
    running = t < n_blocks

    def finish_previous():
        y_ref[0] = acc_ref[...] * inv_ref[:, 0:1] * fin_g_ref[...]

    @pl.when(t == 0)
    def _():
        _run_interleaved(_normalize_input(x_ref.at[0], norm_g_ref, xn_ref))
        acc_ref[...] = jnp.zeros_like(acc_ref)
        inv_ref[...] = jnp.zeros_like(inv_ref)

    @pl.when(running & (j == 0))
    def _():
        st_ref[...] = jnp.zeros_like(st_ref)
        r_out_ref[...] = jnp.zeros_like(r_out_ref)

    def set_st(s, h, val):
        st_ref[h] = val

    def set_r(s, p, val):
        r_out_ref[0, p] = val

    def get_mem(s, h):
        return mk_ref[0, :, _cols(0, h, XA_DH)], mv_ref[0, :, _cols(0, h, XA_DH)]

    @pl.when(running)
    def _():
        inv = _mix(
            1, TOKEN_BLOCK, x_ref.at[0], acc_ref, cos_ref[...], sin_ref[...], mdec_ref, qdec_ref, kdec_ref, rdec_ref,
            get_mem, lb_ref, hg_g_ref, rt_g_ref, w_out_ref,
            proj_ref, gc_ref, mixed_ref,
            lambda s, h: st_ref[h], set_st, lambda s, p: r_out_ref[0, p], set_r,
            _Projection(xn_ref, w_in_ref, proj_ref),
            finish_previous, _normalize_input(x_next_ref.at[0], norm_g_ref, xn_ref))
        inv_ref[...] = jnp.broadcast_to(inv, inv_ref.shape)

    @pl.when(t == n_blocks)
    def _():
        finish_previous()

    @pl.when(running & (j == blocks_per_stream - 1))
    def _():
        for h in range(HG_H):
            s_out_ref[0, h] = st_ref[h].T


def _sample_kernel(x_ref, cos_ref, sin_ref, mdec_ref, qdec_ref, kdec_ref, rdec_ref, mk_ref, mv_ref,
                   s_in_ref, r_in_ref,
                   norm_g_ref, w_in_ref, lb_ref, hg_g_ref, rt_g_ref, w_out_ref, fin_g_ref,
                   y_ref, s_out_ref, r_out_ref, proj_ref, xn_ref, gc_ref, mixed_ref):
    n_streams = TOKEN_BLOCK // CHUNK

    def set_st(s, h, val):
        s_out_ref[s, h] = val.T

    def set_r(s, p, val):
        r_out_ref[s, p] = val

    def get_mem(s, h):
        head_rows = pl.ds(h, N_MEM, stride=XA_H)
        return mk_ref[s, head_rows, :].astype(BF16), mv_ref[s, head_rows, :].astype(BF16)

    cos_t = jnp.concatenate([cos_ref[...]] * n_streams, axis=0)
    sin_t = jnp.concatenate([sin_ref[...]] * n_streams, axis=0)
    _run_interleaved(_normalize_input(x_ref, norm_g_ref, xn_ref))
    inv = _mix(
        n_streams, CHUNK, x_ref, y_ref, cos_t, sin_t, mdec_ref, qdec_ref, kdec_ref, rdec_ref,
        get_mem, lb_ref, hg_g_ref, rt_g_ref, w_out_ref,
        proj_ref, gc_ref, mixed_ref,
        lambda s, h: s_in_ref[s, h].T, set_st, lambda s, p: r_in_ref[s, p], set_r,
        _Projection(xn_ref, w_in_ref, proj_ref), lambda: None, iter(()))
    y_ref[...] = y_ref[...] * inv * fin_g_ref[...]


def _const_spec(shape):
    zeros = (0,) * len(shape)
    return pl.BlockSpec(shape, lambda *_: zeros, pipeline_mode=pl.Buffered(1))


def _layer_scratch():
    return [pltpu.VMEM((TOKEN_BLOCK, D_IN), F32),
            pltpu.VMEM((TOKEN_BLOCK, D_MODEL), BF16),
            pltpu.VMEM((TOKEN_BLOCK, HG_W), F32),
            pltpu.VMEM((TOKEN_BLOCK, D_MIX), BF16)]


def _memory_kv(mem, g, wk, wv):
    b = mem.shape[0]
    out_f = jax.ShapeDtypeStruct((b, N_MEM * XA_H, XA_DH), F32)
    out_b = jax.ShapeDtypeStruct((b, N_MEM, XA_W), BF16)
    blk_f = pl.BlockSpec((1, N_MEM * XA_H, XA_DH), lambda i: (i, 0, 0))
    blk_b = pl.BlockSpec((1, N_MEM, XA_W), lambda i: (i, 0, 0))
    return pl.pallas_call(
        _memory_kv_kernel,
        grid=(b,),
        in_specs=[pl.BlockSpec((1, N_MEM, D_MODEL), lambda i: (i, 0, 0)),
                  pl.BlockSpec((1, D_MODEL), lambda i: (0, 0)),
                  pl.BlockSpec((D_MODEL, XA_W), lambda i: (0, 0)),
                  pl.BlockSpec((D_MODEL, XA_W), lambda i: (0, 0))],
        out_specs=[blk_f, blk_f, blk_b, blk_b],
        out_shape=[out_f, out_f, out_b, out_b],
        name="memory_kv",
    )(mem, g.reshape(1, D_MODEL), wk.astype(BF16), wv.astype(BF16))


def _prompt_layer(x, mk_b, mv_b, weights):
    b, t, _ = x.shape
    tb = TOKEN_BLOCK
    cos_t, sin_t = _rope_tables(np.arange(t))
    mdec, qdec, kdec, rdec = _retention_tables(tb)
    consts = (mdec, qdec, kdec, rdec)
    bps = t // tb
    n_blocks = b * bps

    def cur(step):
        return jnp.minimum(step, n_blocks - 1)

    def nxt(step):
        return jnp.minimum(step + 1, n_blocks - 1)

    def prev(step):
        return jnp.maximum(step - 1, 0)

    in_specs = ([pl.BlockSpec((1, tb, D_MODEL), lambda s: (cur(s) // bps, cur(s) % bps, 0)),
                 pl.BlockSpec((1, tb, D_MODEL), lambda s: (nxt(s) // bps, nxt(s) % bps, 0)),
                 pl.BlockSpec((tb, LANES), lambda s: (cur(s) % bps, 0)),
                 pl.BlockSpec((tb, LANES), lambda s: (cur(s) % bps, 0))]
                + [_const_spec(c.shape) for c in consts]
                + [pl.BlockSpec((1, N_MEM, XA_W), lambda s: (cur(s) // bps, 0, 0)),
                   pl.BlockSpec((1, N_MEM, XA_W), lambda s: (cur(s) // bps, 0, 0))]
                + [_const_spec(w.shape) for w in weights])
    out_shape = [jax.ShapeDtypeStruct((b, t, D_MODEL), F32),
                 jax.ShapeDtypeStruct((b, HG_H, HG_DK, HG_DV), F32),
                 jax.ShapeDtypeStruct((b, RT_H // 2, 2 * RT_DK, RT_DV), F32)]
    out_specs = [pl.BlockSpec((1, tb, D_MODEL), lambda s: (prev(s) // bps, prev(s) % bps, 0)),
                 pl.BlockSpec((1, HG_H, HG_DK, HG_DV), lambda s: (cur(s) // bps, 0, 0, 0)),
                 pl.BlockSpec((1, RT_H // 2, 2 * RT_DK, RT_DV), lambda s: (cur(s) // bps, 0, 0, 0))]
    return pl.pallas_call(
        functools.partial(_prompt_kernel, bps),
        grid=(n_blocks + 1,),
        in_specs=in_specs,
        out_specs=out_specs,
        out_shape=out_shape,
        scratch_shapes=_layer_scratch() + [pltpu.VMEM((HG_H, HG_DV, HG_DK), F32),
                                           pltpu.VMEM((tb, D_MODEL), F32),
                                           pltpu.VMEM((tb, LANES), F32)],
        compiler_params=pltpu.CompilerParams(dimension_semantics=("arbitrary",),
                                             vmem_limit_bytes=VMEM_LIMIT_BYTES),
        name="prompt_layer",
    )(x, x, cos_t, sin_t, *consts, mk_b, mv_b, *weights)


def _sample_layer(x, s_hg, s_rt, mem_k, mem_v, weights):
    b, t, _ = x.shape
    assert t == CHUNK
    tb = TOKEN_BLOCK
    nsb = tb // t
    cos_t, sin_t = _rope_tables(PAST_LEN + np.arange(t))
    mdec, qdec, kdec, rdec = _retention_tables(t)
    consts = (mdec, qdec, kdec, rdec)
    s_rt2 = s_rt.reshape(b, RT_H // 2, 2 * RT_DK, RT_DV)
    mem_spec = pl.BlockSpec((nsb, N_MEM * XA_H, XA_DH), lambda i: (i, 0, 0))
    in_specs = ([pl.BlockSpec((tb, D_MODEL), lambda i: (i, 0)),
                 _const_spec(cos_t.shape), _const_spec(sin_t.shape)]
                + [_const_spec(c.shape) for c in consts]
                + [mem_spec, mem_spec,
                   pl.BlockSpec((nsb, HG_H, HG_DK, HG_DV), lambda i: (i, 0, 0, 0)),
                   pl.BlockSpec((nsb, RT_H // 2, 2 * RT_DK, RT_DV), lambda i: (i, 0, 0, 0))]
                + [_const_spec(w.shape) for w in weights])
    out_shape = [jax.ShapeDtypeStruct((b * t, D_MODEL), F32),
                 jax.ShapeDtypeStruct((b, HG_H, HG_DK, HG_DV), F32),
                 jax.ShapeDtypeStruct((b, RT_H // 2, 2 * RT_DK, RT_DV), F32)]
    out_specs = [pl.BlockSpec((tb, D_MODEL), lambda i: (i, 0)),
                 pl.BlockSpec((nsb, HG_H, HG_DK, HG_DV), lambda i: (i, 0, 0, 0)),
                 pl.BlockSpec((nsb, RT_H // 2, 2 * RT_DK, RT_DV), lambda i: (i, 0, 0, 0))]
    y, s_new, r_new = pl.pallas_call(
        _sample_kernel,
        grid=(b // nsb,),
        in_specs=in_specs,
        out_specs=out_specs,
        out_shape=out_shape,
        scratch_shapes=_layer_scratch(),
        compiler_params=pltpu.CompilerParams(dimension_semantics=("arbitrary",),
                                             vmem_limit_bytes=VMEM_LIMIT_BYTES),
        name="sample_layer",
    )(x.reshape(b * t, D_MODEL), cos_t, sin_t, *consts,
      mem_k.reshape(b, N_MEM * XA_H, XA_DH), mem_v.reshape(b, N_MEM * XA_H, XA_DH), s_hg, s_rt2, *weights)
    return y.reshape(b, t, D_MODEL), s_new, r_new


def kernel(x_prompt, x_sample, mem_prompt, state_hgrn, state_ret, cache_mem_k, cache_mem_v, norm_g, w_in,
           lb_logits, hg_norm_g, rt_norm_g, mem_norm_g, w_mem_k, w_mem_v, w_out, final_norm_g):
    depth = w_in.shape[0]
    assert depth == 1, "single-layer step"
    bp = x_prompt.shape[0]
    bs = x_sample.shape[0]
    weights = (norm_g[0].reshape(1, D_MODEL), w_in[0].astype(BF16), lb_logits,
               hg_norm_g[0].reshape(1, HG_W), rt_norm_g[0].reshape(1, RT_W), w_out[0].astype(BF16),
               final_norm_g.reshape(1, D_MODEL))
    mk, mv, mk_b, mv_b = _memory_kv(mem_prompt, mem_norm_g[0], w_mem_k[0], w_mem_v[0])
    y_p, s_p, r_p = _prompt_layer(x_prompt, mk_b, mv_b, weights)
    y_s, s_s, r_s = _sample_layer(x_sample, state_hgrn[0], state_ret[0], cache_mem_k[0], cache_mem_v[0], weights)
    return (y_p, y_s,
            s_p.reshape(1, bp, HG_H, HG_DK, HG_DV),
            r_p.reshape(1, bp, RT_H, RT_DK, RT_DV),
            mk.reshape(1, bp, N_MEM, XA_H, XA_DH),
            mv.reshape(1, bp, N_MEM, XA_H, XA_DH),
            s_s.reshape(1, bs, HG_H, HG_DK, HG_DV),
            r_s.reshape(1, bs, RT_H, RT_DK, RT_DV))
```

```python
import functools

import numpy as np
import jax
import jax.numpy as jnp
from jax import lax
from jax.experimental import pallas as pl
from jax.experimental.pallas import tpu as pltpu

F32 = jnp.float32
BF16 = jnp.bfloat16

D_MODEL = 1024
PAST_LEN = 4096
CHUNK = 64
N_MEM = 256
EPS = 1e-6
ROPE_BASE = 10000.0
HG_H, HG_DK, HG_DV = 8, 128, 128
HG_W = HG_H * HG_DV
RT_H, RT_DK, RT_DV = 4, 64, 128
RT_QK = RT_H * RT_DK
RT_W = RT_H * RT_DV
XA_H, XA_DH = 4, 128
XA_W = XA_H * XA_DH
D_MIX = HG_W + RT_W + XA_W
OFF_HG_Q = 0
OFF_HG_F = OFF_HG_Q + HG_W
OFF_HG_I = OFF_HG_F + HG_W
OFF_HG_GATE = OFF_HG_I + HG_W
OFF_RT_Q = OFF_HG_GATE + HG_W
OFF_RT_K = OFF_RT_Q + RT_QK
OFF_RT_V = OFF_RT_K + RT_QK
OFF_RT_GATE = OFF_RT_V + RT_W
OFF_XA_Q = OFF_RT_GATE + RT_W
OFF_XA_GATE = OFF_XA_Q + XA_W
D_IN = OFF_XA_GATE + XA_W

LANES = 128
SUBLANES = 8
TOKEN_BLOCK = 256
PROJ_N_TILE = 512
PROJ_TILES_PER_GATE_GROUP = 2
PROJ_TILES_PER_HGRN_STAGE = 1
OUT_N_TILE = 512
VMEM_LIMIT_BYTES = 56 * 1024 * 1024

_NT = (((1,), (1,)), ((), ()))
_TN = (((0,), (0,)), ((), ()))


def _silu(x):
    h = 0.5 * x
    return h + h * jnp.tanh(h)


def _rms(x, g):
    return x * lax.rsqrt(jnp.mean(x * x, axis=-1, keepdims=True) + EPS) * g


def _cols(off, i, width):
    return slice(off + i * width, off + (i + 1) * width)


def _rope_tables(pos):
    half = RT_DK // 2
    inv_freq = ROPE_BASE ** (-np.arange(half, dtype=np.float64) / half)
    ang = np.asarray(pos, np.float64)[:, None] * inv_freq[None, :]
    cos, sin = np.cos(ang), np.sin(ang)
    cos_t = np.concatenate([cos, cos, cos, cos], axis=-1)
    sin_t = np.concatenate([-sin, sin, -sin, sin], axis=-1)
    return jnp.asarray(cos_t, F32), jnp.asarray(sin_t, F32)


def _retention_tables(ts):
    h = np.arange(RT_H, dtype=np.float64)
    log_gamma = np.log1p(-np.exp2(-5.0 - h))
    n = np.arange(ts, dtype=np.float64)
    scale = RT_DK ** -0.5
    dist = np.abs(n[:, None] - n[None, :])
    visible = (n[None, :] // CHUNK) <= (n[:, None] // CHUNK)
    mdec = scale * np.exp(log_gamma[:, None, None] * dist[None]) * visible[None]
    qd = np.exp(log_gamma[:, None] * (n[None, :] + 1.0))
    kd = scale * np.exp(log_gamma[:, None] * (ts - 1.0 - n[None, :]))
    rd = np.exp(log_gamma * ts)

    def pair_lanes(a):
        a = np.repeat(a[:, :, None], RT_DK, axis=2)
        return np.concatenate([a[0::2], a[1::2]], axis=2)

    rdec = np.repeat(np.repeat(rd[:, None], RT_DK, axis=1).reshape(RT_H // 2, 2 * RT_DK, 1), LANES, axis=2)
    return (jnp.asarray(mdec, F32), jnp.asarray(pair_lanes(qd), F32), jnp.asarray(pair_lanes(kd), F32),
            jnp.asarray(rdec, F32))


def _hgrn_chunk(proj_ref, gc_ref, mixed_ref, hg_g_ref, rows, causal, get_st, set_st, proj):
    mid = CHUNK // 2 - 1
    heads = range(HG_H)
    q_rel, k_rel, q_abs, k_end, v_b, dec = [], [], [], [], [], []
    for h in heads:
        q = _silu(proj_ref[rows, _cols(OFF_HG_Q, h, HG_DK)])
        k = proj_ref[rows, _cols(OFF_HG_F, h, HG_DK)]
        gc = gc_ref[rows, _cols(0, h, HG_DK)]
        g_mid = gc[mid:mid + 1, :]
        g_end = gc[CHUNK - 1:CHUNK, :]
        q_rel.append((q * jnp.exp2(gc - g_mid)).astype(BF16))
        k_rel.append((k * jnp.exp2(g_mid - gc)).astype(BF16))
        q_abs.append((q * jnp.exp2(gc)).astype(BF16))
        k_end.append((k * jnp.exp2(g_end - gc)).astype(BF16))
        v_b.append(proj_ref[rows, _cols(OFF_HG_I, h, HG_DV)].astype(BF16))
        dec.append(jnp.exp2(g_end))
    proj.emit(PROJ_TILES_PER_HGRN_STAGE)
    k_pad = jnp.zeros((LANES - CHUNK, HG_DK), BF16)
    a = [lax.dot_general(q_rel[h], jnp.concatenate([k_rel[h], k_pad], axis=0), _NT, preferred_element_type=F32)
         for h in heads]
    st = [get_st(h) for h in heads]
    v_t = [jnp.concatenate([v_b[h], jnp.zeros((LANES - CHUNK, HG_DV), BF16)], axis=0).T for h in heads]
    kv = [jnp.dot(v_t[h][:, :CHUNK], k_end[h], preferred_element_type=F32) for h in heads]
    for h in heads:
        set_st(h, dec[h] * st[h] + kv[h])
    a_m = [jnp.where(causal, a[h], 0.0).astype(BF16) for h in heads]
    o = [lax.dot_general(jnp.concatenate([q_abs[h], a_m[h]], axis=1),
                         jnp.concatenate([st[h].astype(BF16), v_t[h]], axis=1), _NT, preferred_element_type=F32)
         for h in heads]
    ms = [jnp.mean(o[h] * o[h], axis=-1, keepdims=True) for h in heads]
    proj.emit(PROJ_TILES_PER_HGRN_STAGE)
    for h in heads:
        gate = _silu(proj_ref[rows, _cols(OFF_HG_GATE, h, HG_DV)].astype(BF16))
        on = o[h] * lax.rsqrt(ms[h] + EPS) * hg_g_ref[:, _cols(0, h, HG_DV)]
        mixed_ref[rows, _cols(0, h, HG_DV)] = on.astype(BF16) * gate


def _retention(n_streams, ts, proj_ref, mixed_ref, cos_t, sin_t, mdec_ref, qdec_ref, kdec_ref, rdec_ref,
               rt_g_ref, get_r, set_r):
    lane = lax.broadcasted_iota(jnp.int32, (ts, LANES), 1)
    first_half = (lane % RT_DK) < (RT_DK // 2)
    low_head = lane < RT_DK
    low_rows = lax.broadcasted_iota(jnp.int32, (2 * RT_DK, RT_DV), 0) < RT_DK

    def rope(xs, cs, sn):
        swapped = jnp.where(first_half, pltpu.roll(xs, LANES - RT_DK // 2, 1), pltpu.roll(xs, RT_DK // 2, 1))
        return xs * cs + swapped * sn

    pairs = [(s, p) for s in range(n_streams) for p in range(RT_H // 2)]
    insts = [(s, p, i) for (s, p) in pairs for i in range(2)]
    rows = {s: slice(s * ts, (s + 1) * ts) for s in range(n_streams)}
    qp, kp_b, qd, kd_b, r_pair, v_pair = {}, {}, {}, {}, {}, {}
    for (s, p) in pairs:
        cs, sn = cos_t[rows[s], :], sin_t[rows[s], :]
        q = rope(proj_ref[rows[s], _cols(OFF_RT_Q, p, LANES)], cs, sn)
        k = rope(proj_ref[rows[s], _cols(OFF_RT_K, p, LANES)], cs, sn)
        qp[s, p] = q
        qd[s, p] = q * qdec_ref[p]
        kp_b[s, p] = k.astype(BF16)
        kd_b[s, p] = (k * kdec_ref[p]).astype(BF16)
        r_pair[s, p] = get_r(s, p)
        v_pair[s, p] = proj_ref[rows[s], _cols(OFF_RT_V, p, 2 * RT_DV)].astype(BF16)
    yield
    sel = {0: low_head, 1: jnp.logical_not(low_head)}
    q_h = {(s, p, i): jnp.where(sel[i], qp[s, p], 0.0).astype(BF16) for (s, p, i) in insts}
    qd_h = {(s, p, i): jnp.where(sel[i], qd[s, p], 0.0).astype(BF16) for (s, p, i) in insts}
    sc = {(s, p, i): lax.dot_general(q_h[s, p, i], kp_b[s, p], _NT, preferred_element_type=F32)
          for (s, p, i) in insts}
    inter = {(s, p, i): jnp.dot(qd_h[s, p, i], r_pair[s, p].astype(BF16), preferred_element_type=F32)
             for (s, p, i) in insts}
    kv = {(s, p): lax.dot_general(kd_b[s, p], v_pair[s, p], _TN, preferred_element_type=F32) for (s, p) in pairs}
    yield
    for (s, p) in pairs:
        dr = jnp.where(low_rows, kv[s, p][:, :RT_DV], kv[s, p][:, RT_DV:])
        set_r(s, p, rdec_ref[p] * r_pair[s, p] + dr)
    sc_b = {(s, p, i): (sc[s, p, i] * mdec_ref[2 * p + i]).astype(BF16) for (s, p, i) in insts}
    yield
    o = {(s, p, i): jnp.dot(sc_b[s, p, i], v_pair[s, p][:, i * RT_DV:(i + 1) * RT_DV],
                            preferred_element_type=F32) + inter[s, p, i] for (s, p, i) in insts}
    yield
    mu = {k: jnp.mean(o[k], axis=-1, keepdims=True) for k in insts}
    cen = {k: o[k] - mu[k] for k in insts}
    var = {k: jnp.mean(cen[k] * cen[k], axis=-1, keepdims=True) for k in insts}
    yield
    for (s, p, i) in insts:
        h = 2 * p + i
        gate = _silu(proj_ref[rows[s], _cols(OFF_RT_GATE, h, RT_DV)].astype(BF16))
        on = cen[s, p, i] * lax.rsqrt(var[s, p, i] + EPS) * rt_g_ref[:, _cols(0, h, RT_DV)]
        mixed_ref[rows[s], _cols(HG_W, h, RT_DV)] = on.astype(BF16) * gate


def _cross_attention(n_streams, ts, proj_ref, mixed_ref, get_mem):
    xa_scale = XA_DH ** -0.5
    insts = [(s, h) for s in range(n_streams) for h in range(XA_H)]
    rows = {s: slice(s * ts, (s + 1) * ts) for s in range(n_streams)}
    mem = {k: get_mem(*k) for k in insts}
    sc = {(s, h): lax.dot_general(proj_ref[rows[s], _cols(OFF_XA_Q, h, XA_DH)].astype(BF16), mem[s, h][0], _NT,
                                  preferred_element_type=F32) * xa_scale for (s, h) in insts}
    yield
    mx = {k: jnp.max(sc[k], axis=-1, keepdims=True) for k in insts}
    yield
    pr = {k: jnp.exp(sc[k] - mx[k]) for k in insts}
    den = {k: jnp.sum(pr[k], axis=-1, keepdims=True) for k in insts}
    yield
    o = {k: jnp.dot(pr[k].astype(BF16), mem[k][1], preferred_element_type=F32) for k in insts}
    yield
    for (s, h) in insts:
        gate = _silu(proj_ref[rows[s], _cols(OFF_XA_GATE, h, XA_DH)].astype(BF16))
        mixed_ref[rows[s], _cols(HG_W + RT_W, h, XA_DH)] = (o[s, h] / den[s, h]).astype(BF16) * gate


def _chunk_cumsum(g):
    tokens, c = g.shape
    x = g.reshape(tokens // SUBLANES, SUBLANES, c)
    sub = lax.broadcasted_iota(jnp.int32, x.shape, 1)
    shift = 1
    while shift < SUBLANES:
        x = x + jnp.where(sub >= shift, pltpu.roll(x, shift, 1), 0.0)
        shift *= 2
    tiles = CHUNK // SUBLANES
    x = x.reshape(tokens // CHUNK, tiles, SUBLANES, c)
    out = [x[:, 0]]
    for j in range(1, tiles):
        total = out[-1][:, SUBLANES - 1:SUBLANES, :]
        out.append(x[:, j] + total)
    return jnp.stack(out, axis=1).reshape(tokens, c)


class _Projection:
    SEGMENTS = ((OFF_HG_F, OFF_HG_I), (OFF_HG_Q, OFF_HG_F), (OFF_HG_I, OFF_RT_Q), (OFF_RT_Q, D_IN))
    TILES_GATES = HG_W // PROJ_N_TILE
    TILES_HGRN = 4 * HG_W // PROJ_N_TILE
    TILES_ALL = D_IN // PROJ_N_TILE

    def __init__(self, x, norm_g_ref, w_in_ref, xn_ref, proj_ref):
        xn_ref[...] = _rms(x, norm_g_ref[...]).astype(BF16)
        self._refs = (xn_ref, w_in_ref, proj_ref)
        self._pending = [n0 for lo, hi in self.SEGMENTS for n0 in range(lo, hi, PROJ_N_TILE)]
        assert len(self._pending) == self.TILES_ALL
        self._done = 0

    def emit(self, n=1):
        xn_ref, w_in_ref, proj_ref = self._refs
        for _ in range(min(n, len(self._pending))):
            cols = slice(self._pending[0], self._pending.pop(0) + PROJ_N_TILE)
            proj_ref[:, cols] = jnp.dot(xn_ref[...], w_in_ref[:, cols], preferred_element_type=F32)
            self._done += 1

    def require(self, n_tiles):
        self.emit(n_tiles - self._done)


def _mix(n_streams, ts, x_ref, y_ref, cos_t, sin_t, mdec_ref, qdec_ref, kdec_ref, rdec_ref,
         get_mem, lb_ref, hg_g_ref, rt_g_ref, w_out_ref, fin_g_ref,
         proj_ref, gc_ref, mixed_ref, get_st, set_st, get_r, set_r, proj):
    proj.require(proj.TILES_GATES)
    lg = lb_ref[...]
    e = jnp.exp(lg - jnp.max(lg, axis=0, keepdims=True))
    lb = e[0:1, :] / jnp.sum(e, axis=0, keepdims=True)
    for c0 in range(0, HG_W, 2 * LANES):
        cols = slice(OFF_HG_F + c0, OFF_HG_F + c0 + 2 * LANES)
        lbc = lb[:, c0:c0 + 2 * LANES]
        half_span = 0.5 * (1.0 - lbc)
        bt = half_span * jnp.tanh(0.5 * proj_ref[:, cols])
        g = jnp.log2(lbc + (half_span + bt))
        proj_ref[:, cols] = half_span - bt
        gc_ref[:, c0:c0 + 2 * LANES] = _chunk_cumsum(g)
        proj.emit(PROJ_TILES_PER_GATE_GROUP)

    proj.require(proj.TILES_HGRN)
    causal = (lax.broadcasted_iota(jnp.int32, (CHUNK, LANES), 1)
              <= lax.broadcasted_iota(jnp.int32, (CHUNK, LANES), 0))
    for s in range(n_streams):
        for c in range(ts // CHUNK):
            rows = slice(s * ts + c * CHUNK, s * ts + (c + 1) * CHUNK)
            _hgrn_chunk(proj_ref, gc_ref, mixed_ref, hg_g_ref, rows, causal,
                        lambda h, s=s: get_st(s, h), lambda h, val, s=s: set_st(s, h, val), proj)

    proj.require(proj.TILES_ALL)
    _run_interleaved(
        _retention(n_streams, ts, proj_ref, mixed_ref, cos_t, sin_t, mdec_ref, qdec_ref, kdec_ref, rdec_ref,
                   rt_g_ref, get_r, set_r),
        _cross_attention(n_streams, ts, proj_ref, mixed_ref, get_mem),
        _out_projection(mixed_ref, w_out_ref, y_ref, 0, HG_W, x_ref))
    sumsq = []
    _run_interleaved(_out_projection(mixed_ref, w_out_ref, y_ref, HG_W, D_MIX, y_ref, sumsq))

    inv = lax.rsqrt(sum(sumsq) * (1.0 / D_MODEL) + EPS)
    y_ref[...] = y_ref[...] * inv * fin_g_ref[...]


def _out_projection(mixed_ref, w_out_ref, acc_ref, k_lo, k_hi, base_ref, sumsq=None):
    for n0 in range(0, D_MODEL, OUT_N_TILE):
        cols = slice(n0, n0 + OUT_N_TILE)
        out = base_ref[:, cols] + jnp.dot(mixed_ref[:, k_lo:k_hi], w_out_ref[k_lo:k_hi, cols],
                                          preferred_element_type=F32)
        acc_ref[:, cols] = out
        if sumsq is not None:
            sumsq.append(jnp.sum(out * out, axis=-1, keepdims=True))
        yield


def _run_interleaved(*stage_generators):
    pending = list(stage_generators)
    while pending:
        for g in list(pending):
            try:
                next(g)
            except StopIteration:
                pending.remove(g)


def _memory_kv_kernel(mem_ref, g_ref, wk_ref, wv_ref, k_ref, v_ref, kb_ref, vb_ref):
    m = _rms(mem_ref[0], g_ref[...]).astype(BF16)
    k = jnp.dot(m, wk_ref[...], preferred_element_type=F32)
    v = jnp.dot(m, wv_ref[...], preferred_element_type=F32)
    for h in range(XA_H):
        k_ref[0, pl.ds(h, N_MEM, stride=XA_H), :] = k[:, _cols(0, h, XA_DH)]
        v_ref[0, pl.ds(h, N_MEM, stride=XA_H), :] = v[:, _cols(0, h, XA_DH)]
    kb_ref[0] = k.astype(BF16)
    vb_ref[0] = v.astype(BF16)


def _prompt_kernel(x_ref, cos_ref, sin_ref, mdec_ref, qdec_ref, kdec_ref, rdec_ref, mk_ref, mv_ref,
                   norm_g_ref, w_in_ref, lb_ref, hg_g_ref, rt_g_ref, w_out_ref, fin_g_ref,
                   y_ref, s_out_ref, r_out_ref, proj_ref, xn_ref, gc_ref, mixed_ref, st_ref):
    j = pl.program_id(1)

    @pl.when(j == 0)
    def _():
        st_ref[...] = jnp.zeros_like(st_ref)
        r_out_ref[...] = jnp.zeros_like(r_out_ref)

    proj = _Projection(x_ref[0], norm_g_ref, w_in_ref, xn_ref, proj_ref)

    def set_st(s, h, val):
        st_ref[h] = val

    def set_r(s, p, val):
        r_out_ref[0, p] = val

    def get_mem(s, h):
        return mk_ref[0, :, _cols(0, h, XA_DH)], mv_ref[0, :, _cols(0, h, XA_DH)]

    _mix(
        1, TOKEN_BLOCK, x_ref.at[0], y_ref.at[0], cos_ref[...], sin_ref[...], mdec_ref, qdec_ref, kdec_ref, rdec_ref,
        get_mem, lb_ref, hg_g_ref, rt_g_ref, w_out_ref, fin_g_ref,
        proj_ref, gc_ref, mixed_ref,
        lambda s, h: st_ref[h], set_st, lambda s, p: r_out_ref[0, p], set_r, proj)

    @pl.when(j == pl.num_programs(1) - 1)
    def _():
        for h in range(HG_H):
            s_out_ref[0, h] = st_ref[h].T


def _sample_kernel(x_ref, cos_ref, sin_ref, mdec_ref, qdec_ref, kdec_ref, rdec_ref, mk_ref, mv_ref,
                   s_in_ref, r_in_ref,
                   norm_g_ref, w_in_ref, lb_ref, hg_g_ref, rt_g_ref, w_out_ref, fin_g_ref,
                   y_ref, s_out_ref, r_out_ref, proj_ref, xn_ref, gc_ref, mixed_ref):
    n_streams = TOKEN_BLOCK // CHUNK

    def set_st(s, h, val):
        s_out_ref[s, h] = val.T

    def set_r(s, p, val):
        r_out_ref[s, p] = val

    def get_mem(s, h):
        head_rows = pl.ds(h, N_MEM, stride=XA_H)
        return mk_ref[s, head_rows, :].astype(BF16), mv_ref[s, head_rows, :].astype(BF16)

    cos_t = jnp.concatenate([cos_ref[...]] * n_streams, axis=0)
    sin_t = jnp.concatenate([sin_ref[...]] * n_streams, axis=0)
    proj = _Projection(x_ref[...], norm_g_ref, w_in_ref, xn_ref, proj_ref)
    _mix(
        n_streams, CHUNK, x_ref, y_ref, cos_t, sin_t, mdec_ref, qdec_ref, kdec_ref, rdec_ref,
        get_mem, lb_ref, hg_g_ref, rt_g_ref, w_out_ref, fin_g_ref,
        proj_ref, gc_ref, mixed_ref,
        lambda s, h: s_in_ref[s, h].T, set_st, lambda s, p: r_in_ref[s, p], set_r, proj)


def _const_spec(shape):
    zeros = (0,) * len(shape)
    return pl.BlockSpec(shape, lambda *_: zeros, pipeline_mode=pl.Buffered(1))


def _layer_scratch():
    return [pltpu.VMEM((TOKEN_BLOCK, D_IN), F32),
            pltpu.VMEM((TOKEN_BLOCK, D_MODEL), BF16),
            pltpu.VMEM((TOKEN_BLOCK, HG_W), F32),
            pltpu.VMEM((TOKEN_BLOCK, D_MIX), BF16)]


def _memory_kv(mem, g, wk, wv):
    b = mem.shape[0]
    out_f = jax.ShapeDtypeStruct((b, N_MEM * XA_H, XA_DH), F32)
    out_b = jax.ShapeDtypeStruct((b, N_MEM, XA_W), BF16)
    blk_f = pl.BlockSpec((1, N_MEM * XA_H, XA_DH), lambda i: (i, 0, 0))
    blk_b = pl.BlockSpec((1, N_MEM, XA_W), lambda i: (i, 0, 0))
    return pl.pallas_call(
        _memory_kv_kernel,
        grid=(b,),
        in_specs=[pl.BlockSpec((1, N_MEM, D_MODEL), lambda i: (i, 0, 0)),
                  pl.BlockSpec((1, D_MODEL), lambda i: (0, 0)),
                  pl.BlockSpec((D_MODEL, XA_W), lambda i: (0, 0)),
                  pl.BlockSpec((D_MODEL, XA_W), lambda i: (0, 0))],
        out_specs=[blk_f, blk_f, blk_b, blk_b],
        out_shape=[out_f, out_f, out_b, out_b],
        name="memory_kv",
    )(mem, g.reshape(1, D_MODEL), wk.astype(BF16), wv.astype(BF16))


def _prompt_layer(x, mk_b, mv_b, weights):
    b, t, _ = x.shape
    tb = TOKEN_BLOCK
    cos_t, sin_t = _rope_tables(np.arange(t))
    mdec, qdec, kdec, rdec = _retention_tables(tb)
    consts = (mdec, qdec, kdec, rdec)
    in_specs = ([pl.BlockSpec((1, tb, D_MODEL), lambda i, j: (i, j, 0)),
                 pl.BlockSpec((tb, LANES), lambda i, j: (j, 0)),
                 pl.BlockSpec((tb, LANES), lambda i, j: (j, 0))]
                + [_const_spec(c.shape) for c in consts]
                + [pl.BlockSpec((1, N_MEM, XA_W), lambda i, j: (i, 0, 0)),
                   pl.BlockSpec((1, N_MEM, XA_W), lambda i, j: (i, 0, 0))]
                + [_const_spec(w.shape) for w in weights])
    out_shape = [jax.ShapeDtypeStruct((b, t, D_MODEL), F32),
                 jax.ShapeDtypeStruct((b, HG_H, HG_DK, HG_DV), F32),
                 jax.ShapeDtypeStruct((b, RT_H // 2, 2 * RT_DK, RT_DV), F32)]
    out_specs = [pl.BlockSpec((1, tb, D_MODEL), lambda i, j: (i, j, 0)),
                 pl.BlockSpec((1, HG_H, HG_DK, HG_DV), lambda i, j: (i, 0, 0, 0)),
                 pl.BlockSpec((1, RT_H // 2, 2 * RT_DK, RT_DV), lambda i, j: (i, 0, 0, 0))]
    return pl.pallas_call(
        _prompt_kernel,
        grid=(b, t // tb),
        in_specs=in_specs,
        out_specs=out_specs,
        out_shape=out_shape,
        scratch_shapes=_layer_scratch() + [pltpu.VMEM((HG_H, HG_DV, HG_DK), F32)],
        compiler_params=pltpu.CompilerParams(dimension_semantics=("arbitrary", "arbitrary"),
                                             vmem_limit_bytes=VMEM_LIMIT_BYTES),
        name="prompt_layer",
    )(x, cos_t, sin_t, *consts, mk_b, mv_b, *weights)


def _sample_layer(x, s_hg, s_rt, mem_k, mem_v, weights):
    b, t, _ = x.shape
    assert t == CHUNK
    tb = TOKEN_BLOCK
    nsb = tb // t
    cos_t, sin_t = _rope_tables(PAST_LEN + np.arange(t))
    mdec, qdec, kdec, rdec = _retention_tables(t)
    consts = (mdec, qdec, kdec, rdec)
    s_rt2 = s_rt.reshape(b, RT_H // 2, 2 * RT_DK, RT_DV)
    mem_spec = pl.BlockSpec((nsb, N_MEM * XA_H, XA_DH), lambda i: (i, 0, 0))
    in_specs = ([pl.BlockSpec((tb, D_MODEL), lambda i: (i, 0)),
                 _const_spec(cos_t.shape), _const_spec(sin_t.shape)]
                + [_const_spec(c.shape) for c in consts]
                + [mem_spec, mem_spec,
                   pl.BlockSpec((nsb, HG_H, HG_DK, HG_DV), lambda i: (i, 0, 0, 0)),
                   pl.BlockSpec((nsb, RT_H // 2, 2 * RT_DK, RT_DV), lambda i: (i, 0, 0, 0))]
                + [_const_spec(w.shape) for w in weights])
    out_shape = [jax.ShapeDtypeStruct((b * t, D_MODEL), F32),
                 jax.ShapeDtypeStruct((b, HG_H, HG_DK, HG_DV), F32),
                 jax.ShapeDtypeStruct((b, RT_H // 2, 2 * RT_DK, RT_DV), F32)]
    out_specs = [pl.BlockSpec((tb, D_MODEL), lambda i: (i, 0)),
                 pl.BlockSpec((nsb, HG_H, HG_DK, HG_DV), lambda i: (i, 0, 0, 0)),
                 pl.BlockSpec((nsb, RT_H // 2, 2 * RT_DK, RT_DV), lambda i: (i, 0, 0, 0))]
    y, s_new, r_new = pl.pallas_call(
        _sample_kernel,
        grid=(b // nsb,),
        in_specs=in_specs,
        out_specs=out_specs,
        out_shape=out_shape,
        scratch_shapes=_layer_scratch(),
        compiler_params=pltpu.CompilerParams(dimension_semantics=("arbitrary",),
                                             vmem_limit_bytes=VMEM_LIMIT_BYTES),
        name="sample_layer",
    )(x.reshape(b * t, D_MODEL), cos_t, sin_t, *consts,
      mem_k.reshape(b, N_MEM * XA_H, XA_DH), mem_v.reshape(b, N_MEM * XA_H, XA_DH), s_hg, s_rt2, *weights)
    return y.reshape(b, t, D_MODEL), s_new, r_new


def kernel(x_prompt, x_sample, mem_prompt, state_hgrn, state_ret, cache_mem_k, cache_mem_v, norm_g, w_in,
           lb_logits, hg_norm_g, rt_norm_g, mem_norm_g, w_mem_k, w_mem_v, w_out, final_norm_g):
    depth = w_in.shape[0]
    assert depth == 1, "single-layer step"
    bp = x_prompt.shape[0]
    bs = x_sample.shape[0]
    weights = (norm_g[0].reshape(1, D_MODEL), w_in[0].astype(BF16), lb_logits,
               hg_norm_g[0].reshape(1, HG_W), rt_norm_g[0].reshape(1, RT_W), w_out[0].astype(BF16),
               final_norm_g.reshape(1, D_MODEL))
    mk, mv, mk_b, mv_b = _memory_kv(mem_prompt, mem_norm_g[0], w_mem_k[0], w_mem_v[0])
    y_p, s_p, r_p = _prompt_layer(x_prompt, mk_b, mv_b, weights)
    y_s, s_s, r_s = _sample_layer(x_sample, state_hgrn[0], state_ret[0], cache_mem_k[0], cache_mem_v[0], weights)
    return (y_p, y_s,
            s_p.reshape(1, bp, HG_H, HG_DK, HG_DV),
            r_p.reshape(1, bp, RT_H, RT_DK, RT_DV),
            mk.reshape(1, bp, N_MEM, XA_H, XA_DH),
            mv.reshape(1, bp, N_MEM, XA_H, XA_DH),
            s_s.reshape(1, bs, HG_H, HG_DK, HG_DV),
            r_s.reshape(1, bs, RT_H, RT_DK, RT_DV))
```

```python
import functools

import numpy as np
import jax
import jax.numpy as jnp
from jax import lax
from jax.experimental import pallas as pl
from jax.experimental.pallas import tpu as pltpu

F32 = jnp.float32
BF16 = jnp.bfloat16

D_MODEL = 1024
PAST_LEN = 4096
CHUNK = 64
N_MEM = 256
EPS = 1e-6
ROPE_BASE = 10000.0
HG_H, HG_DK, HG_DV = 8, 128, 128
HG_W = HG_H * HG_DV
RT_H, RT_DK, RT_DV = 4, 64, 128
RT_QK = RT_H * RT_DK
RT_W = RT_H * RT_DV
XA_H, XA_DH = 4, 128
XA_W = XA_H * XA_DH
D_MIX = HG_W + RT_W + XA_W
OFF_HG_Q = 0
OFF_HG_F = OFF_HG_Q + HG_W
OFF_HG_I = OFF_HG_F + HG_W
OFF_HG_GATE = OFF_HG_I + HG_W
OFF_RT_Q = OFF_HG_GATE + HG_W
OFF_RT_K = OFF_RT_Q + RT_QK
OFF_RT_V = OFF_RT_K + RT_QK
OFF_RT_GATE = OFF_RT_V + RT_W
OFF_XA_Q = OFF_RT_GATE + RT_W
OFF_XA_GATE = OFF_XA_Q + XA_W
D_IN = OFF_XA_GATE + XA_W

LANES = 128
SUBLANES = 8
TOKEN_BLOCK = 256
PROMPT_BLOCKS_PER_STEP = 2
PROJ_N_TILE = 512
PROJ_TILES_PER_GATE_GROUP = 2
PROJ_TILES_PER_HGRN_STAGE = 1
OUT_N_TILE = 512
VMEM_LIMIT_BYTES = 56 * 1024 * 1024

_NT = (((1,), (1,)), ((), ()))
_TN = (((0,), (0,)), ((), ()))


def _silu(x):
    h = 0.5 * x
    return h + h * jnp.tanh(h)


def _rms(x, g):
    return x * lax.rsqrt(jnp.mean(x * x, axis=-1, keepdims=True) + EPS) * g


def _cols(off, i, width):
    return slice(off + i * width, off + (i + 1) * width)


def _rope_tables(pos):
    half = RT_DK // 2
    inv_freq = ROPE_BASE ** (-np.arange(half, dtype=np.float64) / half)
    ang = np.asarray(pos, np.float64)[:, None] * inv_freq[None, :]
    cos, sin = np.cos(ang), np.sin(ang)
    cos_t = np.concatenate([cos, cos, cos, cos], axis=-1)
    sin_t = np.concatenate([-sin, sin, -sin, sin], axis=-1)
    return jnp.asarray(cos_t, F32), jnp.asarray(sin_t, F32)


def _retention_tables(ts):
    h = np.arange(RT_H, dtype=np.float64)
    log_gamma = np.log1p(-np.exp2(-5.0 - h))
    n = np.arange(ts, dtype=np.float64)
    scale = RT_DK ** -0.5
    dist = np.abs(n[:, None] - n[None, :])
    visible = (n[None, :] // CHUNK) <= (n[:, None] // CHUNK)
    mdec = scale * np.exp(log_gamma[:, None, None] * dist[None]) * visible[None]
    qd = np.exp(log_gamma[:, None] * (n[None, :] + 1.0))
    kd = scale * np.exp(log_gamma[:, None] * (ts - 1.0 - n[None, :]))
    rd = np.exp(log_gamma * ts)

    def pair_lanes(a):
        a = np.repeat(a[:, :, None], RT_DK, axis=2)
        return np.concatenate([a[0::2], a[1::2]], axis=2)

    rdec = np.repeat(np.repeat(rd[:, None], RT_DK, axis=1).reshape(RT_H // 2, 2 * RT_DK, 1), LANES, axis=2)
    return (jnp.asarray(mdec, F32), jnp.asarray(pair_lanes(qd), F32), jnp.asarray(pair_lanes(kd), F32),
            jnp.asarray(rdec, F32))


def _hgrn_chunk(proj_ref, gc_ref, mixed_ref, hg_g_ref, rows, causal, get_st, set_st, proj):
    mid = CHUNK // 2 - 1
    heads = range(HG_H)
    q_rel, k_rel, q_abs, k_end, v_b, dec = [], [], [], [], [], []
    for h in heads:
        q = _silu(proj_ref[rows, _cols(OFF_HG_Q, h, HG_DK)])
        k = proj_ref[rows, _cols(OFF_HG_F, h, HG_DK)]
        gc = gc_ref[rows, _cols(0, h, HG_DK)]
        g_mid = gc[mid:mid + 1, :]
        g_end = gc[CHUNK - 1:CHUNK, :]
        q_rel.append((q * jnp.exp2(gc - g_mid)).astype(BF16))
        k_rel.append((k * jnp.exp2(g_mid - gc)).astype(BF16))
        q_abs.append((q * jnp.exp2(gc)).astype(BF16))
        k_end.append((k * jnp.exp2(g_end - gc)).astype(BF16))
        v_b.append(proj_ref[rows, _cols(OFF_HG_I, h, HG_DV)].astype(BF16))
        dec.append(jnp.exp2(g_end))
    proj.emit(PROJ_TILES_PER_HGRN_STAGE)
    k_pad = jnp.zeros((LANES - CHUNK, HG_DK), BF16)
    a = [lax.dot_general(q_rel[h], jnp.concatenate([k_rel[h], k_pad], axis=0), _NT, preferred_element_type=F32)
         for h in heads]
    st = [get_st(h) for h in heads]
    v_t = [jnp.concatenate([v_b[h], jnp.zeros((LANES - CHUNK, HG_DV), BF16)], axis=0).T for h in heads]
    kv = [jnp.dot(v_t[h][:, :CHUNK], k_end[h], preferred_element_type=F32) for h in heads]
    for h in heads:
        set_st(h, dec[h] * st[h] + kv[h])
    a_m = [jnp.where(causal, a[h], 0.0).astype(BF16) for h in heads]
    o = [lax.dot_general(jnp.concatenate([q_abs[h], a_m[h]], axis=1),
                         jnp.concatenate([st[h].astype(BF16), v_t[h]], axis=1), _NT, preferred_element_type=F32)
         for h in heads]
    ms = [jnp.mean(o[h] * o[h], axis=-1, keepdims=True) for h in heads]
    proj.emit(PROJ_TILES_PER_HGRN_STAGE)
    for h in heads:
        gate = _silu(proj_ref[rows, _cols(OFF_HG_GATE, h, HG_DV)].astype(BF16))
        on = o[h] * lax.rsqrt(ms[h] + EPS) * hg_g_ref[:, _cols(0, h, HG_DV)]
        mixed_ref[rows, _cols(0, h, HG_DV)] = on.astype(BF16) * gate


def _retention(n_streams, ts, proj_ref, mixed_ref, cos_t, sin_t, mdec_ref, qdec_ref, kdec_ref, rdec_ref,
               rt_g_ref, get_r, set_r):
    lane = lax.broadcasted_iota(jnp.int32, (ts, LANES), 1)
    first_half = (lane % RT_DK) < (RT_DK // 2)
    low_head = lane < RT_DK
    low_rows = lax.broadcasted_iota(jnp.int32, (2 * RT_DK, RT_DV), 0) < RT_DK

    def rope(xs, cs, sn):
        swapped = jnp.where(first_half, pltpu.roll(xs, LANES - RT_DK // 2, 1), pltpu.roll(xs, RT_DK // 2, 1))
        return xs * cs + swapped * sn

    pairs = [(s, p) for s in range(n_streams) for p in range(RT_H // 2)]
    insts = [(s, p, i) for (s, p) in pairs for i in range(2)]
    rows = {s: slice(s * ts, (s + 1) * ts) for s in range(n_streams)}
    qp, kp_b, qd, kd_b, r_pair, v_pair = {}, {}, {}, {}, {}, {}
    for (s, p) in pairs:
        cs, sn = cos_t[rows[s], :], sin_t[rows[s], :]
        q = rope(proj_ref[rows[s], _cols(OFF_RT_Q, p, LANES)], cs, sn)
        k = rope(proj_ref[rows[s], _cols(OFF_RT_K, p, LANES)], cs, sn)
        qp[s, p] = q
        qd[s, p] = q * qdec_ref[p]
        kp_b[s, p] = k.astype(BF16)
        kd_b[s, p] = (k * kdec_ref[p]).astype(BF16)
        r_pair[s, p] = get_r(s, p)
        v_pair[s, p] = proj_ref[rows[s], _cols(OFF_RT_V, p, 2 * RT_DV)].astype(BF16)
    yield
    sel = {0: low_head, 1: jnp.logical_not(low_head)}
    q_h = {(s, p, i): jnp.where(sel[i], qp[s, p], 0.0).astype(BF16) for (s, p, i) in insts}
    qd_h = {(s, p, i): jnp.where(sel[i], qd[s, p], 0.0).astype(BF16) for (s, p, i) in insts}
    sc = {(s, p, i): lax.dot_general(q_h[s, p, i], kp_b[s, p], _NT, preferred_element_type=F32)
          for (s, p, i) in insts}
    inter = {(s, p, i): jnp.dot(qd_h[s, p, i], r_pair[s, p].astype(BF16), preferred_element_type=F32)
             for (s, p, i) in insts}
    kv = {(s, p): lax.dot_general(kd_b[s, p], v_pair[s, p], _TN, preferred_element_type=F32) for (s, p) in pairs}
    yield
    for (s, p) in pairs:
        dr = jnp.where(low_rows, kv[s, p][:, :RT_DV], kv[s, p][:, RT_DV:])
        set_r(s, p, rdec_ref[p] * r_pair[s, p] + dr)
    sc_b = {(s, p, i): (sc[s, p, i] * mdec_ref[2 * p + i]).astype(BF16) for (s, p, i) in insts}
    yield
    o = {(s, p, i): jnp.dot(sc_b[s, p, i], v_pair[s, p][:, i * RT_DV:(i + 1) * RT_DV],
                            preferred_element_type=F32) + inter[s, p, i] for (s, p, i) in insts}
    yield
    mu = {k: jnp.mean(o[k], axis=-1, keepdims=True) for k in insts}
    cen = {k: o[k] - mu[k] for k in insts}
    var = {k: jnp.mean(cen[k] * cen[k], axis=-1, keepdims=True) for k in insts}
    yield
    for (s, p, i) in insts:
        h = 2 * p + i
        gate = _silu(proj_ref[rows[s], _cols(OFF_RT_GATE, h, RT_DV)].astype(BF16))
        on = cen[s, p, i] * lax.rsqrt(var[s, p, i] + EPS) * rt_g_ref[:, _cols(0, h, RT_DV)]
        mixed_ref[rows[s], _cols(HG_W, h, RT_DV)] = on.astype(BF16) * gate


def _cross_attention(n_streams, ts, proj_ref, mixed_ref, get_mem):
    xa_scale = XA_DH ** -0.5
    insts = [(s, h) for s in range(n_streams) for h in range(XA_H)]
    rows = {s: slice(s * ts, (s + 1) * ts) for s in range(n_streams)}
    mem = {k: get_mem(*k) for k in insts}
    sc = {(s, h): lax.dot_general(proj_ref[rows[s], _cols(OFF_XA_Q, h, XA_DH)].astype(BF16), mem[s, h][0], _NT,
                                  preferred_element_type=F32) * xa_scale for (s, h) in insts}
    yield
    mx = {k: jnp.max(sc[k], axis=-1, keepdims=True) for k in insts}
    yield
    pr = {k: jnp.exp(sc[k] - mx[k]) for k in insts}
    den = {k: jnp.sum(pr[k], axis=-1, keepdims=True) for k in insts}
    yield
    o = {k: jnp.dot(pr[k].astype(BF16), mem[k][1], preferred_element_type=F32) for k in insts}
    yield
    for (s, h) in insts:
        gate = _silu(proj_ref[rows[s], _cols(OFF_XA_GATE, h, XA_DH)].astype(BF16))
        mixed_ref[rows[s], _cols(HG_W + RT_W, h, XA_DH)] = (o[s, h] / den[s, h]).astype(BF16) * gate


def _chunk_cumsum(g):
    tokens, c = g.shape
    x = g.reshape(tokens // SUBLANES, SUBLANES, c)
    sub = lax.broadcasted_iota(jnp.int32, x.shape, 1)
    shift = 1
    while shift < SUBLANES:
        x = x + jnp.where(sub >= shift, pltpu.roll(x, shift, 1), 0.0)
        shift *= 2
    tiles = CHUNK // SUBLANES
    x = x.reshape(tokens // CHUNK, tiles, SUBLANES, c)
    out = [x[:, 0]]
    for j in range(1, tiles):
        total = out[-1][:, SUBLANES - 1:SUBLANES, :]
        out.append(x[:, j] + total)
    return jnp.stack(out, axis=1).reshape(tokens, c)


class _Projection:
    SEGMENTS = ((OFF_HG_F, OFF_HG_I), (OFF_HG_Q, OFF_HG_F), (OFF_HG_I, OFF_RT_Q), (OFF_RT_Q, D_IN))
    TILES_GATES = HG_W // PROJ_N_TILE
    TILES_HGRN = 4 * HG_W // PROJ_N_TILE
    TILES_ALL = D_IN // PROJ_N_TILE

    def __init__(self, x, norm_g_ref, w_in_ref, xn_ref, proj_ref):
        xn_ref[...] = _rms(x, norm_g_ref[...]).astype(BF16)
        self._refs = (xn_ref, w_in_ref, proj_ref)
        self._pending = [n0 for lo, hi in self.SEGMENTS for n0 in range(lo, hi, PROJ_N_TILE)]
        assert len(self._pending) == self.TILES_ALL
        self._done = 0

    def emit(self, n=1):
        xn_ref, w_in_ref, proj_ref = self._refs
        for _ in range(min(n, len(self._pending))):
            cols = slice(self._pending[0], self._pending.pop(0) + PROJ_N_TILE)
            proj_ref[:, cols] = jnp.dot(xn_ref[...], w_in_ref[:, cols], preferred_element_type=F32)
            self._done += 1

    def require(self, n_tiles):
        self.emit(n_tiles - self._done)


def _mix(n_streams, ts, x_ref, y_ref, cos_t, sin_t, mdec_ref, qdec_ref, kdec_ref, rdec_ref,
         get_mem, lb_ref, hg_g_ref, rt_g_ref, w_out_ref, fin_g_ref,
         proj_ref, gc_ref, mixed_ref, get_st, set_st, get_r, set_r, proj):
    proj.require(proj.TILES_GATES)
    lg = lb_ref[...]
    e = jnp.exp(lg - jnp.max(lg, axis=0, keepdims=True))
    lb = e[0:1, :] / jnp.sum(e, axis=0, keepdims=True)
    for c0 in range(0, HG_W, 2 * LANES):
        cols = slice(OFF_HG_F + c0, OFF_HG_F + c0 + 2 * LANES)
        lbc = lb[:, c0:c0 + 2 * LANES]
        half_span = 0.5 * (1.0 - lbc)
        bt = half_span * jnp.tanh(0.5 * proj_ref[:, cols])
        g = jnp.log2(lbc + (half_span + bt))
        proj_ref[:, cols] = half_span - bt
        gc_ref[:, c0:c0 + 2 * LANES] = _chunk_cumsum(g)
        proj.emit(PROJ_TILES_PER_GATE_GROUP)

    proj.require(proj.TILES_HGRN)
    causal = (lax.broadcasted_iota(jnp.int32, (CHUNK, LANES), 1)
              <= lax.broadcasted_iota(jnp.int32, (CHUNK, LANES), 0))
    for s in range(n_streams):
        for c in range(ts // CHUNK):
            rows = slice(s * ts + c * CHUNK, s * ts + (c + 1) * CHUNK)
            _hgrn_chunk(proj_ref, gc_ref, mixed_ref, hg_g_ref, rows, causal,
                        lambda h, s=s: get_st(s, h), lambda h, val, s=s: set_st(s, h, val), proj)

    proj.require(proj.TILES_ALL)
    _run_interleaved(
        _retention(n_streams, ts, proj_ref, mixed_ref, cos_t, sin_t, mdec_ref, qdec_ref, kdec_ref, rdec_ref,
                   rt_g_ref, get_r, set_r),
        _cross_attention(n_streams, ts, proj_ref, mixed_ref, get_mem),
        _out_projection(mixed_ref, w_out_ref, y_ref, 0, HG_W, x_ref))
    sumsq = []
    _run_interleaved(_out_projection(mixed_ref, w_out_ref, y_ref, HG_W, D_MIX, y_ref, sumsq))

    inv = lax.rsqrt(sum(sumsq) * (1.0 / D_MODEL) + EPS)
    y_ref[...] = y_ref[...] * inv * fin_g_ref[...]


def _out_projection(mixed_ref, w_out_ref, acc_ref, k_lo, k_hi, base_ref, sumsq=None):
    for n0 in range(0, D_MODEL, OUT_N_TILE):
        cols = slice(n0, n0 + OUT_N_TILE)
        out = base_ref[:, cols] + jnp.dot(mixed_ref[:, k_lo:k_hi], w_out_ref[k_lo:k_hi, cols],
                                          preferred_element_type=F32)
        acc_ref[:, cols] = out
        if sumsq is not None:
            sumsq.append(jnp.sum(out * out, axis=-1, keepdims=True))
        yield


def _run_interleaved(*stage_generators):
    pending = list(stage_generators)
    while pending:
        for g in list(pending):
            try:
                next(g)
            except StopIteration:
                pending.remove(g)


def _memory_kv_kernel(mem_ref, g_ref, wk_ref, wv_ref, k_ref, v_ref, kb_ref, vb_ref):
    m = _rms(mem_ref[0], g_ref[...]).astype(BF16)
    k = jnp.dot(m, wk_ref[...], preferred_element_type=F32)
    v = jnp.dot(m, wv_ref[...], preferred_element_type=F32)
    for h in range(XA_H):
        k_ref[0, pl.ds(h, N_MEM, stride=XA_H), :] = k[:, _cols(0, h, XA_DH)]
        v_ref[0, pl.ds(h, N_MEM, stride=XA_H), :] = v[:, _cols(0, h, XA_DH)]
    kb_ref[0] = k.astype(BF16)
    vb_ref[0] = v.astype(BF16)


def _prompt_kernel(x_ref, cos_ref, sin_ref, mdec_ref, qdec_ref, kdec_ref, rdec_ref, mk_ref, mv_ref,
                   norm_g_ref, w_in_ref, lb_ref, hg_g_ref, rt_g_ref, w_out_ref, fin_g_ref,
                   y_ref, s_out_ref, r_out_ref, proj_ref, xn_ref, gc_ref, mixed_ref, st_ref):
    j = pl.program_id(1)

    @pl.when(j == 0)
    def _():
        st_ref[...] = jnp.zeros_like(st_ref)
        r_out_ref[...] = jnp.zeros_like(r_out_ref)

    def set_st(s, h, val):
        st_ref[h] = val

    def set_r(s, p, val):
        r_out_ref[0, p] = val

    def get_mem(s, h):
        return mk_ref[0, :, _cols(0, h, XA_DH)], mv_ref[0, :, _cols(0, h, XA_DH)]

    def token_block(i, carry):
        rows = pl.ds(pl.multiple_of(i * TOKEN_BLOCK, TOKEN_BLOCK), TOKEN_BLOCK)
        x_blk = x_ref.at[0, rows]
        proj = _Projection(x_blk[...], norm_g_ref, w_in_ref, xn_ref, proj_ref)
        _mix(
            1, TOKEN_BLOCK, x_blk, y_ref.at[0, rows], cos_ref[rows, :], sin_ref[rows, :],
            mdec_ref, qdec_ref, kdec_ref, rdec_ref,
            get_mem, lb_ref, hg_g_ref, rt_g_ref, w_out_ref, fin_g_ref,
            proj_ref, gc_ref, mixed_ref,
            lambda s, h: st_ref[h], set_st, lambda s, p: r_out_ref[0, p], set_r, proj)
        return carry

    lax.fori_loop(0, x_ref.shape[1] // TOKEN_BLOCK, token_block, 0)

    @pl.when(j == pl.num_programs(1) - 1)
    def _():
        for h in range(HG_H):
            s_out_ref[0, h] = st_ref[h].T


def _sample_kernel(x_ref, cos_ref, sin_ref, mdec_ref, qdec_ref, kdec_ref, rdec_ref, mk_ref, mv_ref,
                   s_in_ref, r_in_ref,
                   norm_g_ref, w_in_ref, lb_ref, hg_g_ref, rt_g_ref, w_out_ref, fin_g_ref,
                   y_ref, s_out_ref, r_out_ref, proj_ref, xn_ref, gc_ref, mixed_ref):
    n_streams = TOKEN_BLOCK // CHUNK

    def set_st(s, h, val):
        s_out_ref[s, h] = val.T

    def set_r(s, p, val):
        r_out_ref[s, p] = val

    def get_mem(s, h):
        head_rows = pl.ds(h, N_MEM, stride=XA_H)
        return mk_ref[s, head_rows, :].astype(BF16), mv_ref[s, head_rows, :].astype(BF16)

    cos_t = jnp.concatenate([cos_ref[...]] * n_streams, axis=0)
    sin_t = jnp.concatenate([sin_ref[...]] * n_streams, axis=0)
    proj = _Projection(x_ref[...], norm_g_ref, w_in_ref, xn_ref, proj_ref)
    _mix(
        n_streams, CHUNK, x_ref, y_ref, cos_t, sin_t, mdec_ref, qdec_ref, kdec_ref, rdec_ref,
        get_mem, lb_ref, hg_g_ref, rt_g_ref, w_out_ref, fin_g_ref,
        proj_ref, gc_ref, mixed_ref,
        lambda s, h: s_in_ref[s, h].T, set_st, lambda s, p: r_in_ref[s, p], set_r, proj)


def _const_spec(shape):
    zeros = (0,) * len(shape)
    return pl.BlockSpec(shape, lambda *_: zeros, pipeline_mode=pl.Buffered(1))


def _layer_scratch():
    return [pltpu.VMEM((TOKEN_BLOCK, D_IN), F32),
            pltpu.VMEM((TOKEN_BLOCK, D_MODEL), BF16),
            pltpu.VMEM((TOKEN_BLOCK, HG_W), F32),
            pltpu.VMEM((TOKEN_BLOCK, D_MIX), BF16)]


def _memory_kv(mem, g, wk, wv):
    b = mem.shape[0]
    out_f = jax.ShapeDtypeStruct((b, N_MEM * XA_H, XA_DH), F32)
    out_b = jax.ShapeDtypeStruct((b, N_MEM, XA_W), BF16)
    blk_f = pl.BlockSpec((1, N_MEM * XA_H, XA_DH), lambda i: (i, 0, 0))
    blk_b = pl.BlockSpec((1, N_MEM, XA_W), lambda i: (i, 0, 0))
    return pl.pallas_call(
        _memory_kv_kernel,
        grid=(b,),
        in_specs=[pl.BlockSpec((1, N_MEM, D_MODEL), lambda i: (i, 0, 0)),
                  pl.BlockSpec((1, D_MODEL), lambda i: (0, 0)),
                  pl.BlockSpec((D_MODEL, XA_W), lambda i: (0, 0)),
                  pl.BlockSpec((D_MODEL, XA_W), lambda i: (0, 0))],
        out_specs=[blk_f, blk_f, blk_b, blk_b],
        out_shape=[out_f, out_f, out_b, out_b],
        name="memory_kv",
    )(mem, g.reshape(1, D_MODEL), wk.astype(BF16), wv.astype(BF16))


def _prompt_layer(x, mk_b, mv_b, weights):
    b, t, _ = x.shape
    tb = PROMPT_BLOCKS_PER_STEP * TOKEN_BLOCK
    cos_t, sin_t = _rope_tables(np.arange(t))
    mdec, qdec, kdec, rdec = _retention_tables(TOKEN_BLOCK)
    consts = (mdec, qdec, kdec, rdec)
    in_specs = ([pl.BlockSpec((1, tb, D_MODEL), lambda i, j: (i, j, 0)),
                 pl.BlockSpec((tb, LANES), lambda i, j: (j, 0)),
                 pl.BlockSpec((tb, LANES), lambda i, j: (j, 0))]
                + [_const_spec(c.shape) for c in consts]
                + [pl.BlockSpec((1, N_MEM, XA_W), lambda i, j: (i, 0, 0)),
                   pl.BlockSpec((1, N_MEM, XA_W), lambda i, j: (i, 0, 0))]
                + [_const_spec(w.shape) for w in weights])
    out_shape = [jax.ShapeDtypeStruct((b, t, D_MODEL), F32),
                 jax.ShapeDtypeStruct((b, HG_H, HG_DK, HG_DV), F32),
                 jax.ShapeDtypeStruct((b, RT_H // 2, 2 * RT_DK, RT_DV), F32)]
    out_specs = [pl.BlockSpec((1, tb, D_MODEL), lambda i, j: (i, j, 0)),
                 pl.BlockSpec((1, HG_H, HG_DK, HG_DV), lambda i, j: (i, 0, 0, 0)),
                 pl.BlockSpec((1, RT_H // 2, 2 * RT_DK, RT_DV), lambda i, j: (i, 0, 0, 0))]
    return pl.pallas_call(
        _prompt_kernel,
        grid=(b, t // tb),
        in_specs=in_specs,
        out_specs=out_specs,
        out_shape=out_shape,
        scratch_shapes=_layer_scratch() + [pltpu.VMEM((HG_H, HG_DV, HG_DK), F32)],
        compiler_params=pltpu.CompilerParams(dimension_semantics=("arbitrary", "arbitrary"),
                                             vmem_limit_bytes=VMEM_LIMIT_BYTES),
        name="prompt_layer",
    )(x, cos_t, sin_t, *consts, mk_b, mv_b, *weights)


def _sample_layer(x, s_hg, s_rt, mem_k, mem_v, weights):
    b, t, _ = x.shape
    assert t == CHUNK
    tb = TOKEN_BLOCK
    nsb = tb // t
    cos_t, sin_t = _rope_tables(PAST_LEN + np.arange(t))
    mdec, qdec, kdec, rdec = _retention_tables(t)
    consts = (mdec, qdec, kdec, rdec)
    s_rt2 = s_rt.reshape(b, RT_H // 2, 2 * RT_DK, RT_DV)
    mem_spec = pl.BlockSpec((nsb, N_MEM * XA_H, XA_DH), lambda i: (i, 0, 0))
    in_specs = ([pl.BlockSpec((tb, D_MODEL), lambda i: (i, 0)),
                 _const_spec(cos_t.shape), _const_spec(sin_t.shape)]
                + [_const_spec(c.shape) for c in consts]
                + [mem_spec, mem_spec,
                   pl.BlockSpec((nsb, HG_H, HG_DK, HG_DV), lambda i: (i, 0, 0, 0)),
                   pl.BlockSpec((nsb, RT_H // 2, 2 * RT_DK, RT_DV), lambda i: (i, 0, 0, 0))]
                + [_const_spec(w.shape) for w in weights])
    out_shape = [jax.ShapeDtypeStruct((b * t, D_MODEL), F32),
                 jax.ShapeDtypeStruct((b, HG_H, HG_DK, HG_DV), F32),
                 jax.ShapeDtypeStruct((b, RT_H // 2, 2 * RT_DK, RT_DV), F32)]
    out_specs = [pl.BlockSpec((tb, D_MODEL), lambda i: (i, 0)),
                 pl.BlockSpec((nsb, HG_H, HG_DK, HG_DV), lambda i: (i, 0, 0, 0)),
                 pl.BlockSpec((nsb, RT_H // 2, 2 * RT_DK, RT_DV), lambda i: (i, 0, 0, 0))]
    y, s_new, r_new = pl.pallas_call(
        _sample_kernel,
        grid=(b // nsb,),
        in_specs=in_specs,
        out_specs=out_specs,
        out_shape=out_shape,
        scratch_shapes=_layer_scratch(),
        compiler_params=pltpu.CompilerParams(dimension_semantics=("arbitrary",),
                                             vmem_limit_bytes=VMEM_LIMIT_BYTES),
        name="sample_layer",
    )(x.reshape(b * t, D_MODEL), cos_t, sin_t, *consts,
      mem_k.reshape(b, N_MEM * XA_H, XA_DH), mem_v.reshape(b, N_MEM * XA_H, XA_DH), s_hg, s_rt2, *weights)
    return y.reshape(b, t, D_MODEL), s_new, r_new


def kernel(x_prompt, x_sample, mem_prompt, state_hgrn, state_ret, cache_mem_k, cache_mem_v, norm_g, w_in,
           lb_logits, hg_norm_g, rt_norm_g, mem_norm_g, w_mem_k, w_mem_v, w_out, final_norm_g):
    depth = w_in.shape[0]
    assert depth == 1, "single-layer step"
    bp = x_prompt.shape[0]
    bs = x_sample.shape[0]
    weights = (norm_g[0].reshape(1, D_MODEL), w_in[0].astype(BF16), lb_logits,
               hg_norm_g[0].reshape(1, HG_W), rt_norm_g[0].reshape(1, RT_W), w_out[0].astype(BF16),
               final_norm_g.reshape(1, D_MODEL))
    mk, mv, mk_b, mv_b = _memory_kv(mem_prompt, mem_norm_g[0], w_mem_k[0], w_mem_v[0])
    y_p, s_p, r_p = _prompt_layer(x_prompt, mk_b, mv_b, weights)
    y_s, s_s, r_s = _sample_layer(x_sample, state_hgrn[0], state_ret[0], cache_mem_k[0], cache_mem_v[0], weights)
    return (y_p, y_s,
            s_p.reshape(1, bp, HG_H, HG_DK, HG_DV),
            r_p.reshape(1, bp, RT_H, RT_DK, RT_DV),
            mk.reshape(1, bp, N_MEM, XA_H, XA_DH),
            mv.reshape(1, bp, N_MEM, XA_H, XA_DH),
            s_s.reshape(1, bs, HG_H, HG_DK, HG_DV),
            r_s.reshape(1, bs, RT_H, RT_DK, RT_DV))
```

```python
import functools

import numpy as np
import jax
import jax.numpy as jnp
from jax import lax
from jax.experimental import pallas as pl
from jax.experimental.pallas import tpu as pltpu

F32 = jnp.float32
BF16 = jnp.bfloat16

D_MODEL = 1024
PAST_LEN = 4096
CHUNK = 64
N_MEM = 256
EPS = 1e-6
ROPE_BASE = 10000.0
HG_H, HG_DK, HG_DV = 8, 128, 128
HG_W = HG_H * HG_DV
RT_H, RT_DK, RT_DV = 4, 64, 128
RT_QK = RT_H * RT_DK
RT_W = RT_H * RT_DV
XA_H, XA_DH = 4, 128
XA_W = XA_H * XA_DH
D_MIX = HG_W + RT_W + XA_W
OFF_HG_Q = 0
OFF_HG_F = OFF_HG_Q + HG_W
OFF_HG_I = OFF_HG_F + HG_W
OFF_HG_GATE = OFF_HG_I + HG_W
OFF_RT_Q = OFF_HG_GATE + HG_W
OFF_RT_K = OFF_RT_Q + RT_QK
OFF_RT_V = OFF_RT_K + RT_QK
OFF_RT_GATE = OFF_RT_V + RT_W
OFF_XA_Q = OFF_RT_GATE + RT_W
OFF_XA_GATE = OFF_XA_Q + XA_W
D_IN = OFF_XA_GATE + XA_W

LANES = 128
SUBLANES = 8
TOKEN_BLOCK = 256
PROMPT_BLOCKS_PER_STEP = 2
WEIGHT_PREP_STEPS = 16
PROJ_N_TILE = 512
PROJ_TILES_PER_GATE_GROUP = 2
PROJ_TILES_PER_HGRN_STAGE = 1
OUT_N_TILE = 512
VMEM_LIMIT_BYTES = 56 * 1024 * 1024
_NT = (((1,), (1,)), ((), ()))
_TN = (((0,), (0,)), ((), ()))


def _silu(x):
    h = 0.5 * x
    return h + h * jnp.tanh(h)


def _rms(x, g):
    return x * lax.rsqrt(jnp.mean(x * x, axis=-1, keepdims=True) + EPS) * g


def _cols(off, i, width):
    return slice(off + i * width, off + (i + 1) * width)


def _rope_tables(pos):
    half = RT_DK // 2
    inv_freq = ROPE_BASE ** (-np.arange(half, dtype=np.float64) / half)
    ang = np.asarray(pos, np.float64)[:, None] * inv_freq[None, :]
    cos, sin = np.cos(ang), np.sin(ang)
    cos_t = np.concatenate([cos, cos, cos, cos], axis=-1)
    sin_t = np.concatenate([-sin, sin, -sin, sin], axis=-1)
    return jnp.asarray(cos_t, F32), jnp.asarray(sin_t, F32)


def _retention_tables(ts):
    h = np.arange(RT_H, dtype=np.float64)
    log_gamma = np.log1p(-np.exp2(-5.0 - h))
    n = np.arange(ts, dtype=np.float64)
    scale = RT_DK ** -0.5
    dist = np.abs(n[:, None] - n[None, :])
    visible = (n[None, :] // CHUNK) <= (n[:, None] // CHUNK)
    mdec = scale * np.exp(log_gamma[:, None, None] * dist[None]) * visible[None]
    qd = np.exp(log_gamma[:, None] * (n[None, :] + 1.0))
    kd = scale * np.exp(log_gamma[:, None] * (ts - 1.0 - n[None, :]))
    rd = np.exp(log_gamma * ts)

    def pair_lanes(a):
        a = np.repeat(a[:, :, None], RT_DK, axis=2)
        return np.concatenate([a[0::2], a[1::2]], axis=2)

    rdec = np.repeat(np.repeat(rd[:, None], RT_DK, axis=1).reshape(RT_H // 2, 2 * RT_DK, 1), LANES, axis=2)
    return (jnp.asarray(mdec, F32), jnp.asarray(pair_lanes(qd), F32), jnp.asarray(pair_lanes(kd), F32),
            jnp.asarray(rdec, F32))


def _hgrn_chunk(proj_ref, gc_ref, mixed_ref, hg_g_ref, rows, causal, get_st, set_st, proj):
    mid = CHUNK // 2 - 1
    heads = range(HG_H)
    q_rel, k_rel, q_abs, k_end, v_b, dec = [], [], [], [], [], []
    for h in heads:
        q = _silu(proj_ref[rows, _cols(OFF_HG_Q, h, HG_DK)])
        k = proj_ref[rows, _cols(OFF_HG_F, h, HG_DK)]
        gc = gc_ref[rows, _cols(0, h, HG_DK)]
        g_mid = gc[mid:mid + 1, :]
        g_end = gc[CHUNK - 1:CHUNK, :]
        q_rel.append((q * jnp.exp2(gc - g_mid)).astype(BF16))
        k_rel.append((k * jnp.exp2(g_mid - gc)).astype(BF16))
        q_abs.append((q * jnp.exp2(gc)).astype(BF16))
        k_end.append((k * jnp.exp2(g_end - gc)).astype(BF16))
        v_b.append(proj_ref[rows, _cols(OFF_HG_I, h, HG_DV)].astype(BF16))
        dec.append(jnp.exp2(g_end))
    proj.emit(PROJ_TILES_PER_HGRN_STAGE)
    k_pad = jnp.zeros((LANES - CHUNK, HG_DK), BF16)
    a = [lax.dot_general(q_rel[h], jnp.concatenate([k_rel[h], k_pad], axis=0), _NT, preferred_element_type=F32)
         for h in heads]
    st = [get_st(h) for h in heads]
    v_t = [jnp.concatenate([v_b[h], jnp.zeros((LANES - CHUNK, HG_DV), BF16)], axis=0).T for h in heads]
    kv = [jnp.dot(v_t[h][:, :CHUNK], k_end[h], preferred_element_type=F32) for h in heads]
    for h in heads:
        set_st(h, dec[h] * st[h] + kv[h])
    a_m = [jnp.where(causal, a[h], 0.0).astype(BF16) for h in heads]
    o = [lax.dot_general(jnp.concatenate([q_abs[h], a_m[h]], axis=1),
                         jnp.concatenate([st[h].astype(BF16), v_t[h]], axis=1), _NT, preferred_element_type=F32)
         for h in heads]
    ms = [jnp.mean(o[h] * o[h], axis=-1, keepdims=True) for h in heads]
    proj.emit(PROJ_TILES_PER_HGRN_STAGE)
    for h in heads:
        gate = _silu(proj_ref[rows, _cols(OFF_HG_GATE, h, HG_DV)].astype(BF16))
        on = o[h] * lax.rsqrt(ms[h] + EPS) * hg_g_ref[:, _cols(0, h, HG_DV)]
        mixed_ref[rows, _cols(0, h, HG_DV)] = on.astype(BF16) * gate


def _retention(n_streams, ts, proj_ref, mixed_ref, cos_t, sin_t, mdec_ref, qdec_ref, kdec_ref, rdec_ref,
               rt_g_ref, get_r, set_r):
    lane = lax.broadcasted_iota(jnp.int32, (ts, LANES), 1)
    first_half = (lane % RT_DK) < (RT_DK // 2)
    low_head = lane < RT_DK
    low_rows = lax.broadcasted_iota(jnp.int32, (2 * RT_DK, RT_DV), 0) < RT_DK

    def rope(xs, cs, sn):
        swapped = jnp.where(first_half, pltpu.roll(xs, LANES - RT_DK // 2, 1), pltpu.roll(xs, RT_DK // 2, 1))
        return xs * cs + swapped * sn

    pairs = [(s, p) for s in range(n_streams) for p in range(RT_H // 2)]
    insts = [(s, p, i) for (s, p) in pairs for i in range(2)]
    rows = {s: slice(s * ts, (s + 1) * ts) for s in range(n_streams)}
    qp, kp_b, qd, kd_b, r_pair, v_pair = {}, {}, {}, {}, {}, {}
    for (s, p) in pairs:
        cs, sn = cos_t[rows[s], :], sin_t[rows[s], :]
        q = rope(proj_ref[rows[s], _cols(OFF_RT_Q, p, LANES)], cs, sn)
        k = rope(proj_ref[rows[s], _cols(OFF_RT_K, p, LANES)], cs, sn)
        qp[s, p] = q
        qd[s, p] = q * qdec_ref[p]
        kp_b[s, p] = k.astype(BF16)
        kd_b[s, p] = (k * kdec_ref[p]).astype(BF16)
        r_pair[s, p] = get_r(s, p)
        v_pair[s, p] = proj_ref[rows[s], _cols(OFF_RT_V, p, 2 * RT_DV)].astype(BF16)
    yield
    sel = {0: low_head, 1: jnp.logical_not(low_head)}
    q_h = {(s, p, i): jnp.where(sel[i], qp[s, p], 0.0).astype(BF16) for (s, p, i) in insts}
    qd_h = {(s, p, i): jnp.where(sel[i], qd[s, p], 0.0).astype(BF16) for (s, p, i) in insts}
    sc = {(s, p, i): lax.dot_general(q_h[s, p, i], kp_b[s, p], _NT, preferred_element_type=F32)
          for (s, p, i) in insts}
    inter = {(s, p, i): jnp.dot(qd_h[s, p, i], r_pair[s, p].astype(BF16), preferred_element_type=F32)
             for (s, p, i) in insts}
    kv = {(s, p): lax.dot_general(kd_b[s, p], v_pair[s, p], _TN, preferred_element_type=F32) for (s, p) in pairs}
    yield
    for (s, p) in pairs:
        dr = jnp.where(low_rows, kv[s, p][:, :RT_DV], kv[s, p][:, RT_DV:])
        set_r(s, p, rdec_ref[p] * r_pair[s, p] + dr)
    sc_b = {(s, p, i): (sc[s, p, i] * mdec_ref[2 * p + i]).astype(BF16) for (s, p, i) in insts}
    yield
    o = {(s, p, i): jnp.dot(sc_b[s, p, i], v_pair[s, p][:, i * RT_DV:(i + 1) * RT_DV],
                            preferred_element_type=F32) + inter[s, p, i] for (s, p, i) in insts}
    yield
    mu = {k: jnp.mean(o[k], axis=-1, keepdims=True) for k in insts}
    cen = {k: o[k] - mu[k] for k in insts}
    var = {k: jnp.mean(cen[k] * cen[k], axis=-1, keepdims=True) for k in insts}
    yield
    for (s, p, i) in insts:
        h = 2 * p + i
        gate = _silu(proj_ref[rows[s], _cols(OFF_RT_GATE, h, RT_DV)].astype(BF16))
        on = cen[s, p, i] * lax.rsqrt(var[s, p, i] + EPS) * rt_g_ref[:, _cols(0, h, RT_DV)]
        mixed_ref[rows[s], _cols(HG_W, h, RT_DV)] = on.astype(BF16) * gate


def _cross_attention(n_streams, ts, proj_ref, mixed_ref, get_mem):
    xa_scale = XA_DH ** -0.5
    insts = [(s, h) for s in range(n_streams) for h in range(XA_H)]
    rows = {s: slice(s * ts, (s + 1) * ts) for s in range(n_streams)}
    mem = {k: get_mem(*k) for k in insts}
    sc = {(s, h): lax.dot_general(proj_ref[rows[s], _cols(OFF_XA_Q, h, XA_DH)].astype(BF16), mem[s, h][0], _NT,
                                  preferred_element_type=F32) * xa_scale for (s, h) in insts}
    yield
    mx = {k: jnp.max(sc[k], axis=-1, keepdims=True) for k in insts}
    yield
    pr = {k: jnp.exp(sc[k] - mx[k]) for k in insts}
    den = {k: jnp.sum(pr[k], axis=-1, keepdims=True) for k in insts}
    yield
    o = {k: jnp.dot(pr[k].astype(BF16), mem[k][1], preferred_element_type=F32) for k in insts}
    yield
    for (s, h) in insts:
        gate = _silu(proj_ref[rows[s], _cols(OFF_XA_GATE, h, XA_DH)].astype(BF16))
        mixed_ref[rows[s], _cols(HG_W + RT_W, h, XA_DH)] = (o[s, h] / den[s, h]).astype(BF16) * gate


def _chunk_cumsum(g):
    tokens, c = g.shape
    x = g.reshape(tokens // SUBLANES, SUBLANES, c)
    sub = lax.broadcasted_iota(jnp.int32, x.shape, 1)
    shift = 1
    while shift < SUBLANES:
        x = x + jnp.where(sub >= shift, pltpu.roll(x, shift, 1), 0.0)
        shift *= 2
    tiles = CHUNK // SUBLANES
    x = x.reshape(tokens // CHUNK, tiles, SUBLANES, c)
    out = [x[:, 0]]
    for j in range(1, tiles):
        total = out[-1][:, SUBLANES - 1:SUBLANES, :]
        out.append(x[:, j] + total)
    return jnp.stack(out, axis=1).reshape(tokens, c)


class _Projection:
    SEGMENTS = ((OFF_HG_F, OFF_HG_I), (OFF_HG_Q, OFF_HG_F), (OFF_HG_I, OFF_RT_Q), (OFF_RT_Q, D_IN))
    TILES_GATES = HG_W // PROJ_N_TILE
    TILES_HGRN = 4 * HG_W // PROJ_N_TILE
    TILES_ALL = D_IN // PROJ_N_TILE

    def __init__(self, x, norm_g_ref, w_in_ref, xn_ref, proj_ref):
        xn_ref[...] = _rms(x, norm_g_ref[...]).astype(BF16)
        self._refs = (xn_ref, w_in_ref, proj_ref)
        self._pending = [n0 for lo, hi in self.SEGMENTS for n0 in range(lo, hi, PROJ_N_TILE)]
        assert len(self._pending) == self.TILES_ALL
        self._done = 0

    def emit(self, n=1):
        xn_ref, w_in_ref, proj_ref = self._refs
        for _ in range(min(n, len(self._pending))):
            cols = slice(self._pending[0], self._pending.pop(0) + PROJ_N_TILE)
            proj_ref[:, cols] = jnp.dot(xn_ref[...], w_in_ref[:, cols], preferred_element_type=F32)
            self._done += 1

    def require(self, n_tiles):
        self.emit(n_tiles - self._done)


def _mix(n_streams, ts, x_ref, y_ref, cos_t, sin_t, mdec_ref, qdec_ref, kdec_ref, rdec_ref,
         get_mem, lb_ref, hg_g_ref, rt_g_ref, w_out_ref, fin_g_ref,
         proj_ref, gc_ref, mixed_ref, get_st, set_st, get_r, set_r, proj):
    proj.require(proj.TILES_GATES)
    lg = lb_ref[...]
    e = jnp.exp(lg - jnp.max(lg, axis=0, keepdims=True))
    lb = e[0:1, :] / jnp.sum(e, axis=0, keepdims=True)
    for c0 in range(0, HG_W, 2 * LANES):
        cols = slice(OFF_HG_F + c0, OFF_HG_F + c0 + 2 * LANES)
        lbc = lb[:, c0:c0 + 2 * LANES]
        half_span = 0.5 * (1.0 - lbc)
        bt = half_span * jnp.tanh(0.5 * proj_ref[:, cols])
        g = jnp.log2(lbc + (half_span + bt))
        proj_ref[:, cols] = half_span - bt
        gc_ref[:, c0:c0 + 2 * LANES] = _chunk_cumsum(g)
        proj.emit(PROJ_TILES_PER_GATE_GROUP)

    proj.require(proj.TILES_HGRN)
    causal = (lax.broadcasted_iota(jnp.int32, (CHUNK, LANES), 1)
              <= lax.broadcasted_iota(jnp.int32, (CHUNK, LANES), 0))
    for s in range(n_streams):
        for c in range(ts // CHUNK):
            rows = slice(s * ts + c * CHUNK, s * ts + (c + 1) * CHUNK)
            _hgrn_chunk(proj_ref, gc_ref, mixed_ref, hg_g_ref, rows, causal,
                        lambda h, s=s: get_st(s, h), lambda h, val, s=s: set_st(s, h, val), proj)

    proj.require(proj.TILES_ALL)
    _run_interleaved(
        _retention(n_streams, ts, proj_ref, mixed_ref, cos_t, sin_t, mdec_ref, qdec_ref, kdec_ref, rdec_ref,
                   rt_g_ref, get_r, set_r),
        _cross_attention(n_streams, ts, proj_ref, mixed_ref, get_mem),
        _out_projection(mixed_ref, w_out_ref, y_ref, 0, HG_W, x_ref))
    sumsq = []
    _run_interleaved(_out_projection(mixed_ref, w_out_ref, y_ref, HG_W, D_MIX, y_ref, sumsq))

    inv = lax.rsqrt(sum(sumsq) * (1.0 / D_MODEL) + EPS)
    y_ref[...] = y_ref[...] * inv * fin_g_ref[...]


def _out_projection(mixed_ref, w_out_ref, acc_ref, k_lo, k_hi, base_ref, sumsq=None):
    for n0 in range(0, D_MODEL, OUT_N_TILE):
        cols = slice(n0, n0 + OUT_N_TILE)
        out = base_ref[:, cols] + jnp.dot(mixed_ref[:, k_lo:k_hi], w_out_ref[k_lo:k_hi, cols],
                                          preferred_element_type=F32)
        acc_ref[:, cols] = out
        if sumsq is not None:
            sumsq.append(jnp.sum(out * out, axis=-1, keepdims=True))
        yield


def _run_interleaved(*stage_generators):
    pending = list(stage_generators)
    while pending:
        for g in list(pending):
            try:
                next(g)
            except StopIteration:
                pending.remove(g)


def _memory_kv_kernel(mem_ref, g_ref, wk_ref, wv_ref, k_ref, v_ref, kb_ref, vb_ref):
    m = _rms(mem_ref[0], g_ref[...]).astype(BF16)
    k = jnp.dot(m, wk_ref[...], preferred_element_type=F32)
    v = jnp.dot(m, wv_ref[...], preferred_element_type=F32)
    for h in range(XA_H):
        k_ref[0, pl.ds(h, N_MEM, stride=XA_H), :] = k[:, _cols(0, h, XA_DH)]
        v_ref[0, pl.ds(h, N_MEM, stride=XA_H), :] = v[:, _cols(0, h, XA_DH)]
    kb_ref[0] = k.astype(BF16)
    vb_ref[0] = v.astype(BF16)


def _prompt_kernel(steps_per_stream, w_in_slab_ref, w_out_slab_ref,
                   x_ref, cos_ref, sin_ref, mdec_ref, qdec_ref, kdec_ref, rdec_ref, mk_ref, mv_ref,
                   norm_g_ref, lb_ref, hg_g_ref, rt_g_ref, fin_g_ref,
                   y_ref, s_out_ref, r_out_ref, w_in_b_slab_ref, w_out_b_slab_ref,
                   proj_ref, xn_ref, gc_ref, mixed_ref, st_ref, w_in_ref, w_out_ref):
    t = pl.program_id(0)
    j = (t - WEIGHT_PREP_STEPS) % steps_per_stream
    running = t >= WEIGHT_PREP_STEPS

    @pl.when(jnp.logical_not(running))
    def _():
        for slab_ref, out_ref, full_ref in ((w_in_slab_ref, w_in_b_slab_ref, w_in_ref),
                                            (w_out_slab_ref, w_out_b_slab_ref, w_out_ref)):
            n = slab_ref.shape[0]
            slab = slab_ref[...].astype(BF16)
            full_ref[pl.ds(pl.multiple_of(t * n, n), n), :] = slab
            out_ref[...] = slab

    @pl.when(running & (j == 0))
    def _():
        st_ref[...] = jnp.zeros_like(st_ref)
        r_out_ref[...] = jnp.zeros_like(r_out_ref)

    def set_st(s, h, val):
        st_ref[h] = val

    def set_r(s, p, val):
        r_out_ref[0, p] = val

    def get_mem(s, h):
        return mk_ref[0, :, _cols(0, h, XA_DH)], mv_ref[0, :, _cols(0, h, XA_DH)]

    def token_block(i, carry):
        rows = pl.ds(pl.multiple_of(i * TOKEN_BLOCK, TOKEN_BLOCK), TOKEN_BLOCK)
        x_blk = x_ref.at[0, rows]
        proj = _Projection(x_blk[...], norm_g_ref, w_in_ref, xn_ref, proj_ref)
        _mix(
            1, TOKEN_BLOCK, x_blk, y_ref.at[0, rows], cos_ref[rows, :], sin_ref[rows, :],
            mdec_ref, qdec_ref, kdec_ref, rdec_ref,
            get_mem, lb_ref, hg_g_ref, rt_g_ref, w_out_ref, fin_g_ref,
            proj_ref, gc_ref, mixed_ref,
            lambda s, h: st_ref[h], set_st, lambda s, p: r_out_ref[0, p], set_r, proj)
        return carry

    @pl.when(running)
    def _():
        lax.fori_loop(0, x_ref.shape[1] // TOKEN_BLOCK, token_block, 0)

    @pl.when(running & (j == steps_per_stream - 1))
    def _():
        for h in range(HG_H):
            s_out_ref[0, h] = st_ref[h].T


def _sample_kernel(x_ref, cos_ref, sin_ref, mdec_ref, qdec_ref, kdec_ref, rdec_ref, mk_ref, mv_ref,
                   s_in_ref, r_in_ref,
                   norm_g_ref, w_in_ref, lb_ref, hg_g_ref, rt_g_ref, w_out_ref, fin_g_ref,
                   y_ref, s_out_ref, r_out_ref, proj_ref, xn_ref, gc_ref, mixed_ref):
    n_streams = TOKEN_BLOCK // CHUNK

    def set_st(s, h, val):
        s_out_ref[s, h] = val.T

    def set_r(s, p, val):
        r_out_ref[s, p] = val

    def get_mem(s, h):
        head_rows = pl.ds(h, N_MEM, stride=XA_H)
        return mk_ref[s, head_rows, :].astype(BF16), mv_ref[s, head_rows, :].astype(BF16)

    cos_t = jnp.concatenate([cos_ref[...]] * n_streams, axis=0)
    sin_t = jnp.concatenate([sin_ref[...]] * n_streams, axis=0)
    proj = _Projection(x_ref[...], norm_g_ref, w_in_ref, xn_ref, proj_ref)
    _mix(
        n_streams, CHUNK, x_ref, y_ref, cos_t, sin_t, mdec_ref, qdec_ref, kdec_ref, rdec_ref,
        get_mem, lb_ref, hg_g_ref, rt_g_ref, w_out_ref, fin_g_ref,
        proj_ref, gc_ref, mixed_ref,
        lambda s, h: s_in_ref[s, h].T, set_st, lambda s, p: r_in_ref[s, p], set_r, proj)


def _const_spec(shape):
    zeros = (0,) * len(shape)
    return pl.BlockSpec(shape, lambda *_: zeros, pipeline_mode=pl.Buffered(1))


def _layer_scratch():
    return [pltpu.VMEM((TOKEN_BLOCK, D_IN), F32),
            pltpu.VMEM((TOKEN_BLOCK, D_MODEL), BF16),
            pltpu.VMEM((TOKEN_BLOCK, HG_W), F32),
            pltpu.VMEM((TOKEN_BLOCK, D_MIX), BF16)]


def _memory_kv(mem, g, wk, wv):
    b = mem.shape[0]
    out_f = jax.ShapeDtypeStruct((b, N_MEM * XA_H, XA_DH), F32)
    out_b = jax.ShapeDtypeStruct((b, N_MEM, XA_W), BF16)
    blk_f = pl.BlockSpec((1, N_MEM * XA_H, XA_DH), lambda i: (i, 0, 0))
    blk_b = pl.BlockSpec((1, N_MEM, XA_W), lambda i: (i, 0, 0))
    return pl.pallas_call(
        _memory_kv_kernel,
        grid=(b,),
        in_specs=[pl.BlockSpec((1, N_MEM, D_MODEL), lambda i: (i, 0, 0)),
                  pl.BlockSpec((1, D_MODEL), lambda i: (0, 0)),
                  pl.BlockSpec((D_MODEL, XA_W), lambda i: (0, 0)),
                  pl.BlockSpec((D_MODEL, XA_W), lambda i: (0, 0))],
        out_specs=[blk_f, blk_f, blk_b, blk_b],
        out_shape=[out_f, out_f, out_b, out_b],
        name="memory_kv",
    )(mem, g.reshape(1, D_MODEL), wk.astype(BF16), wv.astype(BF16))


def _prompt_layer(x, mk_b, mv_b, w_in, w_out, params):
    b, t, _ = x.shape
    tb = PROMPT_BLOCKS_PER_STEP * TOKEN_BLOCK
    spp = t // tb
    prep = WEIGHT_PREP_STEPS
    cos_t, sin_t = _rope_tables(np.arange(t))
    mdec, qdec, kdec, rdec = _retention_tables(TOKEN_BLOCK)
    consts = (mdec, qdec, kdec, rdec)
    in_rows, out_rows = D_MODEL // prep, D_MIX // prep

    def slab(step):
        return jnp.minimum(step, prep - 1)

    def run(step):
        return jnp.maximum(step - prep, 0)

    in_specs = ([pl.BlockSpec((in_rows, D_IN), lambda s: (slab(s), 0)),
                 pl.BlockSpec((out_rows, D_MODEL), lambda s: (slab(s), 0)),
                 pl.BlockSpec((1, tb, D_MODEL), lambda s: (run(s) // spp, run(s) % spp, 0)),
                 pl.BlockSpec((tb, LANES), lambda s: (run(s) % spp, 0)),
                 pl.BlockSpec((tb, LANES), lambda s: (run(s) % spp, 0))]
                + [_const_spec(c.shape) for c in consts]
                + [pl.BlockSpec((1, N_MEM, XA_W), lambda s: (run(s) // spp, 0, 0)),
                   pl.BlockSpec((1, N_MEM, XA_W), lambda s: (run(s) // spp, 0, 0))]
                + [_const_spec(p.shape) for p in params])
    out_shape = [jax.ShapeDtypeStruct((b, t, D_MODEL), F32),
                 jax.ShapeDtypeStruct((b, HG_H, HG_DK, HG_DV), F32),
                 jax.ShapeDtypeStruct((b, RT_H // 2, 2 * RT_DK, RT_DV), F32),
                 jax.ShapeDtypeStruct((D_MODEL, D_IN), BF16),
                 jax.ShapeDtypeStruct((D_MIX, D_MODEL), BF16)]
    out_specs = [pl.BlockSpec((1, tb, D_MODEL), lambda s: (run(s) // spp, run(s) % spp, 0)),
                 pl.BlockSpec((1, HG_H, HG_DK, HG_DV), lambda s: (run(s) // spp, 0, 0, 0)),
                 pl.BlockSpec((1, RT_H // 2, 2 * RT_DK, RT_DV), lambda s: (run(s) // spp, 0, 0, 0)),
                 pl.BlockSpec((in_rows, D_IN), lambda s: (slab(s), 0)),
                 pl.BlockSpec((out_rows, D_MODEL), lambda s: (slab(s), 0))]
    return pl.pallas_call(
        functools.partial(_prompt_kernel, spp),
        grid=(prep + b * spp,),
        in_specs=in_specs,
        out_specs=out_specs,
        out_shape=out_shape,
        scratch_shapes=_layer_scratch() + [pltpu.VMEM((HG_H, HG_DV, HG_DK), F32),
                                           pltpu.VMEM((D_MODEL, D_IN), BF16),
                                           pltpu.VMEM((D_MIX, D_MODEL), BF16)],
        compiler_params=pltpu.CompilerParams(dimension_semantics=("arbitrary",),
                                             vmem_limit_bytes=VMEM_LIMIT_BYTES),
        name="prompt_layer",
    )(w_in, w_out, x, cos_t, sin_t, *consts, mk_b, mv_b, *params)


def _sample_layer(x, s_hg, s_rt, mem_k, mem_v, weights):
    b, t, _ = x.shape
    assert t == CHUNK
    tb = TOKEN_BLOCK
    nsb = tb // t
    cos_t, sin_t = _rope_tables(PAST_LEN + np.arange(t))
    mdec, qdec, kdec, rdec = _retention_tables(t)
    consts = (mdec, qdec, kdec, rdec)
    s_rt2 = s_rt.reshape(b, RT_H // 2, 2 * RT_DK, RT_DV)
    mem_spec = pl.BlockSpec((nsb, N_MEM * XA_H, XA_DH), lambda i: (i, 0, 0))
    in_specs = ([pl.BlockSpec((tb, D_MODEL), lambda i: (i, 0)),
                 _const_spec(cos_t.shape), _const_spec(sin_t.shape)]
                + [_const_spec(c.shape) for c in consts]
                + [mem_spec, mem_spec,
                   pl.BlockSpec((nsb, HG_H, HG_DK, HG_DV), lambda i: (i, 0, 0, 0)),
                   pl.BlockSpec((nsb, RT_H // 2, 2 * RT_DK, RT_DV), lambda i: (i, 0, 0, 0))]
                + [_const_spec(w.shape) for w in weights])
    out_shape = [jax.ShapeDtypeStruct((b * t, D_MODEL), F32),
                 jax.ShapeDtypeStruct((b, HG_H, HG_DK, HG_DV), F32),
                 jax.ShapeDtypeStruct((b, RT_H // 2, 2 * RT_DK, RT_DV), F32)]
    out_specs = [pl.BlockSpec((tb, D_MODEL), lambda i: (i, 0)),
                 pl.BlockSpec((nsb, HG_H, HG_DK, HG_DV), lambda i: (i, 0, 0, 0)),
                 pl.BlockSpec((nsb, RT_H // 2, 2 * RT_DK, RT_DV), lambda i: (i, 0, 0, 0))]
    y, s_new, r_new = pl.pallas_call(
        _sample_kernel,
        grid=(b // nsb,),
        in_specs=in_specs,
        out_specs=out_specs,
        out_shape=out_shape,
        scratch_shapes=_layer_scratch(),
        compiler_params=pltpu.CompilerParams(dimension_semantics=("arbitrary",),
                                             vmem_limit_bytes=VMEM_LIMIT_BYTES),
        name="sample_layer",
    )(x.reshape(b * t, D_MODEL), cos_t, sin_t, *consts,
      mem_k.reshape(b, N_MEM * XA_H, XA_DH), mem_v.reshape(b, N_MEM * XA_H, XA_DH), s_hg, s_rt2, *weights)
    return y.reshape(b, t, D_MODEL), s_new, r_new


def kernel(x_prompt, x_sample, mem_prompt, state_hgrn, state_ret, cache_mem_k, cache_mem_v, norm_g, w_in,
           lb_logits, hg_norm_g, rt_norm_g, mem_norm_g, w_mem_k, w_mem_v, w_out, final_norm_g):
    depth = w_in.shape[0]
    assert depth == 1, "single-layer step"
    bp = x_prompt.shape[0]
    bs = x_sample.shape[0]
    g_in, g_hg, g_rt = norm_g[0].reshape(1, D_MODEL), hg_norm_g[0].reshape(1, HG_W), rt_norm_g[0].reshape(1, RT_W)
    g_fin = final_norm_g.reshape(1, D_MODEL)
    mk, mv, mk_b, mv_b = _memory_kv(mem_prompt, mem_norm_g[0], w_mem_k[0], w_mem_v[0])
    y_p, s_p, r_p, w_in_b, w_out_b = _prompt_layer(x_prompt, mk_b, mv_b, w_in[0], w_out[0],
                                                   (g_in, lb_logits, g_hg, g_rt, g_fin))
    y_s, s_s, r_s = _sample_layer(x_sample, state_hgrn[0], state_ret[0], cache_mem_k[0], cache_mem_v[0],
                                  (g_in, w_in_b, lb_logits, g_hg, g_rt, w_out_b, g_fin))
    return (y_p, y_s,
            s_p.reshape(1, bp, HG_H, HG_DK, HG_DV),
            r_p.reshape(1, bp, RT_H, RT_DK, RT_DV),
            mk.reshape(1, bp, N_MEM, XA_H, XA_DH),
            mv.reshape(1, bp, N_MEM, XA_H, XA_DH),
            s_s.reshape(1, bs, HG_H, HG_DK, HG_DV),
            r_s.reshape(1, bs, RT_H, RT_DK, RT_DV))
```

```python
import functools

import numpy as np
import jax
import jax.numpy as jnp
from jax import lax
from jax.experimental import pallas as pl
from jax.experimental.pallas import tpu as pltpu

F32 = jnp.float32
BF16 = jnp.bfloat16

D_MODEL = 1024
PAST_LEN = 4096
CHUNK = 64
N_MEM = 256
EPS = 1e-6
ROPE_BASE = 10000.0
HG_H, HG_DK, HG_DV = 8, 128, 128
HG_W = HG_H * HG_DV
RT_H, RT_DK, RT_DV = 4, 64, 128
RT_QK = RT_H * RT_DK
RT_W = RT_H * RT_DV
XA_H, XA_DH = 4, 128
XA_W = XA_H * XA_DH
D_MIX = HG_W + RT_W + XA_W
OFF_HG_Q = 0
OFF_HG_F = OFF_HG_Q + HG_W
OFF_HG_I = OFF_HG_F + HG_W
OFF_HG_GATE = OFF_HG_I + HG_W
OFF_RT_Q = OFF_HG_GATE + HG_W
OFF_RT_K = OFF_RT_Q + RT_QK
OFF_RT_V = OFF_RT_K + RT_QK
OFF_RT_GATE = OFF_RT_V + RT_W
OFF_XA_Q = OFF_RT_GATE + RT_W
OFF_XA_GATE = OFF_XA_Q + XA_W
D_IN = OFF_XA_GATE + XA_W

LANES = 128
SUBLANES = 8
TOKEN_BLOCK = 256
PROMPT_BLOCKS_PER_STEP = 4
WEIGHT_PREP_STEPS = 16
PROJ_N_TILE = 512
PROJ_TILES_PER_GATE_GROUP = 2
PROJ_TILES_PER_HGRN_STAGE = 1
HGRN_STREAMS_PER_STAGE = 4
OUT_N_TILE = 512
VMEM_LIMIT_BYTES = 56 * 1024 * 1024
_NT = (((1,), (1,)), ((), ()))
_TN = (((0,), (0,)), ((), ()))


def _silu(x):
    h = 0.5 * x
    return h + h * jnp.tanh(h)


def _rms(x, g):
    return x * lax.rsqrt(jnp.mean(x * x, axis=-1, keepdims=True) + EPS) * g


def _cols(off, i, width):
    return slice(off + i * width, off + (i + 1) * width)


def _rope_tables(pos):
    half = RT_DK // 2
    inv_freq = ROPE_BASE ** (-np.arange(half, dtype=np.float64) / half)
    ang = np.asarray(pos, np.float64)[:, None] * inv_freq[None, :]
    cos, sin = np.cos(ang), np.sin(ang)
    cos_t = np.concatenate([cos, cos, cos, cos], axis=-1)
    sin_t = np.concatenate([-sin, sin, -sin, sin], axis=-1)
    return jnp.asarray(cos_t, F32), jnp.asarray(sin_t, F32)


def _retention_tables(ts):
    h = np.arange(RT_H, dtype=np.float64)
    log_gamma = np.log1p(-np.exp2(-5.0 - h))
    n = np.arange(ts, dtype=np.float64)
    scale = RT_DK ** -0.5
    dist = np.abs(n[:, None] - n[None, :])
    visible = (n[None, :] // CHUNK) <= (n[:, None] // CHUNK)
    mdec = scale * np.exp(log_gamma[:, None, None] * dist[None]) * visible[None]
    qd = np.exp(log_gamma[:, None] * (n[None, :] + 1.0))
    kd = scale * np.exp(log_gamma[:, None] * (ts - 1.0 - n[None, :]))
    rd = np.exp(log_gamma * ts)

    def pair_lanes(a):
        a = np.repeat(a[:, :, None], RT_DK, axis=2)
        return np.concatenate([a[0::2], a[1::2]], axis=2)

    rdec = np.repeat(np.repeat(rd[:, None], RT_DK, axis=1).reshape(RT_H // 2, 2 * RT_DK, 1), LANES, axis=2)
    return (jnp.asarray(mdec, F32), jnp.asarray(pair_lanes(qd), F32), jnp.asarray(pair_lanes(kd), F32),
            jnp.asarray(rdec, F32))


def _hgrn_chunks(proj_ref, gc_ref, mixed_ref, hg_g_ref, chunks, causal, get_st, set_st, proj):
    mid = CHUNK // 2 - 1
    insts = [(s, h) for s, _ in chunks for h in range(HG_H)]
    rows = dict(chunks)
    q_rel, k_rel, q_abs, k_end, v_b, dec = {}, {}, {}, {}, {}, {}
    for (s, h) in insts:
        q = _silu(proj_ref[rows[s], _cols(OFF_HG_Q, h, HG_DK)])
        k = proj_ref[rows[s], _cols(OFF_HG_F, h, HG_DK)]
        gc = gc_ref[rows[s], _cols(0, h, HG_DK)]
        g_mid = gc[mid:mid + 1, :]
        g_end = gc[CHUNK - 1:CHUNK, :]
        q_rel[s, h] = (q * jnp.exp2(gc - g_mid)).astype(BF16)
        k_rel[s, h] = (k * jnp.exp2(g_mid - gc)).astype(BF16)
        q_abs[s, h] = (q * jnp.exp2(gc)).astype(BF16)
        k_end[s, h] = (k * jnp.exp2(g_end - gc)).astype(BF16)
        v_b[s, h] = proj_ref[rows[s], _cols(OFF_HG_I, h, HG_DV)].astype(BF16)
        dec[s, h] = jnp.exp2(g_end)
    proj.emit(PROJ_TILES_PER_HGRN_STAGE)
    k_pad = jnp.zeros((LANES - CHUNK, HG_DK), BF16)
    a = {k: lax.dot_general(q_rel[k], jnp.concatenate([k_rel[k], k_pad], axis=0), _NT, preferred_element_type=F32)
         for k in insts}
    st = {k: get_st(*k) for k in insts}
    v_t = {k: jnp.concatenate([v_b[k], jnp.zeros((LANES - CHUNK, HG_DV), BF16)], axis=0).T for k in insts}
    kv = {k: jnp.dot(v_t[k][:, :CHUNK], k_end[k], preferred_element_type=F32) for k in insts}
    for k in insts:
        set_st(*k, dec[k] * st[k] + kv[k])
    a_m = {k: jnp.where(causal, a[k], 0.0).astype(BF16) for k in insts}
    o = {k: lax.dot_general(jnp.concatenate([q_abs[k], a_m[k]], axis=1),
                            jnp.concatenate([st[k].astype(BF16), v_t[k]], axis=1), _NT, preferred_element_type=F32)
         for k in insts}
    ms = {k: jnp.mean(o[k] * o[k], axis=-1, keepdims=True) for k in insts}
    proj.emit(PROJ_TILES_PER_HGRN_STAGE)
    for (s, h) in insts:
        gate = _silu(proj_ref[rows[s], _cols(OFF_HG_GATE, h, HG_DV)].astype(BF16))
        on = o[s, h] * lax.rsqrt(ms[s, h] + EPS) * hg_g_ref[:, _cols(0, h, HG_DV)]
        mixed_ref[rows[s], _cols(0, h, HG_DV)] = on.astype(BF16) * gate


def _retention(n_streams, ts, proj_ref, mixed_ref, cos_t, sin_t, mdec_ref, qdec_ref, kdec_ref, rdec_ref,
               rt_g_ref, get_r, set_r):
    lane = lax.broadcasted_iota(jnp.int32, (ts, LANES), 1)
    first_half = (lane % RT_DK) < (RT_DK // 2)
    low_head = lane < RT_DK
    low_rows = lax.broadcasted_iota(jnp.int32, (2 * RT_DK, RT_DV), 0) < RT_DK

    def rope(xs, cs, sn):
        swapped = jnp.where(first_half, pltpu.roll(xs, LANES - RT_DK // 2, 1), pltpu.roll(xs, RT_DK // 2, 1))
        return xs * cs + swapped * sn

    pairs = [(s, p) for s in range(n_streams) for p in range(RT_H // 2)]
    insts = [(s, p, i) for (s, p) in pairs for i in range(2)]
    rows = {s: slice(s * ts, (s + 1) * ts) for s in range(n_streams)}
    qp, kp_b, qd, kd_b, r_pair, v_pair = {}, {}, {}, {}, {}, {}
    for (s, p) in pairs:
        cs, sn = cos_t[rows[s], :], sin_t[rows[s], :]
        q = rope(proj_ref[rows[s], _cols(OFF_RT_Q, p, LANES)], cs, sn)
        k = rope(proj_ref[rows[s], _cols(OFF_RT_K, p, LANES)], cs, sn)
        qp[s, p] = q
        qd[s, p] = q * qdec_ref[p]
        kp_b[s, p] = k.astype(BF16)
        kd_b[s, p] = (k * kdec_ref[p]).astype(BF16)
        r_pair[s, p] = get_r(s, p)
        v_pair[s, p] = proj_ref[rows[s], _cols(OFF_RT_V, p, 2 * RT_DV)].astype(BF16)
    yield
    sel = {0: low_head, 1: jnp.logical_not(low_head)}
    q_h = {(s, p, i): jnp.where(sel[i], qp[s, p], 0.0).astype(BF16) for (s, p, i) in insts}
    qd_h = {(s, p, i): jnp.where(sel[i], qd[s, p], 0.0).astype(BF16) for (s, p, i) in insts}
    sc = {(s, p, i): lax.dot_general(q_h[s, p, i], kp_b[s, p], _NT, preferred_element_type=F32)
          for (s, p, i) in insts}
    inter = {(s, p, i): jnp.dot(qd_h[s, p, i], r_pair[s, p].astype(BF16), preferred_element_type=F32)
             for (s, p, i) in insts}
    kv = {(s, p): lax.dot_general(kd_b[s, p], v_pair[s, p], _TN, preferred_element_type=F32) for (s, p) in pairs}
    yield
    for (s, p) in pairs:
        dr = jnp.where(low_rows, kv[s, p][:, :RT_DV], kv[s, p][:, RT_DV:])
        set_r(s, p, rdec_ref[p] * r_pair[s, p] + dr)
    sc_b = {(s, p, i): (sc[s, p, i] * mdec_ref[2 * p + i]).astype(BF16) for (s, p, i) in insts}
    yield
    o = {(s, p, i): jnp.dot(sc_b[s, p, i], v_pair[s, p][:, i * RT_DV:(i + 1) * RT_DV],
                            preferred_element_type=F32) + inter[s, p, i] for (s, p, i) in insts}
    yield
    mu = {k: jnp.mean(o[k], axis=-1, keepdims=True) for k in insts}
    cen = {k: o[k] - mu[k] for k in insts}
    var = {k: jnp.mean(cen[k] * cen[k], axis=-1, keepdims=True) for k in insts}
    yield
    for (s, p, i) in insts:
        h = 2 * p + i
        gate = _silu(proj_ref[rows[s], _cols(OFF_RT_GATE, h, RT_DV)].astype(BF16))
        on = cen[s, p, i] * lax.rsqrt(var[s, p, i] + EPS) * rt_g_ref[:, _cols(0, h, RT_DV)]
        mixed_ref[rows[s], _cols(HG_W, h, RT_DV)] = on.astype(BF16) * gate


def _cross_attention(n_streams, ts, proj_ref, mixed_ref, get_mem):
    xa_scale = XA_DH ** -0.5
    insts = [(s, h) for s in range(n_streams) for h in range(XA_H)]
    rows = {s: slice(s * ts, (s + 1) * ts) for s in range(n_streams)}
    mem = {k: get_mem(*k) for k in insts}
    sc = {(s, h): lax.dot_general(proj_ref[rows[s], _cols(OFF_XA_Q, h, XA_DH)].astype(BF16), mem[s, h][0], _NT,
                                  preferred_element_type=F32) * xa_scale for (s, h) in insts}
    yield
    mx = {k: jnp.max(sc[k], axis=-1, keepdims=True) for k in insts}
    yield
    pr = {k: jnp.exp(sc[k] - mx[k]) for k in insts}
    den = {k: jnp.sum(pr[k], axis=-1, keepdims=True) for k in insts}
    yield
    o = {k: jnp.dot(pr[k].astype(BF16), mem[k][1], preferred_element_type=F32) for k in insts}
    yield
    for (s, h) in insts:
        gate = _silu(proj_ref[rows[s], _cols(OFF_XA_GATE, h, XA_DH)].astype(BF16))
        mixed_ref[rows[s], _cols(HG_W + RT_W, h, XA_DH)] = (o[s, h] / den[s, h]).astype(BF16) * gate


def _chunk_cumsum(g):
    tokens, c = g.shape
    x = g.reshape(tokens // SUBLANES, SUBLANES, c)
    sub = lax.broadcasted_iota(jnp.int32, x.shape, 1)
    shift = 1
    while shift < SUBLANES:
        x = x + jnp.where(sub >= shift, pltpu.roll(x, shift, 1), 0.0)
        shift *= 2
    tiles = CHUNK // SUBLANES
    x = x.reshape(tokens // CHUNK, tiles, SUBLANES, c)
    out = [x[:, 0]]
    for j in range(1, tiles):
        total = out[-1][:, SUBLANES - 1:SUBLANES, :]
        out.append(x[:, j] + total)
    return jnp.stack(out, axis=1).reshape(tokens, c)


class _Projection:
    SEGMENTS = ((OFF_HG_F, OFF_HG_I), (OFF_HG_Q, OFF_HG_F), (OFF_HG_I, OFF_RT_Q), (OFF_RT_Q, D_IN))
    TILES_GATES = HG_W // PROJ_N_TILE
    TILES_HGRN = 4 * HG_W // PROJ_N_TILE
    TILES_ALL = D_IN // PROJ_N_TILE

    def __init__(self, x, norm_g_ref, w_in_ref, xn_ref, proj_ref):
        xn_ref[...] = _rms(x, norm_g_ref[...]).astype(BF16)
        self._refs = (xn_ref, w_in_ref, proj_ref)
        self._pending = [n0 for lo, hi in self.SEGMENTS for n0 in range(lo, hi, PROJ_N_TILE)]
        assert len(self._pending) == self.TILES_ALL
        self._done = 0

    def emit(self, n=1):
        xn_ref, w_in_ref, proj_ref = self._refs
        for _ in range(min(n, len(self._pending))):
            cols = slice(self._pending[0], self._pending.pop(0) + PROJ_N_TILE)
            proj_ref[:, cols] = jnp.dot(xn_ref[...], w_in_ref[:, cols], preferred_element_type=F32)
            self._done += 1

    def require(self, n_tiles):
        self.emit(n_tiles - self._done)


def _mix(n_streams, ts, x_ref, y_ref, cos_t, sin_t, mdec_ref, qdec_ref, kdec_ref, rdec_ref,
         get_mem, lb_ref, hg_g_ref, rt_g_ref, w_out_ref, fin_g_ref,
         proj_ref, gc_ref, mixed_ref, get_st, set_st, get_r, set_r, proj):
    proj.require(proj.TILES_GATES)
    lg = lb_ref[...]
    e = jnp.exp(lg - jnp.max(lg, axis=0, keepdims=True))
    lb = e[0:1, :] / jnp.sum(e, axis=0, keepdims=True)
    for c0 in range(0, HG_W, 2 * LANES):
        cols = slice(OFF_HG_F + c0, OFF_HG_F + c0 + 2 * LANES)
        lbc = lb[:, c0:c0 + 2 * LANES]
        half_span = 0.5 * (1.0 - lbc)
        bt = half_span * jnp.tanh(0.5 * proj_ref[:, cols])
        g = jnp.log2(lbc + (half_span + bt))
        proj_ref[:, cols] = half_span - bt
        gc_ref[:, c0:c0 + 2 * LANES] = _chunk_cumsum(g)
        proj.emit(PROJ_TILES_PER_GATE_GROUP)

    proj.require(proj.TILES_HGRN)
    causal = (lax.broadcasted_iota(jnp.int32, (CHUNK, LANES), 1)
              <= lax.broadcasted_iota(jnp.int32, (CHUNK, LANES), 0))
    for c in range(ts // CHUNK):
        chunks = [(s, slice(s * ts + c * CHUNK, s * ts + (c + 1) * CHUNK)) for s in range(n_streams)]
        for i in range(0, n_streams, HGRN_STREAMS_PER_STAGE):
            _hgrn_chunks(proj_ref, gc_ref, mixed_ref, hg_g_ref, chunks[i:i + HGRN_STREAMS_PER_STAGE], causal,
                         get_st, set_st, proj)

    proj.require(proj.TILES_ALL)
    _run_interleaved(
        _retention(n_streams, ts, proj_ref, mixed_ref, cos_t, sin_t, mdec_ref, qdec_ref, kdec_ref, rdec_ref,
                   rt_g_ref, get_r, set_r),
        _cross_attention(n_streams, ts, proj_ref, mixed_ref, get_mem),
        _out_projection(mixed_ref, w_out_ref, y_ref, 0, HG_W, x_ref))
    sumsq = []
    _run_interleaved(_out_projection(mixed_ref, w_out_ref, y_ref, HG_W, D_MIX, y_ref, sumsq))

    inv = lax.rsqrt(sum(sumsq) * (1.0 / D_MODEL) + EPS)
    y_ref[...] = y_ref[...] * inv * fin_g_ref[...]


def _out_projection(mixed_ref, w_out_ref, acc_ref, k_lo, k_hi, base_ref, sumsq=None):
    for n0 in range(0, D_MODEL, OUT_N_TILE):
        cols = slice(n0, n0 + OUT_N_TILE)
        out = base_ref[:, cols] + jnp.dot(mixed_ref[:, k_lo:k_hi], w_out_ref[k_lo:k_hi, cols],
                                          preferred_element_type=F32)
        acc_ref[:, cols] = out
        if sumsq is not None:
            sumsq.append(jnp.sum(out * out, axis=-1, keepdims=True))
        yield


def _run_interleaved(*stage_generators):
    pending = list(stage_generators)
    while pending:
        for g in list(pending):
            try:
                next(g)
            except StopIteration:
                pending.remove(g)


def _memory_kv_kernel(mem_ref, g_ref, wk_ref, wv_ref, k_ref, v_ref, kb_ref, vb_ref):
    m = _rms(mem_ref[0], g_ref[...]).astype(BF16)
    k = jnp.dot(m, wk_ref[...].astype(BF16), preferred_element_type=F32)
    v = jnp.dot(m, wv_ref[...].astype(BF16), preferred_element_type=F32)
    for h in range(XA_H):
        k_ref[0, pl.ds(h, N_MEM, stride=XA_H), :] = k[:, _cols(0, h, XA_DH)]
        v_ref[0, pl.ds(h, N_MEM, stride=XA_H), :] = v[:, _cols(0, h, XA_DH)]
    kb_ref[0] = k.astype(BF16)
    vb_ref[0] = v.astype(BF16)


def _prompt_kernel(steps_per_stream, w_in_slab_ref, w_out_slab_ref,
                   x_ref, cos_ref, sin_ref, mdec_ref, qdec_ref, kdec_ref, rdec_ref, mk_ref, mv_ref,
                   norm_g_ref, lb_ref, hg_g_ref, rt_g_ref, fin_g_ref,
                   y_ref, s_out_ref, r_out_ref, w_in_b_slab_ref, w_out_b_slab_ref,
                   proj_ref, xn_ref, gc_ref, mixed_ref, st_ref, w_in_ref, w_out_ref):
    t = pl.program_id(0)
    j = (t - WEIGHT_PREP_STEPS) % steps_per_stream
    running = t >= WEIGHT_PREP_STEPS

    @pl.when(jnp.logical_not(running))
    def _():
        for slab_ref, out_ref, full_ref in ((w_in_slab_ref, w_in_b_slab_ref, w_in_ref),
                                            (w_out_slab_ref, w_out_b_slab_ref, w_out_ref)):
            n = slab_ref.shape[0]
            slab = slab_ref[...].astype(BF16)
            full_ref[pl.ds(pl.multiple_of(t * n, n), n), :] = slab
            out_ref[...] = slab

    @pl.when(running & (j == 0))
    def _():
        st_ref[...] = jnp.zeros_like(st_ref)
        r_out_ref[...] = jnp.zeros_like(r_out_ref)

    def set_st(s, h, val):
        st_ref[h] = val

    def set_r(s, p, val):
        r_out_ref[0, p] = val

    def get_mem(s, h):
        return mk_ref[0, :, _cols(0, h, XA_DH)], mv_ref[0, :, _cols(0, h, XA_DH)]

    def token_block(i, carry):
        rows = pl.ds(pl.multiple_of(i * TOKEN_BLOCK, TOKEN_BLOCK), TOKEN_BLOCK)
        x_blk = x_ref.at[0, rows]
        proj = _Projection(x_blk[...], norm_g_ref, w_in_ref, xn_ref, proj_ref)
        _mix(
            1, TOKEN_BLOCK, x_blk, y_ref.at[0, rows], cos_ref[rows, :], sin_ref[rows, :],
            mdec_ref, qdec_ref, kdec_ref, rdec_ref,
            get_mem, lb_ref, hg_g_ref, rt_g_ref, w_out_ref, fin_g_ref,
            proj_ref, gc_ref, mixed_ref,
            lambda s, h: st_ref[h], set_st, lambda s, p: r_out_ref[0, p], set_r, proj)
        return carry

    @pl.when(running)
    def _():
        lax.fori_loop(0, x_ref.shape[1] // TOKEN_BLOCK, token_block, 0)

    @pl.when(running & (j == steps_per_stream - 1))
    def _():
        for h in range(HG_H):
            s_out_ref[0, h] = st_ref[h].T


def _sample_kernel(x_ref, cos_ref, sin_ref, mdec_ref, qdec_ref, kdec_ref, rdec_ref, mk_ref, mv_ref,
                   s_in_ref, r_in_ref,
                   norm_g_ref, w_in_ref, lb_ref, hg_g_ref, rt_g_ref, w_out_ref, fin_g_ref,
                   y_ref, s_out_ref, r_out_ref, proj_ref, xn_ref, gc_ref, mixed_ref):
    n_streams = TOKEN_BLOCK // CHUNK

    def set_st(s, h, val):
        s_out_ref[s, h] = val.T

    def set_r(s, p, val):
        r_out_ref[s, p] = val

    def get_mem(s, h):
        head_rows = pl.ds(h, N_MEM, stride=XA_H)
        return mk_ref[s, head_rows, :].astype(BF16), mv_ref[s, head_rows, :].astype(BF16)

    cos_t = jnp.concatenate([cos_ref[...]] * n_streams, axis=0)
    sin_t = jnp.concatenate([sin_ref[...]] * n_streams, axis=0)
    proj = _Projection(x_ref[...], norm_g_ref, w_in_ref, xn_ref, proj_ref)
    _mix(
        n_streams, CHUNK, x_ref, y_ref, cos_t, sin_t, mdec_ref, qdec_ref, kdec_ref, rdec_ref,
        get_mem, lb_ref, hg_g_ref, rt_g_ref, w_out_ref, fin_g_ref,
        proj_ref, gc_ref, mixed_ref,
        lambda s, h: s_in_ref[s, h].T, set_st, lambda s, p: r_in_ref[s, p], set_r, proj)


def _const_spec(shape):
    zeros = (0,) * len(shape)
    return pl.BlockSpec(shape, lambda *_: zeros, pipeline_mode=pl.Buffered(1))


def _layer_scratch():
    return [pltpu.VMEM((TOKEN_BLOCK, D_IN), F32),
            pltpu.VMEM((TOKEN_BLOCK, D_MODEL), BF16),
            pltpu.VMEM((TOKEN_BLOCK, HG_W), F32),
            pltpu.VMEM((TOKEN_BLOCK, D_MIX), BF16)]


def _memory_kv(mem, g, wk, wv):
    b = mem.shape[0]
    out_f = jax.ShapeDtypeStruct((b, N_MEM * XA_H, XA_DH), F32)
    out_b = jax.ShapeDtypeStruct((b, N_MEM, XA_W), BF16)
    blk_f = pl.BlockSpec((1, N_MEM * XA_H, XA_DH), lambda i: (i, 0, 0))
    blk_b = pl.BlockSpec((1, N_MEM, XA_W), lambda i: (i, 0, 0))
    return pl.pallas_call(
        _memory_kv_kernel,
        grid=(b,),
        in_specs=[pl.BlockSpec((1, N_MEM, D_MODEL), lambda i: (i, 0, 0)),
                  pl.BlockSpec((1, D_MODEL), lambda i: (0, 0)),
                  pl.BlockSpec((D_MODEL, XA_W), lambda i: (0, 0)),
                  pl.BlockSpec((D_MODEL, XA_W), lambda i: (0, 0))],
        out_specs=[blk_f, blk_f, blk_b, blk_b],
        out_shape=[out_f, out_f, out_b, out_b],
        name="memory_kv",
    )(mem, g.reshape(1, D_MODEL), wk, wv)


def _prompt_layer(x, mk_b, mv_b, w_in, w_out, params):
    b, t, _ = x.shape
    tb = PROMPT_BLOCKS_PER_STEP * TOKEN_BLOCK
    spp = t // tb
    prep = WEIGHT_PREP_STEPS
    cos_t, sin_t = _rope_tables(np.arange(t))
    mdec, qdec, kdec, rdec = _retention_tables(TOKEN_BLOCK)
    consts = (mdec, qdec, kdec, rdec)
    in_rows, out_rows = D_MODEL // prep, D_MIX // prep

    def slab(step):
        return jnp.minimum(step, prep - 1)

    def run(step):
        return jnp.maximum(step - prep, 0)

    in_specs = ([pl.BlockSpec((in_rows, D_IN), lambda s: (slab(s), 0)),
                 pl.BlockSpec((out_rows, D_MODEL), lambda s: (slab(s), 0)),
                 pl.BlockSpec((1, tb, D_MODEL), lambda s: (run(s) // spp, run(s) % spp, 0)),
                 pl.BlockSpec((tb, LANES), lambda s: (run(s) % spp, 0)),
                 pl.BlockSpec((tb, LANES), lambda s: (run(s) % spp, 0))]
                + [_const_spec(c.shape) for c in consts]
                + [pl.BlockSpec((1, N_MEM, XA_W), lambda s: (run(s) // spp, 0, 0)),
                   pl.BlockSpec((1, N_MEM, XA_W), lambda s: (run(s) // spp, 0, 0))]
                + [_const_spec(p.shape) for p in params])
    out_shape = [jax.ShapeDtypeStruct((b, t, D_MODEL), F32),
                 jax.ShapeDtypeStruct((b, HG_H, HG_DK, HG_DV), F32),
                 jax.ShapeDtypeStruct((b, RT_H // 2, 2 * RT_DK, RT_DV), F32),
                 jax.ShapeDtypeStruct((D_MODEL, D_IN), BF16),
                 jax.ShapeDtypeStruct((D_MIX, D_MODEL), BF16)]
    out_specs = [pl.BlockSpec((1, tb, D_MODEL), lambda s: (run(s) // spp, run(s) % spp, 0)),
                 pl.BlockSpec((1, HG_H, HG_DK, HG_DV), lambda s: (run(s) // spp, 0, 0, 0)),
                 pl.BlockSpec((1, RT_H // 2, 2 * RT_DK, RT_DV), lambda s: (run(s) // spp, 0, 0, 0)),
                 pl.BlockSpec((in_rows, D_IN), lambda s: (slab(s), 0)),
                 pl.BlockSpec((out_rows, D_MODEL), lambda s: (slab(s), 0))]
    return pl.pallas_call(
        functools.partial(_prompt_kernel, spp),
        grid=(prep + b * spp,),
        in_specs=in_specs,
        out_specs=out_specs,
        out_shape=out_shape,
        scratch_shapes=_layer_scratch() + [pltpu.VMEM((HG_H, HG_DV, HG_DK), F32),
                                           pltpu.VMEM((D_MODEL, D_IN), BF16),
                                           pltpu.VMEM((D_MIX, D_MODEL), BF16)],
        compiler_params=pltpu.CompilerParams(dimension_semantics=("arbitrary",),
                                             vmem_limit_bytes=VMEM_LIMIT_BYTES),
        name="prompt_layer",
    )(w_in, w_out, x, cos_t, sin_t, *consts, mk_b, mv_b, *params)


def _sample_layer(x, s_hg, s_rt, mem_k, mem_v, weights):
    b, t, _ = x.shape
    assert t == CHUNK
    tb = TOKEN_BLOCK
    nsb = tb // t
    cos_t, sin_t = _rope_tables(PAST_LEN + np.arange(t))
    mdec, qdec, kdec, rdec = _retention_tables(t)
    consts = (mdec, qdec, kdec, rdec)
    s_rt2 = s_rt.reshape(b, RT_H // 2, 2 * RT_DK, RT_DV)
    mem_spec = pl.BlockSpec((nsb, N_MEM * XA_H, XA_DH), lambda i: (i, 0, 0))
    in_specs = ([pl.BlockSpec((tb, D_MODEL), lambda i: (i, 0)),
                 _const_spec(cos_t.shape), _const_spec(sin_t.shape)]
                + [_const_spec(c.shape) for c in consts]
                + [mem_spec, mem_spec,
                   pl.BlockSpec((nsb, HG_H, HG_DK, HG_DV), lambda i: (i, 0, 0, 0)),
                   pl.BlockSpec((nsb, RT_H // 2, 2 * RT_DK, RT_DV), lambda i: (i, 0, 0, 0))]
                + [_const_spec(w.shape) for w in weights])
    out_shape = [jax.ShapeDtypeStruct((b * t, D_MODEL), F32),
                 jax.ShapeDtypeStruct((b, HG_H, HG_DK, HG_DV), F32),
                 jax.ShapeDtypeStruct((b, RT_H // 2, 2 * RT_DK, RT_DV), F32)]
    out_specs = [pl.BlockSpec((tb, D_MODEL), lambda i: (i, 0)),
                 pl.BlockSpec((nsb, HG_H, HG_DK, HG_DV), lambda i: (i, 0, 0, 0)),
                 pl.BlockSpec((nsb, RT_H // 2, 2 * RT_DK, RT_DV), lambda i: (i, 0, 0, 0))]
    y, s_new, r_new = pl.pallas_call(
        _sample_kernel,
        grid=(b // nsb,),
        in_specs=in_specs,
        out_specs=out_specs,
        out_shape=out_shape,
        scratch_shapes=_layer_scratch(),
        compiler_params=pltpu.CompilerParams(dimension_semantics=("arbitrary",),
                                             vmem_limit_bytes=VMEM_LIMIT_BYTES),
        name="sample_layer",
    )(x.reshape(b * t, D_MODEL), cos_t, sin_t, *consts,
      mem_k.reshape(b, N_MEM * XA_H, XA_DH), mem_v.reshape(b, N_MEM * XA_H, XA_DH), s_hg, s_rt2, *weights)
    return y.reshape(b, t, D_MODEL), s_new, r_new


def kernel(x_prompt, x_sample, mem_prompt, state_hgrn, state_ret, cache_mem_k, cache_mem_v, norm_g, w_in,
           lb_logits, hg_norm_g, rt_norm_g, mem_norm_g, w_mem_k, w_mem_v, w_out, final_norm_g):
    depth = w_in.shape[0]
    assert depth == 1, "single-layer step"
    bp = x_prompt.shape[0]
    bs = x_sample.shape[0]
    g_in, g_hg, g_rt = norm_g[0].reshape(1, D_MODEL), hg_norm_g[0].reshape(1, HG_W), rt_norm_g[0].reshape(1, RT_W)
    g_fin = final_norm_g.reshape(1, D_MODEL)
    mk, mv, mk_b, mv_b = _memory_kv(mem_prompt, mem_norm_g[0], w_mem_k[0], w_mem_v[0])
    y_p, s_p, r_p, w_in_b, w_out_b = _prompt_layer(x_prompt, mk_b, mv_b, w_in[0], w_out[0],
                                                   (g_in, lb_logits, g_hg, g_rt, g_fin))
    y_s, s_s, r_s = _sample_layer(x_sample, state_hgrn[0], state_ret[0], cache_mem_k[0], cache_mem_v[0],
                                  (g_in, w_in_b, lb_logits, g_hg, g_rt, w_out_b, g_fin))
    return (y_p, y_s,
            s_p.reshape(1, bp, HG_H, HG_DK, HG_DV),
            r_p.reshape(1, bp, RT_H, RT_DK, RT_DV),
            mk.reshape(1, bp, N_MEM, XA_H, XA_DH),
            mv.reshape(1, bp, N_MEM, XA_H, XA_DH),
            s_s.reshape(1, bs, HG_H, HG_DK, HG_DV),
            r_s.reshape(1, bs, RT_H, RT_DK, RT_DV))
```

```python
import functools

import numpy as np
import jax
import jax.numpy as jnp
from jax import lax
from jax.experimental import pallas as pl
from jax.experimental.pallas import tpu as pltpu

F32 = jnp.float32
BF16 = jnp.bfloat16

D_MODEL = 1024
PAST_LEN = 4096
CHUNK = 64
N_MEM = 256
EPS = 1e-6
ROPE_BASE = 10000.0
HG_H, HG_DK, HG_DV = 8, 128, 128
HG_W = HG_H * HG_DV
RT_H, RT_DK, RT_DV = 4, 64, 128
RT_QK = RT_H * RT_DK
RT_W = RT_H * RT_DV
XA_H, XA_DH = 4, 128
XA_W = XA_H * XA_DH
D_MIX = HG_W + RT_W + XA_W
OFF_HG_Q = 0
OFF_HG_F = OFF_HG_Q + HG_W
OFF_HG_I = OFF_HG_F + HG_W
OFF_HG_GATE = OFF_HG_I + HG_W
OFF_RT_Q = OFF_HG_GATE + HG_W
OFF_RT_K = OFF_RT_Q + RT_QK
OFF_RT_V = OFF_RT_K + RT_QK
OFF_RT_GATE = OFF_RT_V + RT_W
OFF_XA_Q = OFF_RT_GATE + RT_W
OFF_XA_GATE = OFF_XA_Q + XA_W
D_IN = OFF_XA_GATE + XA_W

LANES = 128
SUBLANES = 8
TOKEN_BLOCK = 256
PROMPT_BLOCKS_PER_STEP = 4
WEIGHT_PREP_STEPS = 16
PROJ_N_TILE = 512
PROJ_TILES_PER_GATE_GROUP = 2
PROJ_TILES_PER_HGRN_STAGE = 1
HGRN_STREAMS_PER_STAGE = 4
OUT_N_TILE = 512
VMEM_LIMIT_BYTES = 56 * 1024 * 1024
_NT = (((1,), (1,)), ((), ()))
_TN = (((0,), (0,)), ((), ()))


def _silu(x):
    h = 0.5 * x
    return h + h * jnp.tanh(h)


def _rms(x, g):
    return x * lax.rsqrt(jnp.mean(x * x, axis=-1, keepdims=True) + EPS) * g


def _cols(off, i, width):
    return slice(off + i * width, off + (i + 1) * width)


def _rope_tables(pos):
    half = RT_DK // 2
    inv_freq = ROPE_BASE ** (-np.arange(half, dtype=np.float64) / half)
    ang = np.asarray(pos, np.float64)[:, None] * inv_freq[None, :]
    cos, sin = np.cos(ang), np.sin(ang)
    cos_t = np.concatenate([cos, cos, cos, cos], axis=-1)
    sin_t = np.concatenate([-sin, sin, -sin, sin], axis=-1)
    return jnp.asarray(cos_t, F32), jnp.asarray(sin_t, F32)


def _retention_tables(ts):
    h = np.arange(RT_H, dtype=np.float64)
    log_gamma = np.log1p(-np.exp2(-5.0 - h))
    n = np.arange(ts, dtype=np.float64)
    scale = RT_DK ** -0.5
    dist = np.abs(n[:, None] - n[None, :])
    visible = (n[None, :] // CHUNK) <= (n[:, None] // CHUNK)
    mdec = scale * np.exp(log_gamma[:, None, None] * dist[None]) * visible[None]
    qd = np.exp(log_gamma[:, None] * (n[None, :] + 1.0))
    kd = scale * np.exp(log_gamma[:, None] * (ts - 1.0 - n[None, :]))
    rd = np.exp(log_gamma * ts)

    def pair_lanes(a):
        a = np.repeat(a[:, :, None], RT_DK, axis=2)
        return np.concatenate([a[0::2], a[1::2]], axis=2)

    rdec = np.repeat(np.repeat(rd[:, None], RT_DK, axis=1).reshape(RT_H // 2, 2 * RT_DK, 1), LANES, axis=2)
    return (jnp.asarray(mdec, F32), jnp.asarray(pair_lanes(qd), F32), jnp.asarray(pair_lanes(kd), F32),
            jnp.asarray(rdec, F32))


def _hgrn_chunks(proj_ref, gc_ref, mixed_ref, hg_g_ref, chunks, score_mask, get_st, set_st, proj):
    half = CHUNK // 2
    zero_half = jnp.zeros((half, HG_DK), BF16)
    insts = [(s, h) for s, _ in chunks for h in range(HG_H)]
    rows = dict(chunks)
    q_rel, k_rel, q_abs, k_end, v_b, dec = {}, {}, {}, {}, {}, {}
    for (s, h) in insts:
        q = _silu(proj_ref[rows[s], _cols(OFF_HG_Q, h, HG_DK)])
        k = proj_ref[rows[s], _cols(OFF_HG_F, h, HG_DK)]
        gc = gc_ref[rows[s], _cols(0, h, HG_DK)]
        g_end = gc[CHUNK - 1:CHUNK, :]
        (q1, q2), (k1, k2), (g1, g2) = ((a[:half], a[half:]) for a in (q, k, gc))
        g_m1, g_m2 = gc[half // 2 - 1:half // 2, :], gc[half + half // 2 - 1:half + half // 2, :]
        g_b = gc[half - 1:half, :]
        qd1, kd1 = (q1 * jnp.exp2(g1 - g_m1)).astype(BF16), (k1 * jnp.exp2(g_m1 - g1)).astype(BF16)
        qd2, kd2 = (q2 * jnp.exp2(g2 - g_m2)).astype(BF16), (k2 * jnp.exp2(g_m2 - g2)).astype(BF16)
        qo, ko = (q2 * jnp.exp2(g2 - g_b)).astype(BF16), (k1 * jnp.exp2(g_b - g1)).astype(BF16)
        q_rel[s, h] = jnp.concatenate([jnp.concatenate([qd1, qd2], axis=0),
                                       jnp.concatenate([zero_half, qo], axis=0)], axis=1)
        k_rel[s, h] = jnp.concatenate([jnp.concatenate([kd1, kd2, zero_half, zero_half], axis=0),
                                       jnp.concatenate([zero_half, zero_half, ko, zero_half], axis=0)],
                                      axis=1)
        q_abs[s, h] = (q * jnp.exp2(gc)).astype(BF16)
        k_end[s, h] = (k * jnp.exp2(g_end - gc)).astype(BF16)
        v_b[s, h] = proj_ref[rows[s], _cols(OFF_HG_I, h, HG_DV)].astype(BF16)
        dec[s, h] = jnp.exp2(g_end)
    proj.emit(PROJ_TILES_PER_HGRN_STAGE)
    a = {k: lax.dot_general(q_rel[k], k_rel[k], _NT, preferred_element_type=F32) for k in insts}
    st = {k: get_st(*k) for k in insts}
    v_t = {k: jnp.concatenate([v_b[k], v_b[k][:half], zero_half], axis=0).T for k in insts}
    kv = {k: jnp.dot(v_t[k][:, :CHUNK], k_end[k], preferred_element_type=F32) for k in insts}
    for k in insts:
        set_st(*k, dec[k] * st[k] + kv[k])
    a_m = {k: jnp.where(score_mask, a[k], 0.0).astype(BF16) for k in insts}
    o = {k: lax.dot_general(jnp.concatenate([q_abs[k], a_m[k]], axis=1),
                            jnp.concatenate([st[k].astype(BF16), v_t[k]], axis=1), _NT, preferred_element_type=F32)
         for k in insts}
    ms = {k: jnp.mean(o[k] * o[k], axis=-1, keepdims=True) for k in insts}
    proj.emit(PROJ_TILES_PER_HGRN_STAGE)
    for (s, h) in insts:
        gate = _silu(proj_ref[rows[s], _cols(OFF_HG_GATE, h, HG_DV)].astype(BF16))
        on = o[s, h] * lax.rsqrt(ms[s, h] + EPS) * hg_g_ref[:, _cols(0, h, HG_DV)]
        mixed_ref[rows[s], _cols(0, h, HG_DV)] = on.astype(BF16) * gate


def _retention(n_streams, ts, proj_ref, mixed_ref, cos_t, sin_t, mdec_ref, qdec_ref, kdec_ref, rdec_ref,
               rt_g_ref, get_r, set_r):
    lane = lax.broadcasted_iota(jnp.int32, (ts, LANES), 1)
    first_half = (lane % RT_DK) < (RT_DK // 2)
    low_head = lane < RT_DK
    low_rows = lax.broadcasted_iota(jnp.int32, (2 * RT_DK, RT_DV), 0) < RT_DK

    def rope(xs, cs, sn):
        swapped = jnp.where(first_half, pltpu.roll(xs, LANES - RT_DK // 2, 1), pltpu.roll(xs, RT_DK // 2, 1))
        return xs * cs + swapped * sn

    pairs = [(s, p) for s in range(n_streams) for p in range(RT_H // 2)]
    insts = [(s, p, i) for (s, p) in pairs for i in range(2)]
    rows = {s: slice(s * ts, (s + 1) * ts) for s in range(n_streams)}
    qp, kp_b, qd, kd_b, r_pair, v_pair = {}, {}, {}, {}, {}, {}
    for (s, p) in pairs:
        cs, sn = cos_t[rows[s], :], sin_t[rows[s], :]
        q = rope(proj_ref[rows[s], _cols(OFF_RT_Q, p, LANES)], cs, sn)
        k = rope(proj_ref[rows[s], _cols(OFF_RT_K, p, LANES)], cs, sn)
        qp[s, p] = q
        qd[s, p] = q * qdec_ref[p]
        kp_b[s, p] = k.astype(BF16)
        kd_b[s, p] = (k * kdec_ref[p]).astype(BF16)
        r_pair[s, p] = get_r(s, p)
        v_pair[s, p] = proj_ref[rows[s], _cols(OFF_RT_V, p, 2 * RT_DV)].astype(BF16)
    yield
    sel = {0: low_head, 1: jnp.logical_not(low_head)}
    q_h = {(s, p, i): jnp.where(sel[i], qp[s, p], 0.0).astype(BF16) for (s, p, i) in insts}
    qd_h = {(s, p, i): jnp.where(sel[i], qd[s, p], 0.0).astype(BF16) for (s, p, i) in insts}
    sc = {(s, p, i): lax.dot_general(q_h[s, p, i], kp_b[s, p], _NT, preferred_element_type=F32)
          for (s, p, i) in insts}
    inter = {(s, p, i): jnp.dot(qd_h[s, p, i], r_pair[s, p].astype(BF16), preferred_element_type=F32)
             for (s, p, i) in insts}
    kv = {(s, p): lax.dot_general(kd_b[s, p], v_pair[s, p], _TN, preferred_element_type=F32) for (s, p) in pairs}
    yield
    for (s, p) in pairs:
        dr = jnp.where(low_rows, kv[s, p][:, :RT_DV], kv[s, p][:, RT_DV:])
        set_r(s, p, rdec_ref[p] * r_pair[s, p] + dr)
    sc_b = {(s, p, i): (sc[s, p, i] * mdec_ref[2 * p + i]).astype(BF16) for (s, p, i) in insts}
    yield
    o = {(s, p, i): jnp.dot(sc_b[s, p, i], v_pair[s, p][:, i * RT_DV:(i + 1) * RT_DV],
                            preferred_element_type=F32) + inter[s, p, i] for (s, p, i) in insts}
    yield
    mu = {k: jnp.mean(o[k], axis=-1, keepdims=True) for k in insts}
    cen = {k: o[k] - mu[k] for k in insts}
    var = {k: jnp.mean(cen[k] * cen[k], axis=-1, keepdims=True) for k in insts}
    yield
    for (s, p, i) in insts:
        h = 2 * p + i
        gate = _silu(proj_ref[rows[s], _cols(OFF_RT_GATE, h, RT_DV)].astype(BF16))
        on = cen[s, p, i] * lax.rsqrt(var[s, p, i] + EPS) * rt_g_ref[:, _cols(0, h, RT_DV)]
        mixed_ref[rows[s], _cols(HG_W, h, RT_DV)] = on.astype(BF16) * gate


def _cross_attention(n_streams, ts, proj_ref, mixed_ref, get_mem):
    xa_scale = XA_DH ** -0.5
    insts = [(s, h) for s in range(n_streams) for h in range(XA_H)]
    rows = {s: slice(s * ts, (s + 1) * ts) for s in range(n_streams)}
    mem = {k: get_mem(*k) for k in insts}
    sc = {(s, h): lax.dot_general(proj_ref[rows[s], _cols(OFF_XA_Q, h, XA_DH)].astype(BF16), mem[s, h][0], _NT,
                                  preferred_element_type=F32) * xa_scale for (s, h) in insts}
    yield
    mx = {k: jnp.max(sc[k], axis=-1, keepdims=True) for k in insts}
    yield
    pr = {k: jnp.exp(sc[k] - mx[k]) for k in insts}
    den = {k: jnp.sum(pr[k], axis=-1, keepdims=True) for k in insts}
    yield
    o = {k: jnp.dot(pr[k].astype(BF16), mem[k][1], preferred_element_type=F32) for k in insts}
    yield
    for (s, h) in insts:
        gate = _silu(proj_ref[rows[s], _cols(OFF_XA_GATE, h, XA_DH)].astype(BF16))
        mixed_ref[rows[s], _cols(HG_W + RT_W, h, XA_DH)] = (o[s, h] / den[s, h]).astype(BF16) * gate


def _chunk_cumsum(g):
    tokens, c = g.shape
    x = g.reshape(tokens // SUBLANES, SUBLANES, c)
    sub = lax.broadcasted_iota(jnp.int32, x.shape, 1)
    shift = 1
    while shift < SUBLANES:
        x = x + jnp.where(sub >= shift, pltpu.roll(x, shift, 1), 0.0)
        shift *= 2
    tiles = CHUNK // SUBLANES
    x = x.reshape(tokens // CHUNK, tiles, SUBLANES, c)
    out = [x[:, 0]]
    for j in range(1, tiles):
        total = out[-1][:, SUBLANES - 1:SUBLANES, :]
        out.append(x[:, j] + total)
    return jnp.stack(out, axis=1).reshape(tokens, c)


class _Projection:
    SEGMENTS = ((OFF_HG_F, OFF_HG_I), (OFF_HG_Q, OFF_HG_F), (OFF_HG_I, OFF_RT_Q), (OFF_RT_Q, D_IN))
    TILES_GATES = HG_W // PROJ_N_TILE
    TILES_HGRN = 4 * HG_W // PROJ_N_TILE
    TILES_ALL = D_IN // PROJ_N_TILE

    def __init__(self, x, norm_g_ref, w_in_ref, xn_ref, proj_ref):
        xn_ref[...] = _rms(x, norm_g_ref[...]).astype(BF16)
        self._refs = (xn_ref, w_in_ref, proj_ref)
        self._pending = [n0 for lo, hi in self.SEGMENTS for n0 in range(lo, hi, PROJ_N_TILE)]
        assert len(self._pending) == self.TILES_ALL
        self._done = 0

    def emit(self, n=1):
        xn_ref, w_in_ref, proj_ref = self._refs
        for _ in range(min(n, len(self._pending))):
            cols = slice(self._pending[0], self._pending.pop(0) + PROJ_N_TILE)
            proj_ref[:, cols] = jnp.dot(xn_ref[...], w_in_ref[:, cols], preferred_element_type=F32)
            self._done += 1

    def require(self, n_tiles):
        self.emit(n_tiles - self._done)


def _mix(n_streams, ts, x_ref, y_ref, cos_t, sin_t, mdec_ref, qdec_ref, kdec_ref, rdec_ref,
         get_mem, lb_ref, hg_g_ref, rt_g_ref, w_out_ref, fin_g_ref,
         proj_ref, gc_ref, mixed_ref, get_st, set_st, get_r, set_r, proj):
    proj.require(proj.TILES_GATES)
    lg = lb_ref[...]
    e = jnp.exp(lg - jnp.max(lg, axis=0, keepdims=True))
    lb = e[0:1, :] / jnp.sum(e, axis=0, keepdims=True)
    for c0 in range(0, HG_W, 2 * LANES):
        cols = slice(OFF_HG_F + c0, OFF_HG_F + c0 + 2 * LANES)
        lbc = lb[:, c0:c0 + 2 * LANES]
        half_span = 0.5 * (1.0 - lbc)
        bt = half_span * jnp.tanh(0.5 * proj_ref[:, cols])
        g = jnp.log2(lbc + (half_span + bt))
        proj_ref[:, cols] = half_span - bt
        gc_ref[:, c0:c0 + 2 * LANES] = _chunk_cumsum(g)
        proj.emit(PROJ_TILES_PER_GATE_GROUP)

    proj.require(proj.TILES_HGRN)
    row = lax.broadcasted_iota(jnp.int32, (CHUNK, LANES), 0)
    col = lax.broadcasted_iota(jnp.int32, (CHUNK, LANES), 1)
    half = CHUNK // 2
    causal = (((col <= row) & (col // half == row // half))
              | ((col >= CHUNK) & (col < CHUNK + half) & (row >= half)))
    for c in range(ts // CHUNK):
        chunks = [(s, slice(s * ts + c * CHUNK, s * ts + (c + 1) * CHUNK)) for s in range(n_streams)]
        for i in range(0, n_streams, HGRN_STREAMS_PER_STAGE):
            _hgrn_chunks(proj_ref, gc_ref, mixed_ref, hg_g_ref, chunks[i:i + HGRN_STREAMS_PER_STAGE], causal,
                         get_st, set_st, proj)

    proj.require(proj.TILES_ALL)
    _run_interleaved(
        _retention(n_streams, ts, proj_ref, mixed_ref, cos_t, sin_t, mdec_ref, qdec_ref, kdec_ref, rdec_ref,
                   rt_g_ref, get_r, set_r),
        _cross_attention(n_streams, ts, proj_ref, mixed_ref, get_mem),
        _out_projection(mixed_ref, w_out_ref, y_ref, 0, HG_W, x_ref))
    sumsq = []
    _run_interleaved(_out_projection(mixed_ref, w_out_ref, y_ref, HG_W, D_MIX, y_ref, sumsq))

    inv = lax.rsqrt(sum(sumsq) * (1.0 / D_MODEL) + EPS)
    y_ref[...] = y_ref[...] * inv * fin_g_ref[...]


def _out_projection(mixed_ref, w_out_ref, acc_ref, k_lo, k_hi, base_ref, sumsq=None):
    for n0 in range(0, D_MODEL, OUT_N_TILE):
        cols = slice(n0, n0 + OUT_N_TILE)
        out = base_ref[:, cols] + jnp.dot(mixed_ref[:, k_lo:k_hi], w_out_ref[k_lo:k_hi, cols],
                                          preferred_element_type=F32)
        acc_ref[:, cols] = out
        if sumsq is not None:
            sumsq.append(jnp.sum(out * out, axis=-1, keepdims=True))
        yield


def _run_interleaved(*stage_generators):
    pending = list(stage_generators)
    while pending:
        for g in list(pending):
            try:
                next(g)
            except StopIteration:
                pending.remove(g)


def _memory_kv_kernel(mem_ref, g_ref, wk_ref, wv_ref, k_ref, v_ref, kb_ref, vb_ref):
    m = _rms(mem_ref[0], g_ref[...]).astype(BF16)
    k = jnp.dot(m, wk_ref[...].astype(BF16), preferred_element_type=F32)
    v = jnp.dot(m, wv_ref[...].astype(BF16), preferred_element_type=F32)
    for h in range(XA_H):
        k_ref[0, pl.ds(h, N_MEM, stride=XA_H), :] = k[:, _cols(0, h, XA_DH)]
        v_ref[0, pl.ds(h, N_MEM, stride=XA_H), :] = v[:, _cols(0, h, XA_DH)]
    kb_ref[0] = k.astype(BF16)
    vb_ref[0] = v.astype(BF16)


def _prompt_kernel(steps_per_stream, w_in_slab_ref, w_out_slab_ref,
                   x_ref, cos_ref, sin_ref, mdec_ref, qdec_ref, kdec_ref, rdec_ref, mk_ref, mv_ref,
                   norm_g_ref, lb_ref, hg_g_ref, rt_g_ref, fin_g_ref,
                   y_ref, s_out_ref, r_out_ref, w_in_b_slab_ref, w_out_b_slab_ref,
                   proj_ref, xn_ref, gc_ref, mixed_ref, st_ref, w_in_ref, w_out_ref):
    t = pl.program_id(0)
    j = (t - WEIGHT_PREP_STEPS) % steps_per_stream
    running = t >= WEIGHT_PREP_STEPS

    @pl.when(jnp.logical_not(running))
    def _():
        for slab_ref, out_ref, full_ref in ((w_in_slab_ref, w_in_b_slab_ref, w_in_ref),
                                            (w_out_slab_ref, w_out_b_slab_ref, w_out_ref)):
            n = slab_ref.shape[0]
            slab = slab_ref[...].astype(BF16)
            full_ref[pl.ds(pl.multiple_of(t * n, n), n), :] = slab
            out_ref[...] = slab

    @pl.when(running & (j == 0))
    def _():
        st_ref[...] = jnp.zeros_like(st_ref)
        r_out_ref[...] = jnp.zeros_like(r_out_ref)

    def set_st(s, h, val):
        st_ref[h] = val

    def set_r(s, p, val):
        r_out_ref[0, p] = val

    def get_mem(s, h):
        return mk_ref[0, :, _cols(0, h, XA_DH)], mv_ref[0, :, _cols(0, h, XA_DH)]

    def token_block(i, carry):
        rows = pl.ds(pl.multiple_of(i * TOKEN_BLOCK, TOKEN_BLOCK), TOKEN_BLOCK)
        x_blk = x_ref.at[0, rows]
        proj = _Projection(x_blk[...], norm_g_ref, w_in_ref, xn_ref, proj_ref)
        _mix(
            1, TOKEN_BLOCK, x_blk, y_ref.at[0, rows], cos_ref[rows, :], sin_ref[rows, :],
            mdec_ref, qdec_ref, kdec_ref, rdec_ref,
            get_mem, lb_ref, hg_g_ref, rt_g_ref, w_out_ref, fin_g_ref,
            proj_ref, gc_ref, mixed_ref,
            lambda s, h: st_ref[h], set_st, lambda s, p: r_out_ref[0, p], set_r, proj)
        return carry

    @pl.when(running)
    def _():
        lax.fori_loop(0, x_ref.shape[1] // TOKEN_BLOCK, token_block, 0)

    @pl.when(running & (j == steps_per_stream - 1))
    def _():
        for h in range(HG_H):
            s_out_ref[0, h] = st_ref[h].T


def _sample_kernel(x_ref, cos_ref, sin_ref, mdec_ref, qdec_ref, kdec_ref, rdec_ref, mk_ref, mv_ref,
                   s_in_ref, r_in_ref,
                   norm_g_ref, w_in_ref, lb_ref, hg_g_ref, rt_g_ref, w_out_ref, fin_g_ref,
                   y_ref, s_out_ref, r_out_ref, proj_ref, xn_ref, gc_ref, mixed_ref):
    n_streams = TOKEN_BLOCK // CHUNK

    def set_st(s, h, val):
        s_out_ref[s, h] = val.T

    def set_r(s, p, val):
        r_out_ref[s, p] = val

    def get_mem(s, h):
        head_rows = pl.ds(h, N_MEM, stride=XA_H)
        return mk_ref[s, head_rows, :].astype(BF16), mv_ref[s, head_rows, :].astype(BF16)

    cos_t = jnp.concatenate([cos_ref[...]] * n_streams, axis=0)
    sin_t = jnp.concatenate([sin_ref[...]] * n_streams, axis=0)
    proj = _Projection(x_ref[...], norm_g_ref, w_in_ref, xn_ref, proj_ref)
    _mix(
        n_streams, CHUNK, x_ref, y_ref, cos_t, sin_t, mdec_ref, qdec_ref, kdec_ref, rdec_ref,
        get_mem, lb_ref, hg_g_ref, rt_g_ref, w_out_ref, fin_g_ref,
        proj_ref, gc_ref, mixed_ref,
        lambda s, h: s_in_ref[s, h].T, set_st, lambda s, p: r_in_ref[s, p], set_r, proj)


def _const_spec(shape):
    zeros = (0,) * len(shape)
    return pl.BlockSpec(shape, lambda *_: zeros, pipeline_mode=pl.Buffered(1))


def _layer_scratch():
    return [pltpu.VMEM((TOKEN_BLOCK, D_IN), F32),
            pltpu.VMEM((TOKEN_BLOCK, D_MODEL), BF16),
            pltpu.VMEM((TOKEN_BLOCK, HG_W), F32),
            pltpu.VMEM((TOKEN_BLOCK, D_MIX), BF16)]


def _memory_kv(mem, g, wk, wv):
    b = mem.shape[0]
    out_f = jax.ShapeDtypeStruct((b, N_MEM * XA_H, XA_DH), F32)
    out_b = jax.ShapeDtypeStruct((b, N_MEM, XA_W), BF16)
    blk_f = pl.BlockSpec((1, N_MEM * XA_H, XA_DH), lambda i: (i, 0, 0))
    blk_b = pl.BlockSpec((1, N_MEM, XA_W), lambda i: (i, 0, 0))
    return pl.pallas_call(
        _memory_kv_kernel,
        grid=(b,),
        in_specs=[pl.BlockSpec((1, N_MEM, D_MODEL), lambda i: (i, 0, 0)),
                  pl.BlockSpec((1, D_MODEL), lambda i: (0, 0)),
                  pl.BlockSpec((D_MODEL, XA_W), lambda i: (0, 0)),
                  pl.BlockSpec((D_MODEL, XA_W), lambda i: (0, 0))],
        out_specs=[blk_f, blk_f, blk_b, blk_b],
        out_shape=[out_f, out_f, out_b, out_b],
        name="memory_kv",
    )(mem, g.reshape(1, D_MODEL), wk, wv)


def _prompt_layer(x, mk_b, mv_b, w_in, w_out, params):
    b, t, _ = x.shape
    tb = PROMPT_BLOCKS_PER_STEP * TOKEN_BLOCK
    spp = t // tb
    prep = WEIGHT_PREP_STEPS
    cos_t, sin_t = _rope_tables(np.arange(t))
    mdec, qdec, kdec, rdec = _retention_tables(TOKEN_BLOCK)
    consts = (mdec, qdec, kdec, rdec)
    in_rows, out_rows = D_MODEL // prep, D_MIX // prep

    def slab(step):
        return jnp.minimum(step, prep - 1)

    def run(step):
        return jnp.maximum(step - prep, 0)

    in_specs = ([pl.BlockSpec((in_rows, D_IN), lambda s: (slab(s), 0)),
                 pl.BlockSpec((out_rows, D_MODEL), lambda s: (slab(s), 0)),
                 pl.BlockSpec((1, tb, D_MODEL), lambda s: (run(s) // spp, run(s) % spp, 0)),
                 pl.BlockSpec((tb, LANES), lambda s: (run(s) % spp, 0)),
                 pl.BlockSpec((tb, LANES), lambda s: (run(s) % spp, 0))]
                + [_const_spec(c.shape) for c in consts]
                + [pl.BlockSpec((1, N_MEM, XA_W), lambda s: (run(s) // spp, 0, 0)),
                   pl.BlockSpec((1, N_MEM, XA_W), lambda s: (run(s) // spp, 0, 0))]
                + [_const_spec(p.shape) for p in params])
    out_shape = [jax.ShapeDtypeStruct((b, t, D_MODEL), F32),
                 jax.ShapeDtypeStruct((b, HG_H, HG_DK, HG_DV), F32),
                 jax.ShapeDtypeStruct((b, RT_H // 2, 2 * RT_DK, RT_DV), F32),
                 jax.ShapeDtypeStruct((D_MODEL, D_IN), BF16),
                 jax.ShapeDtypeStruct((D_MIX, D_MODEL), BF16)]
    out_specs = [pl.BlockSpec((1, tb, D_MODEL), lambda s: (run(s) // spp, run(s) % spp, 0)),
                 pl.BlockSpec((1, HG_H, HG_DK, HG_DV), lambda s: (run(s) // spp, 0, 0, 0)),
                 pl.BlockSpec((1, RT_H // 2, 2 * RT_DK, RT_DV), lambda s: (run(s) // spp, 0, 0, 0)),
                 pl.BlockSpec((in_rows, D_IN), lambda s: (slab(s), 0)),
                 pl.BlockSpec((out_rows, D_MODEL), lambda s: (slab(s), 0))]
    return pl.pallas_call(
        functools.partial(_prompt_kernel, spp),
        grid=(prep + b * spp,),
        in_specs=in_specs,
        out_specs=out_specs,
        out_shape=out_shape,
        scratch_shapes=_layer_scratch() + [pltpu.VMEM((HG_H, HG_DV, HG_DK), F32),
                                           pltpu.VMEM((D_MODEL, D_IN), BF16),
                                           pltpu.VMEM((D_MIX, D_MODEL), BF16)],
        compiler_params=pltpu.CompilerParams(dimension_semantics=("arbitrary",),
                                             vmem_limit_bytes=VMEM_LIMIT_BYTES),
        name="prompt_layer",
    )(w_in, w_out, x, cos_t, sin_t, *consts, mk_b, mv_b, *params)


def _sample_layer(x, s_hg, s_rt, mem_k, mem_v, weights):
    b, t, _ = x.shape
    assert t == CHUNK
    tb = TOKEN_BLOCK
    nsb = tb // t
    cos_t, sin_t = _rope_tables(PAST_LEN + np.arange(t))
    mdec, qdec, kdec, rdec = _retention_tables(t)
    consts = (mdec, qdec, kdec, rdec)
    s_rt2 = s_rt.reshape(b, RT_H // 2, 2 * RT_DK, RT_DV)
    mem_spec = pl.BlockSpec((nsb, N_MEM * XA_H, XA_DH), lambda i: (i, 0, 0))
    in_specs = ([pl.BlockSpec((tb, D_MODEL), lambda i: (i, 0)),
                 _const_spec(cos_t.shape), _const_spec(sin_t.shape)]
                + [_const_spec(c.shape) for c in consts]
                + [mem_spec, mem_spec,
                   pl.BlockSpec((nsb, HG_H, HG_DK, HG_DV), lambda i: (i, 0, 0, 0)),
                   pl.BlockSpec((nsb, RT_H // 2, 2 * RT_DK, RT_DV), lambda i: (i, 0, 0, 0))]
                + [_const_spec(w.shape) for w in weights])
    out_shape = [jax.ShapeDtypeStruct((b * t, D_MODEL), F32),
                 jax.ShapeDtypeStruct((b, HG_H, HG_DK, HG_DV), F32),
                 jax.ShapeDtypeStruct((b, RT_H // 2, 2 * RT_DK, RT_DV), F32)]
    out_specs = [pl.BlockSpec((tb, D_MODEL), lambda i: (i, 0)),
                 pl.BlockSpec((nsb, HG_H, HG_DK, HG_DV), lambda i: (i, 0, 0, 0)),
                 pl.BlockSpec((nsb, RT_H // 2, 2 * RT_DK, RT_DV), lambda i: (i, 0, 0, 0))]
    y, s_new, r_new = pl.pallas_call(
        _sample_kernel,
        grid=(b // nsb,),
        in_specs=in_specs,
        out_specs=out_specs,
        out_shape=out_shape,
        scratch_shapes=_layer_scratch(),
        compiler_params=pltpu.CompilerParams(dimension_semantics=("arbitrary",),
                                             vmem_limit_bytes=VMEM_LIMIT_BYTES),
        name="sample_layer",
    )(x.reshape(b * t, D_MODEL), cos_t, sin_t, *consts,
      mem_k.reshape(b, N_MEM * XA_H, XA_DH), mem_v.reshape(b, N_MEM * XA_H, XA_DH), s_hg, s_rt2, *weights)
    return y.reshape(b, t, D_MODEL), s_new, r_new


def kernel(x_prompt, x_sample, mem_prompt, state_hgrn, state_ret, cache_mem_k, cache_mem_v, norm_g, w_in,
           lb_logits, hg_norm_g, rt_norm_g, mem_norm_g, w_mem_k, w_mem_v, w_out, final_norm_g):
    depth = w_in.shape[0]
    assert depth == 1, "single-layer step"
    bp = x_prompt.shape[0]
    bs = x_sample.shape[0]
    g_in, g_hg, g_rt = norm_g[0].reshape(1, D_MODEL), hg_norm_g[0].reshape(1, HG_W), rt_norm_g[0].reshape(1, RT_W)
    g_fin = final_norm_g.reshape(1, D_MODEL)
    mk, mv, mk_b, mv_b = _memory_kv(mem_prompt, mem_norm_g[0], w_mem_k[0], w_mem_v[0])
    y_p, s_p, r_p, w_in_b, w_out_b = _prompt_layer(x_prompt, mk_b, mv_b, w_in[0], w_out[0],
                                                   (g_in, lb_logits, g_hg, g_rt, g_fin))
    y_s, s_s, r_s = _sample_layer(x_sample, state_hgrn[0], state_ret[0], cache_mem_k[0], cache_mem_v[0],
                                  (g_in, w_in_b, lb_logits, g_hg, g_rt, w_out_b, g_fin))
    return (y_p, y_s,
            s_p.reshape(1, bp, HG_H, HG_DK, HG_DV),
            r_p.reshape(1, bp, RT_H, RT_DK, RT_DV),
            mk.reshape(1, bp, N_MEM, XA_H, XA_DH),
            mv.reshape(1, bp, N_MEM, XA_H, XA_DH),
            s_s.reshape(1, bs, HG_H, HG_DK, HG_DV),
            r_s.reshape(1, bs, RT_H, RT_DK, RT_DV))
```

```python
import functools

import numpy as np
import jax
import jax.numpy as jnp
from jax import lax
from jax.experimental import pallas as pl
from jax.experimental.pallas import tpu as pltpu

F32 = jnp.float32
BF16 = jnp.bfloat16

D_MODEL = 1024
PAST_LEN = 4096
CHUNK = 64
N_MEM = 256
EPS = 1e-6
ROPE_BASE = 10000.0
HG_H, HG_DK, HG_DV = 8, 128, 128
HG_W = HG_H * HG_DV
RT_H, RT_DK, RT_DV = 4, 64, 128
RT_QK = RT_H * RT_DK
RT_W = RT_H * RT_DV
XA_H, XA_DH = 4, 128
XA_W = XA_H * XA_DH
D_MIX = HG_W + RT_W + XA_W
OFF_HG_Q = 0
OFF_HG_F = OFF_HG_Q + HG_W
OFF_HG_I = OFF_HG_F + HG_W
OFF_HG_GATE = OFF_HG_I + HG_W
OFF_RT_Q = OFF_HG_GATE + HG_W
OFF_RT_K = OFF_RT_Q + RT_QK
OFF_RT_V = OFF_RT_K + RT_QK
OFF_RT_GATE = OFF_RT_V + RT_W
OFF_XA_Q = OFF_RT_GATE + RT_W
OFF_XA_GATE = OFF_XA_Q + XA_W
D_IN = OFF_XA_GATE + XA_W

LANES = 128
SUBLANES = 8
TOKEN_BLOCK = 256
PROMPT_BLOCKS_PER_STEP = 4
WEIGHT_PREP_STEPS = 16
PROJ_N_TILE = 512
PROJ_TILES_PER_GATE_GROUP = 2
PROJ_TILES_PER_HGRN_STAGE = 1
HGRN_STREAMS_PER_STAGE = 4
OUT_N_TILE = 512
VMEM_LIMIT_BYTES = 56 * 1024 * 1024
_NT = (((1,), (1,)), ((), ()))
_TN = (((0,), (0,)), ((), ()))


def _silu(x):
    h = 0.5 * x
    return h + h * jnp.tanh(h)


def _rms(x, g):
    return x * lax.rsqrt(jnp.mean(x * x, axis=-1, keepdims=True) + EPS) * g


def _cols(off, i, width):
    return slice(off + i * width, off + (i + 1) * width)


def _rope_tables(pos):
    half = RT_DK // 2
    inv_freq = ROPE_BASE ** (-np.arange(half, dtype=np.float64) / half)
    ang = np.asarray(pos, np.float64)[:, None] * inv_freq[None, :]
    cos, sin = np.cos(ang), np.sin(ang)
    cos_t = np.concatenate([cos, cos, cos, cos], axis=-1)
    sin_t = np.concatenate([-sin, sin, -sin, sin], axis=-1)
    return jnp.asarray(cos_t, F32), jnp.asarray(sin_t, F32)


def _retention_tables(ts):
    h = np.arange(RT_H, dtype=np.float64)
    log_gamma = np.log1p(-np.exp2(-5.0 - h))
    n = np.arange(ts, dtype=np.float64)
    scale = RT_DK ** -0.5
    dist = np.abs(n[:, None] - n[None, :])
    visible = (n[None, :] // CHUNK) <= (n[:, None] // CHUNK)
    mdec = scale * np.exp(log_gamma[:, None, None] * dist[None]) * visible[None]
    qd = np.exp(log_gamma[:, None] * (n[None, :] + 1.0))
    kd = scale * np.exp(log_gamma[:, None] * (ts - 1.0 - n[None, :]))
    rd = np.exp(log_gamma * ts)

    def pair_lanes(a):
        a = np.repeat(a[:, :, None], RT_DK, axis=2)
        return np.concatenate([a[0::2], a[1::2]], axis=2)

    rdec = np.repeat(np.repeat(rd[:, None], RT_DK, axis=1).reshape(RT_H // 2, 2 * RT_DK, 1), LANES, axis=2)
    return (jnp.asarray(mdec, F32), jnp.asarray(pair_lanes(qd), F32), jnp.asarray(pair_lanes(kd), F32),
            jnp.asarray(rdec, F32))


def _hgrn_chunks(proj_ref, gc_ref, mixed_ref, hg_g_ref, chunks, score_mask, get_st, set_st, proj):
    half = CHUNK // 2
    zero_half = jnp.zeros((half, HG_DK), BF16)
    insts = [(s, h) for s, _ in chunks for h in range(HG_H)]
    rows = dict(chunks)
    q_rel, k_rel, q_abs, k_end, v_b, dec = {}, {}, {}, {}, {}, {}

    def bf(*row_blocks):
        return jnp.concatenate([p.astype(BF16) for p in row_blocks], axis=0)

    for (s, h) in insts:
        q = _silu(proj_ref[rows[s], _cols(OFF_HG_Q, h, HG_DK)])
        k = proj_ref[rows[s], _cols(OFF_HG_F, h, HG_DK)]
        gc = gc_ref[rows[s], _cols(0, h, HG_DK)]
        g_end = gc[CHUNK - 1:CHUNK, :]
        (q1, q2), (k1, k2), (g1, g2) = ((a[:half], a[half:]) for a in (q, k, gc))
        g_m1, g_m2 = gc[half // 2 - 1:half // 2, :], gc[half + half // 2 - 1:half + half // 2, :]
        g_b = gc[half - 1:half, :]
        qd1, kd1 = q1 * jnp.exp2(g1 - g_m1), k1 * jnp.exp2(g_m1 - g1)
        qd2, kd2 = q2 * jnp.exp2(g2 - g_m2), k2 * jnp.exp2(g_m2 - g2)
        qo, ko = qd2 * jnp.exp2(g_m2 - g_b), kd1 * jnp.exp2(g_b - g_m1)
        q_rel[s, h] = jnp.concatenate([bf(qd1, qd2), bf(zero_half, qo)], axis=1)
        k_rel[s, h] = jnp.concatenate([bf(kd1, kd2, zero_half, zero_half),
                                       bf(zero_half, zero_half, ko, zero_half)], axis=1)
        q_abs[s, h] = bf(qd1 * jnp.exp2(g_m1), qd2 * jnp.exp2(g_m2))
        k_end[s, h] = bf(kd1 * jnp.exp2(g_end - g_m1), kd2 * jnp.exp2(g_end - g_m2))
        v_b[s, h] = proj_ref[rows[s], _cols(OFF_HG_I, h, HG_DV)].astype(BF16)
        dec[s, h] = jnp.exp2(g_end)
    proj.emit(PROJ_TILES_PER_HGRN_STAGE)
    a = {k: lax.dot_general(q_rel[k], k_rel[k], _NT, preferred_element_type=F32) for k in insts}
    st = {k: get_st(*k) for k in insts}
    v_t = {k: jnp.concatenate([v_b[k], v_b[k][:half], zero_half], axis=0).T for k in insts}
    kv = {k: jnp.dot(v_t[k][:, :CHUNK], k_end[k], preferred_element_type=F32) for k in insts}
    for k in insts:
        set_st(*k, dec[k] * st[k] + kv[k])
    a_m = {k: jnp.where(score_mask, a[k], 0.0).astype(BF16) for k in insts}
    o = {k: lax.dot_general(jnp.concatenate([q_abs[k], a_m[k]], axis=1),
                            jnp.concatenate([st[k].astype(BF16), v_t[k]], axis=1), _NT, preferred_element_type=F32)
         for k in insts}
    ms = {k: jnp.mean(o[k] * o[k], axis=-1, keepdims=True) for k in insts}
    proj.emit(PROJ_TILES_PER_HGRN_STAGE)
    for (s, h) in insts:
        gate = _silu(proj_ref[rows[s], _cols(OFF_HG_GATE, h, HG_DV)].astype(BF16))
        on = o[s, h] * lax.rsqrt(ms[s, h] + EPS) * hg_g_ref[:, _cols(0, h, HG_DV)]
        mixed_ref[rows[s], _cols(0, h, HG_DV)] = on.astype(BF16) * gate


def _retention(n_streams, ts, proj_ref, mixed_ref, cos_t, sin_t, mdec_ref, qdec_ref, kdec_ref, rdec_ref,
               rt_g_ref, get_r, set_r):
    lane = lax.broadcasted_iota(jnp.int32, (ts, LANES), 1)
    first_half = (lane % RT_DK) < (RT_DK // 2)
    low_head = lane < RT_DK
    low_rows = lax.broadcasted_iota(jnp.int32, (2 * RT_DK, RT_DV), 0) < RT_DK

    def rope(xs, cs, sn):
        swapped = jnp.where(first_half, pltpu.roll(xs, LANES - RT_DK // 2, 1), pltpu.roll(xs, RT_DK // 2, 1))
        return xs * cs + swapped * sn

    pairs = [(s, p) for s in range(n_streams) for p in range(RT_H // 2)]
    insts = [(s, p, i) for (s, p) in pairs for i in range(2)]
    rows = {s: slice(s * ts, (s + 1) * ts) for s in range(n_streams)}
    qp, kp_b, qd, kd_b, r_pair, v_pair = {}, {}, {}, {}, {}, {}
    for (s, p) in pairs:
        cs, sn = cos_t[rows[s], :], sin_t[rows[s], :]
        q = rope(proj_ref[rows[s], _cols(OFF_RT_Q, p, LANES)], cs, sn)
        k = rope(proj_ref[rows[s], _cols(OFF_RT_K, p, LANES)], cs, sn)
        qp[s, p] = q
        qd[s, p] = q * qdec_ref[p]
        kp_b[s, p] = k.astype(BF16)
        kd_b[s, p] = (k * kdec_ref[p]).astype(BF16)
        r_pair[s, p] = get_r(s, p)
        v_pair[s, p] = proj_ref[rows[s], _cols(OFF_RT_V, p, 2 * RT_DV)].astype(BF16)
    yield
    sel = {0: low_head, 1: jnp.logical_not(low_head)}
    q_h = {(s, p, i): jnp.where(sel[i], qp[s, p], 0.0).astype(BF16) for (s, p, i) in insts}
    qd_h = {(s, p, i): jnp.where(sel[i], qd[s, p], 0.0).astype(BF16) for (s, p, i) in insts}
    sc = {(s, p, i): lax.dot_general(q_h[s, p, i], kp_b[s, p], _NT, preferred_element_type=F32)
          for (s, p, i) in insts}
    inter = {(s, p, i): jnp.dot(qd_h[s, p, i], r_pair[s, p].astype(BF16), preferred_element_type=F32)
             for (s, p, i) in insts}
    kv = {(s, p): lax.dot_general(kd_b[s, p], v_pair[s, p], _TN, preferred_element_type=F32) for (s, p) in pairs}
    yield
    for (s, p) in pairs:
        dr = jnp.where(low_rows, kv[s, p][:, :RT_DV], kv[s, p][:, RT_DV:])
        set_r(s, p, rdec_ref[p] * r_pair[s, p] + dr)
    sc_b = {(s, p, i): (sc[s, p, i] * mdec_ref[2 * p + i]).astype(BF16) for (s, p, i) in insts}
    yield
    o = {(s, p, i): jnp.dot(sc_b[s, p, i], v_pair[s, p][:, i * RT_DV:(i + 1) * RT_DV],
                            preferred_element_type=F32) + inter[s, p, i] for (s, p, i) in insts}
    yield
    mu = {k: jnp.mean(o[k], axis=-1, keepdims=True) for k in insts}
    cen = {k: o[k] - mu[k] for k in insts}
    var = {k: jnp.mean(cen[k] * cen[k], axis=-1, keepdims=True) for k in insts}
    yield
    for (s, p, i) in insts:
        h = 2 * p + i
        gate = _silu(proj_ref[rows[s], _cols(OFF_RT_GATE, h, RT_DV)].astype(BF16))
        on = cen[s, p, i] * lax.rsqrt(var[s, p, i] + EPS) * rt_g_ref[:, _cols(0, h, RT_DV)]
        mixed_ref[rows[s], _cols(HG_W, h, RT_DV)] = on.astype(BF16) * gate


def _cross_attention(n_streams, ts, proj_ref, mixed_ref, get_mem):
    xa_scale = XA_DH ** -0.5
    insts = [(s, h) for s in range(n_streams) for h in range(XA_H)]
    rows = {s: slice(s * ts, (s + 1) * ts) for s in range(n_streams)}
    mem = {k: get_mem(*k) for k in insts}
    sc = {(s, h): lax.dot_general(proj_ref[rows[s], _cols(OFF_XA_Q, h, XA_DH)].astype(BF16), mem[s, h][0], _NT,
                                  preferred_element_type=F32) * xa_scale for (s, h) in insts}
    yield
    mx = {k: jnp.max(sc[k], axis=-1, keepdims=True) for k in insts}
    yield
    pr = {k: jnp.exp(sc[k] - mx[k]) for k in insts}
    den = {k: jnp.sum(pr[k], axis=-1, keepdims=True) for k in insts}
    yield
    o = {k: jnp.dot(pr[k].astype(BF16), mem[k][1], preferred_element_type=F32) for k in insts}
    yield
    for (s, h) in insts:
        gate = _silu(proj_ref[rows[s], _cols(OFF_XA_GATE, h, XA_DH)].astype(BF16))
        mixed_ref[rows[s], _cols(HG_W + RT_W, h, XA_DH)] = (o[s, h] / den[s, h]).astype(BF16) * gate


def _chunk_cumsum(g):
    tokens, c = g.shape
    x = g.reshape(tokens // SUBLANES, SUBLANES, c)
    sub = lax.broadcasted_iota(jnp.int32, x.shape, 1)
    shift = 1
    while shift < SUBLANES:
        x = x + jnp.where(sub >= shift, pltpu.roll(x, shift, 1), 0.0)
        shift *= 2
    tiles = CHUNK // SUBLANES
    x = x.reshape(tokens // CHUNK, tiles, SUBLANES, c)
    out = [x[:, 0]]
    for j in range(1, tiles):
        total = out[-1][:, SUBLANES - 1:SUBLANES, :]
        out.append(x[:, j] + total)
    return jnp.stack(out, axis=1).reshape(tokens, c)


class _Projection:
    SEGMENTS = ((OFF_HG_F, OFF_HG_I), (OFF_HG_Q, OFF_HG_F), (OFF_HG_I, OFF_RT_Q), (OFF_RT_Q, D_IN))
    TILES_GATES = HG_W // PROJ_N_TILE
    TILES_HGRN = 4 * HG_W // PROJ_N_TILE
    TILES_ALL = D_IN // PROJ_N_TILE

    def __init__(self, x, norm_g_ref, w_in_ref, xn_ref, proj_ref):
        xn_ref[...] = _rms(x, norm_g_ref[...]).astype(BF16)
        self._refs = (xn_ref, w_in_ref, proj_ref)
        self._pending = [n0 for lo, hi in self.SEGMENTS for n0 in range(lo, hi, PROJ_N_TILE)]
        assert len(self._pending) == self.TILES_ALL
        self._done = 0

    def emit(self, n=1):
        xn_ref, w_in_ref, proj_ref = self._refs
        for _ in range(min(n, len(self._pending))):
            cols = slice(self._pending[0], self._pending.pop(0) + PROJ_N_TILE)
            proj_ref[:, cols] = jnp.dot(xn_ref[...], w_in_ref[:, cols], preferred_element_type=F32)
            self._done += 1

    def require(self, n_tiles):
        self.emit(n_tiles - self._done)


def _mix(n_streams, ts, x_ref, y_ref, cos_t, sin_t, mdec_ref, qdec_ref, kdec_ref, rdec_ref,
         get_mem, lb_ref, hg_g_ref, rt_g_ref, w_out_ref, fin_g_ref,
         proj_ref, gc_ref, mixed_ref, get_st, set_st, get_r, set_r, proj):
    proj.require(proj.TILES_GATES)
    lg = lb_ref[...]
    e = jnp.exp(lg - jnp.max(lg, axis=0, keepdims=True))
    lb = e[0:1, :] / jnp.sum(e, axis=0, keepdims=True)
    for c0 in range(0, HG_W, 2 * LANES):
        cols = slice(OFF_HG_F + c0, OFF_HG_F + c0 + 2 * LANES)
        lbc = lb[:, c0:c0 + 2 * LANES]
        half_span = 0.5 * (1.0 - lbc)
        bt = half_span * jnp.tanh(0.5 * proj_ref[:, cols])
        g = jnp.log2(lbc + (half_span + bt))
        proj_ref[:, cols] = half_span - bt
        gc_ref[:, c0:c0 + 2 * LANES] = _chunk_cumsum(g)
        proj.emit(PROJ_TILES_PER_GATE_GROUP)

    proj.require(proj.TILES_HGRN)
    row = lax.broadcasted_iota(jnp.int32, (CHUNK, LANES), 0)
    col = lax.broadcasted_iota(jnp.int32, (CHUNK, LANES), 1)
    half = CHUNK // 2
    causal = (((col <= row) & (col // half == row // half))
              | ((col >= CHUNK) & (col < CHUNK + half) & (row >= half)))
    for c in range(ts // CHUNK):
        chunks = [(s, slice(s * ts + c * CHUNK, s * ts + (c + 1) * CHUNK)) for s in range(n_streams)]
        for i in range(0, n_streams, HGRN_STREAMS_PER_STAGE):
            _hgrn_chunks(proj_ref, gc_ref, mixed_ref, hg_g_ref, chunks[i:i + HGRN_STREAMS_PER_STAGE], causal,
                         get_st, set_st, proj)

    proj.require(proj.TILES_ALL)
    _run_interleaved(
        _retention(n_streams, ts, proj_ref, mixed_ref, cos_t, sin_t, mdec_ref, qdec_ref, kdec_ref, rdec_ref,
                   rt_g_ref, get_r, set_r),
        _cross_attention(n_streams, ts, proj_ref, mixed_ref, get_mem),
        _out_projection(mixed_ref, w_out_ref, y_ref, 0, HG_W, x_ref))
    sumsq = []
    _run_interleaved(_out_projection(mixed_ref, w_out_ref, y_ref, HG_W, D_MIX, y_ref, sumsq))

    inv = lax.rsqrt(sum(sumsq) * (1.0 / D_MODEL) + EPS)
    y_ref[...] = y_ref[...] * inv * fin_g_ref[...]


def _out_projection(mixed_ref, w_out_ref, acc_ref, k_lo, k_hi, base_ref, sumsq=None):
    for n0 in range(0, D_MODEL, OUT_N_TILE):
        cols = slice(n0, n0 + OUT_N_TILE)
        out = base_ref[:, cols] + jnp.dot(mixed_ref[:, k_lo:k_hi], w_out_ref[k_lo:k_hi, cols],
                                          preferred_element_type=F32)
        acc_ref[:, cols] = out
        if sumsq is not None:
            sumsq.append(jnp.sum(out * out, axis=-1, keepdims=True))
        yield


def _run_interleaved(*stage_generators):
    pending = list(stage_generators)
    while pending:
        for g in list(pending):
            try:
                next(g)
            except StopIteration:
                pending.remove(g)


def _memory_kv_kernel(mem_ref, g_ref, wk_ref, wv_ref, k_ref, v_ref, kb_ref, vb_ref):
    m = _rms(mem_ref[0], g_ref[...]).astype(BF16)
    k = jnp.dot(m, wk_ref[...].astype(BF16), preferred_element_type=F32)
    v = jnp.dot(m, wv_ref[...].astype(BF16), preferred_element_type=F32)
    for h in range(XA_H):
        k_ref[0, pl.ds(h, N_MEM, stride=XA_H), :] = k[:, _cols(0, h, XA_DH)]
        v_ref[0, pl.ds(h, N_MEM, stride=XA_H), :] = v[:, _cols(0, h, XA_DH)]
    kb_ref[0] = k.astype(BF16)
    vb_ref[0] = v.astype(BF16)


def _prompt_kernel(steps_per_stream, w_in_slab_ref, w_out_slab_ref,
                   x_ref, cos_ref, sin_ref, mdec_ref, qdec_ref, kdec_ref, rdec_ref, mk_ref, mv_ref,
                   norm_g_ref, lb_ref, hg_g_ref, rt_g_ref, fin_g_ref,
                   y_ref, s_out_ref, r_out_ref, w_in_b_slab_ref, w_out_b_slab_ref,
                   proj_ref, xn_ref, gc_ref, mixed_ref, st_ref, w_in_ref, w_out_ref):
    t = pl.program_id(0)
    j = (t - WEIGHT_PREP_STEPS) % steps_per_stream
    running = t >= WEIGHT_PREP_STEPS

    @pl.when(jnp.logical_not(running))
    def _():
        for slab_ref, out_ref, full_ref in ((w_in_slab_ref, w_in_b_slab_ref, w_in_ref),
                                            (w_out_slab_ref, w_out_b_slab_ref, w_out_ref)):
            n = slab_ref.shape[0]
            slab = slab_ref[...].astype(BF16)
            full_ref[pl.ds(pl.multiple_of(t * n, n), n), :] = slab
            out_ref[...] = slab

    @pl.when(running & (j == 0))
    def _():
        st_ref[...] = jnp.zeros_like(st_ref)
        r_out_ref[...] = jnp.zeros_like(r_out_ref)

    def set_st(s, h, val):
        st_ref[h] = val

    def set_r(s, p, val):
        r_out_ref[0, p] = val

    def get_mem(s, h):
        return mk_ref[0, :, _cols(0, h, XA_DH)], mv_ref[0, :, _cols(0, h, XA_DH)]

    def token_block(i, carry):
        rows = pl.ds(pl.multiple_of(i * TOKEN_BLOCK, TOKEN_BLOCK), TOKEN_BLOCK)
        x_blk = x_ref.at[0, rows]
        proj = _Projection(x_blk[...], norm_g_ref, w_in_ref, xn_ref, proj_ref)
        _mix(
            1, TOKEN_BLOCK, x_blk, y_ref.at[0, rows], cos_ref[rows, :], sin_ref[rows, :],
            mdec_ref, qdec_ref, kdec_ref, rdec_ref,
            get_mem, lb_ref, hg_g_ref, rt_g_ref, w_out_ref, fin_g_ref,
            proj_ref, gc_ref, mixed_ref,
            lambda s, h: st_ref[h], set_st, lambda s, p: r_out_ref[0, p], set_r, proj)
        return carry

    @pl.when(running)
    def _():
        lax.fori_loop(0, x_ref.shape[1] // TOKEN_BLOCK, token_block, 0)

    @pl.when(running & (j == steps_per_stream - 1))
    def _():
        for h in range(HG_H):
            s_out_ref[0, h] = st_ref[h].T


def _sample_kernel(x_ref, cos_ref, sin_ref, mdec_ref, qdec_ref, kdec_ref, rdec_ref, mk_ref, mv_ref,
                   s_in_ref, r_in_ref,
                   norm_g_ref, w_in_ref, lb_ref, hg_g_ref, rt_g_ref, w_out_ref, fin_g_ref,
                   y_ref, s_out_ref, r_out_ref, proj_ref, xn_ref, gc_ref, mixed_ref):
    n_streams = TOKEN_BLOCK // CHUNK

    def set_st(s, h, val):
        s_out_ref[s, h] = val.T

    def set_r(s, p, val):
        r_out_ref[s, p] = val

    def get_mem(s, h):
        head_rows = pl.ds(h, N_MEM, stride=XA_H)
        return mk_ref[s, head_rows, :].astype(BF16), mv_ref[s, head_rows, :].astype(BF16)

    cos_t = jnp.concatenate([cos_ref[...]] * n_streams, axis=0)
    sin_t = jnp.concatenate([sin_ref[...]] * n_streams, axis=0)
    proj = _Projection(x_ref[...], norm_g_ref, w_in_ref, xn_ref, proj_ref)
    _mix(
        n_streams, CHUNK, x_ref, y_ref, cos_t, sin_t, mdec_ref, qdec_ref, kdec_ref, rdec_ref,
        get_mem, lb_ref, hg_g_ref, rt_g_ref, w_out_ref, fin_g_ref,
        proj_ref, gc_ref, mixed_ref,
        lambda s, h: s_in_ref[s, h].T, set_st, lambda s, p: r_in_ref[s, p], set_r, proj)


def _const_spec(shape):
    zeros = (0,) * len(shape)
    return pl.BlockSpec(shape, lambda *_: zeros, pipeline_mode=pl.Buffered(1))


def _layer_scratch():
    return [pltpu.VMEM((TOKEN_BLOCK, D_IN), F32),
            pltpu.VMEM((TOKEN_BLOCK, D_MODEL), BF16),
            pltpu.VMEM((TOKEN_BLOCK, HG_W), F32),
            pltpu.VMEM((TOKEN_BLOCK, D_MIX), BF16)]


def _memory_kv(mem, g, wk, wv):
    b = mem.shape[0]
    out_f = jax.ShapeDtypeStruct((b, N_MEM * XA_H, XA_DH), F32)
    out_b = jax.ShapeDtypeStruct((b, N_MEM, XA_W), BF16)
    blk_f = pl.BlockSpec((1, N_MEM * XA_H, XA_DH), lambda i: (i, 0, 0))
    blk_b = pl.BlockSpec((1, N_MEM, XA_W), lambda i: (i, 0, 0))
    return pl.pallas_call(
        _memory_kv_kernel,
        grid=(b,),
        in_specs=[pl.BlockSpec((1, N_MEM, D_MODEL), lambda i: (i, 0, 0)),
                  pl.BlockSpec((1, D_MODEL), lambda i: (0, 0)),
                  pl.BlockSpec((D_MODEL, XA_W), lambda i: (0, 0)),
                  pl.BlockSpec((D_MODEL, XA_W), lambda i: (0, 0))],
        out_specs=[blk_f, blk_f, blk_b, blk_b],
        out_shape=[out_f, out_f, out_b, out_b],
        name="memory_kv",
    )(mem, g.reshape(1, D_MODEL), wk, wv)


def _prompt_layer(x, mk_b, mv_b, w_in, w_out, params):
    b, t, _ = x.shape
    tb = PROMPT_BLOCKS_PER_STEP * TOKEN_BLOCK
    spp = t // tb
    prep = WEIGHT_PREP_STEPS
    cos_t, sin_t = _rope_tables(np.arange(t))
    mdec, qdec, kdec, rdec = _retention_tables(TOKEN_BLOCK)
    consts = (mdec, qdec, kdec, rdec)
    in_rows, out_rows = D_MODEL // prep, D_MIX // prep

    def slab(step):
        return jnp.minimum(step, prep - 1)

    def run(step):
        return jnp.maximum(step - prep, 0)

    in_specs = ([pl.BlockSpec((in_rows, D_IN), lambda s: (slab(s), 0)),
                 pl.BlockSpec((out_rows, D_MODEL), lambda s: (slab(s), 0)),
                 pl.BlockSpec((1, tb, D_MODEL), lambda s: (run(s) // spp, run(s) % spp, 0)),
                 pl.BlockSpec((tb, LANES), lambda s: (run(s) % spp, 0)),
                 pl.BlockSpec((tb, LANES), lambda s: (run(s) % spp, 0))]
                + [_const_spec(c.shape) for c in consts]
                + [pl.BlockSpec((1, N_MEM, XA_W), lambda s: (run(s) // spp, 0, 0)),
                   pl.BlockSpec((1, N_MEM, XA_W), lambda s: (run(s) // spp, 0, 0))]
                + [_const_spec(p.shape) for p in params])
    out_shape = [jax.ShapeDtypeStruct((b, t, D_MODEL), F32),
                 jax.ShapeDtypeStruct((b, HG_H, HG_DK, HG_DV), F32),
                 jax.ShapeDtypeStruct((b, RT_H // 2, 2 * RT_DK, RT_DV), F32),
                 jax.ShapeDtypeStruct((D_MODEL, D_IN), BF16),
                 jax.ShapeDtypeStruct((D_MIX, D_MODEL), BF16)]
    out_specs = [pl.BlockSpec((1, tb, D_MODEL), lambda s: (run(s) // spp, run(s) % spp, 0)),
                 pl.BlockSpec((1, HG_H, HG_DK, HG_DV), lambda s: (run(s) // spp, 0, 0, 0)),
                 pl.BlockSpec((1, RT_H // 2, 2 * RT_DK, RT_DV), lambda s: (run(s) // spp, 0, 0, 0)),
                 pl.BlockSpec((in_rows, D_IN), lambda s: (slab(s), 0)),
                 pl.BlockSpec((out_rows, D_MODEL), lambda s: (slab(s), 0))]
    return pl.pallas_call(
        functools.partial(_prompt_kernel, spp),
        grid=(prep + b * spp,),
        in_specs=in_specs,
        out_specs=out_specs,
        out_shape=out_shape,
        scratch_shapes=_layer_scratch() + [pltpu.VMEM((HG_H, HG_DV, HG_DK), F32),
                                           pltpu.VMEM((D_MODEL, D_IN), BF16),
                                           pltpu.VMEM((D_MIX, D_MODEL), BF16)],
        compiler_params=pltpu.CompilerParams(dimension_semantics=("arbitrary",),
                                             vmem_limit_bytes=VMEM_LIMIT_BYTES),
        name="prompt_layer",
    )(w_in, w_out, x, cos_t, sin_t, *consts, mk_b, mv_b, *params)


def _sample_layer(x, s_hg, s_rt, mem_k, mem_v, weights):
    b, t, _ = x.shape
    assert t == CHUNK
    tb = TOKEN_BLOCK
    nsb = tb // t
    cos_t, sin_t = _rope_tables(PAST_LEN + np.arange(t))
    mdec, qdec, kdec, rdec = _retention_tables(t)
    consts = (mdec, qdec, kdec, rdec)
    s_rt2 = s_rt.reshape(b, RT_H // 2, 2 * RT_DK, RT_DV)
    mem_spec = pl.BlockSpec((nsb, N_MEM * XA_H, XA_DH), lambda i: (i, 0, 0))
    in_specs = ([pl.BlockSpec((tb, D_MODEL), lambda i: (i, 0)),
                 _const_spec(cos_t.shape), _const_spec(sin_t.shape)]
                + [_const_spec(c.shape) for c in consts]
                + [mem_spec, mem_spec,
                   pl.BlockSpec((nsb, HG_H, HG_DK, HG_DV), lambda i: (i, 0, 0, 0)),
                   pl.BlockSpec((nsb, RT_H // 2, 2 * RT_DK, RT_DV), lambda i: (i, 0, 0, 0))]
                + [_const_spec(w.shape) for w in weights])
    out_shape = [jax.ShapeDtypeStruct((b * t, D_MODEL), F32),
                 jax.ShapeDtypeStruct((b, HG_H, HG_DK, HG_DV), F32),
                 jax.ShapeDtypeStruct((b, RT_H // 2, 2 * RT_DK, RT_DV), F32)]
    out_specs = [pl.BlockSpec((tb, D_MODEL), lambda i: (i, 0)),
                 pl.BlockSpec((nsb, HG_H, HG_DK, HG_DV), lambda i: (i, 0, 0, 0)),
                 pl.BlockSpec((nsb, RT_H // 2, 2 * RT_DK, RT_DV), lambda i: (i, 0, 0, 0))]
    y, s_new, r_new = pl.pallas_call(
        _sample_kernel,
        grid=(b // nsb,),
        in_specs=in_specs,
        out_specs=out_specs,
        out_shape=out_shape,
        scratch_shapes=_layer_scratch(),
        compiler_params=pltpu.CompilerParams(dimension_semantics=("arbitrary",),
                                             vmem_limit_bytes=VMEM_LIMIT_BYTES),
        name="sample_layer",
    )(x.reshape(b * t, D_MODEL), cos_t, sin_t, *consts,
      mem_k.reshape(b, N_MEM * XA_H, XA_DH), mem_v.reshape(b, N_MEM * XA_H, XA_DH), s_hg, s_rt2, *weights)
    return y.reshape(b, t, D_MODEL), s_new, r_new


def kernel(x_prompt, x_sample, mem_prompt, state_hgrn, state_ret, cache_mem_k, cache_mem_v, norm_g, w_in,
           lb_logits, hg_norm_g, rt_norm_g, mem_norm_g, w_mem_k, w_mem_v, w_out, final_norm_g):
    depth = w_in.shape[0]
    assert depth == 1, "single-layer step"
    bp = x_prompt.shape[0]
    bs = x_sample.shape[0]
    g_in, g_hg, g_rt = norm_g[0].reshape(1, D_MODEL), hg_norm_g[0].reshape(1, HG_W), rt_norm_g[0].reshape(1, RT_W)
    g_fin = final_norm_g.reshape(1, D_MODEL)
    mk, mv, mk_b, mv_b = _memory_kv(mem_prompt, mem_norm_g[0], w_mem_k[0], w_mem_v[0])
    y_p, s_p, r_p, w_in_b, w_out_b = _prompt_layer(x_prompt, mk_b, mv_b, w_in[0], w_out[0],
                                                   (g_in, lb_logits, g_hg, g_rt, g_fin))
    y_s, s_s, r_s = _sample_layer(x_sample, state_hgrn[0], state_ret[0], cache_mem_k[0], cache_mem_v[0],
                                  (g_in, w_in_b, lb_logits, g_hg, g_rt, w_out_b, g_fin))
    return (y_p, y_s,
            s_p.reshape(1, bp, HG_H, HG_DK, HG_DV),
            r_p.reshape(1, bp, RT_H, RT_DK, RT_DV),
            mk.reshape(1, bp, N_MEM, XA_H, XA_DH),
            mv.reshape(1, bp, N_MEM, XA_H, XA_DH),
            s_s.reshape(1, bs, HG_H, HG_DK, HG_DV),
            r_s.reshape(1, bs, RT_H, RT_DK, RT_DV))
```

```python
import functools

import numpy as np
import jax
import jax.numpy as jnp
from jax import lax
from jax.experimental import pallas as pl
from jax.experimental.pallas import tpu as pltpu

F32 = jnp.float32
BF16 = jnp.bfloat16

D_MODEL = 1024
PAST_LEN = 4096
CHUNK = 64
N_MEM = 256
EPS = 1e-6
ROPE_BASE = 10000.0
HG_H, HG_DK, HG_DV = 8, 128, 128
HG_W = HG_H * HG_DV
RT_H, RT_DK, RT_DV = 4, 64, 128
RT_QK = RT_H * RT_DK
RT_W = RT_H * RT_DV
XA_H, XA_DH = 4, 128
XA_W = XA_H * XA_DH
D_MIX = HG_W + RT_W + XA_W
OFF_HG_Q = 0
OFF_HG_F = OFF_HG_Q + HG_W
OFF_HG_I = OFF_HG_F + HG_W
OFF_HG_GATE = OFF_HG_I + HG_W
OFF_RT_Q = OFF_HG_GATE + HG_W
OFF_RT_K = OFF_RT_Q + RT_QK
OFF_RT_V = OFF_RT_K + RT_QK
OFF_RT_GATE = OFF_RT_V + RT_W
OFF_XA_Q = OFF_RT_GATE + RT_W
OFF_XA_GATE = OFF_XA_Q + XA_W
D_IN = OFF_XA_GATE + XA_W

LANES = 128
SUBLANES = 8
TOKEN_BLOCK = 256
PROMPT_BLOCKS_PER_STEP = 1
WEIGHT_PREP_STEPS = 16
PROJ_N_TILE = 512
PROJ_TILES_PER_GATE_GROUP = 2
PROJ_TILES_PER_HGRN_STAGE = 1
HGRN_STREAMS_PER_STAGE = 4
OUT_N_TILE = 512
VMEM_LIMIT_BYTES = 56 * 1024 * 1024
_NT = (((1,), (1,)), ((), ()))
_TN = (((0,), (0,)), ((), ()))


def _silu(x):
    h = 0.5 * x
    return h + h * jnp.tanh(h)


def _rms(x, g):
    return x * lax.rsqrt(jnp.mean(x * x, axis=-1, keepdims=True) + EPS) * g


def _cols(off, i, width):
    return slice(off + i * width, off + (i + 1) * width)


def _rope_tables(pos):
    half = RT_DK // 2
    inv_freq = ROPE_BASE ** (-np.arange(half, dtype=np.float64) / half)
    ang = np.asarray(pos, np.float64)[:, None] * inv_freq[None, :]
    cos, sin = np.cos(ang), np.sin(ang)
    cos_t = np.concatenate([cos, cos, cos, cos], axis=-1)
    sin_t = np.concatenate([-sin, sin, -sin, sin], axis=-1)
    return jnp.asarray(cos_t, F32), jnp.asarray(sin_t, F32)


def _retention_tables(ts):
    h = np.arange(RT_H, dtype=np.float64)
    log_gamma = np.log1p(-np.exp2(-5.0 - h))
    n = np.arange(ts, dtype=np.float64)
    scale = RT_DK ** -0.5
    dist = np.abs(n[:, None] - n[None, :])
    visible = (n[None, :] // CHUNK) <= (n[:, None] // CHUNK)
    mdec = scale * np.exp(log_gamma[:, None, None] * dist[None]) * visible[None]
    qd = np.exp(log_gamma[:, None] * (n[None, :] + 1.0))
    kd = scale * np.exp(log_gamma[:, None] * (ts - 1.0 - n[None, :]))
    rd = np.exp(log_gamma * ts)

    def pair_lanes(a):
        a = np.repeat(a[:, :, None], RT_DK, axis=2)
        return np.concatenate([a[0::2], a[1::2]], axis=2)

    rdec = np.repeat(np.repeat(rd[:, None], RT_DK, axis=1).reshape(RT_H // 2, 2 * RT_DK, 1), LANES, axis=2)
    return (jnp.asarray(mdec, F32), jnp.asarray(pair_lanes(qd), F32), jnp.asarray(pair_lanes(kd), F32),
            jnp.asarray(rdec, F32))


def _hgrn_chunks(proj_ref, gc_ref, mixed_ref, hg_g_ref, chunks, score_mask, get_st, set_st, proj):
    half = CHUNK // 2
    zero_half = jnp.zeros((half, HG_DK), BF16)
    insts = [(s, h) for s, _ in chunks for h in range(HG_H)]
    rows = dict(chunks)
    q_rel, k_rel, q_abs, k_end, v_b, dec = {}, {}, {}, {}, {}, {}
    for (s, h) in insts:
        q = _silu(proj_ref[rows[s], _cols(OFF_HG_Q, h, HG_DK)])
        k = proj_ref[rows[s], _cols(OFF_HG_F, h, HG_DK)]
        gc = gc_ref[rows[s], _cols(0, h, HG_DK)]
        g_end = gc[CHUNK - 1:CHUNK, :]
        (q1, q2), (k1, k2), (g1, g2) = ((a[:half], a[half:]) for a in (q, k, gc))
        g_m1, g_m2 = gc[half // 2 - 1:half // 2, :], gc[half + half // 2 - 1:half + half // 2, :]
        g_b = gc[half - 1:half, :]
        qd1, kd1 = (q1 * jnp.exp2(g1 - g_m1)).astype(BF16), (k1 * jnp.exp2(g_m1 - g1)).astype(BF16)
        qd2, kd2 = (q2 * jnp.exp2(g2 - g_m2)).astype(BF16), (k2 * jnp.exp2(g_m2 - g2)).astype(BF16)
        qo, ko = (q2 * jnp.exp2(g2 - g_b)).astype(BF16), (k1 * jnp.exp2(g_b - g1)).astype(BF16)
        q_rel[s, h] = jnp.concatenate([jnp.concatenate([qd1, qd2], axis=0),
                                       jnp.concatenate([zero_half, qo], axis=0)], axis=1)
        k_rel[s, h] = jnp.concatenate([jnp.concatenate([kd1, kd2, zero_half, zero_half], axis=0),
                                       jnp.concatenate([zero_half, zero_half, ko, zero_half], axis=0)],
                                      axis=1)
        q_abs[s, h] = (q * jnp.exp2(gc)).astype(BF16)
        k_end[s, h] = (k * jnp.exp2(g_end - gc)).astype(BF16)
        v_b[s, h] = proj_ref[rows[s], _cols(OFF_HG_I, h, HG_DV)].astype(BF16)
        dec[s, h] = jnp.exp2(g_end)
    proj.emit(PROJ_TILES_PER_HGRN_STAGE)
    a = {k: lax.dot_general(q_rel[k], k_rel[k], _NT, preferred_element_type=F32) for k in insts}
    st = {k: get_st(*k) for k in insts}
    v_t = {k: jnp.concatenate([v_b[k], v_b[k][:half], zero_half], axis=0).T for k in insts}
    kv = {k: jnp.dot(v_t[k][:, :CHUNK], k_end[k], preferred_element_type=F32) for k in insts}
    for k in insts:
        set_st(*k, dec[k] * st[k] + kv[k])
    a_m = {k: jnp.where(score_mask, a[k], 0.0).astype(BF16) for k in insts}
    o = {k: lax.dot_general(jnp.concatenate([q_abs[k], a_m[k]], axis=1),
                            jnp.concatenate([st[k].astype(BF16), v_t[k]], axis=1), _NT, preferred_element_type=F32)
         for k in insts}
    ms = {k: jnp.mean(o[k] * o[k], axis=-1, keepdims=True) for k in insts}
    proj.emit(PROJ_TILES_PER_HGRN_STAGE)
    for (s, h) in insts:
        gate = _silu(proj_ref[rows[s], _cols(OFF_HG_GATE, h, HG_DV)].astype(BF16))
        on = o[s, h] * lax.rsqrt(ms[s, h] + EPS) * hg_g_ref[:, _cols(0, h, HG_DV)]
        mixed_ref[rows[s], _cols(0, h, HG_DV)] = on.astype(BF16) * gate


def _retention(n_streams, ts, proj_ref, mixed_ref, cos_t, sin_t, mdec_ref, qdec_ref, kdec_ref, rdec_ref,
               rt_g_ref, get_r, set_r):
    lane = lax.broadcasted_iota(jnp.int32, (ts, LANES), 1)
    first_half = (lane % RT_DK) < (RT_DK // 2)
    low_head = lane < RT_DK
    low_rows = lax.broadcasted_iota(jnp.int32, (2 * RT_DK, RT_DV), 0) < RT_DK

    def rope(xs, cs, sn):
        swapped = jnp.where(first_half, pltpu.roll(xs, LANES - RT_DK // 2, 1), pltpu.roll(xs, RT_DK // 2, 1))
        return xs * cs + swapped * sn

    pairs = [(s, p) for s in range(n_streams) for p in range(RT_H // 2)]
    insts = [(s, p, i) for (s, p) in pairs for i in range(2)]
    rows = {s: slice(s * ts, (s + 1) * ts) for s in range(n_streams)}
    qp, kp_b, qd, kd_b, r_pair, v_pair = {}, {}, {}, {}, {}, {}
    for (s, p) in pairs:
        cs, sn = cos_t[rows[s], :], sin_t[rows[s], :]
        q = rope(proj_ref[rows[s], _cols(OFF_RT_Q, p, LANES)], cs, sn)
        k = rope(proj_ref[rows[s], _cols(OFF_RT_K, p, LANES)], cs, sn)
        qp[s, p] = q
        qd[s, p] = q * qdec_ref[p]
        kp_b[s, p] = k.astype(BF16)
        kd_b[s, p] = (k * kdec_ref[p]).astype(BF16)
        r_pair[s, p] = get_r(s, p)
        v_pair[s, p] = proj_ref[rows[s], _cols(OFF_RT_V, p, 2 * RT_DV)].astype(BF16)
    yield
    sel = {0: low_head, 1: jnp.logical_not(low_head)}
    q_h = {(s, p, i): jnp.where(sel[i], qp[s, p], 0.0).astype(BF16) for (s, p, i) in insts}
    qd_h = {(s, p, i): jnp.where(sel[i], qd[s, p], 0.0).astype(BF16) for (s, p, i) in insts}
    sc = {(s, p, i): lax.dot_general(q_h[s, p, i], kp_b[s, p], _NT, preferred_element_type=F32)
          for (s, p, i) in insts}
    inter = {(s, p, i): jnp.dot(qd_h[s, p, i], r_pair[s, p].astype(BF16), preferred_element_type=F32)
             for (s, p, i) in insts}
    kv = {(s, p): lax.dot_general(kd_b[s, p], v_pair[s, p], _TN, preferred_element_type=F32) for (s, p) in pairs}
    yield
    for (s, p) in pairs:
        dr = jnp.where(low_rows, kv[s, p][:, :RT_DV], kv[s, p][:, RT_DV:])
        set_r(s, p, rdec_ref[p] * r_pair[s, p] + dr)
    sc_b = {(s, p, i): (sc[s, p, i] * mdec_ref[2 * p + i]).astype(BF16) for (s, p, i) in insts}
    yield
    o = {(s, p, i): jnp.dot(sc_b[s, p, i], v_pair[s, p][:, i * RT_DV:(i + 1) * RT_DV],
                            preferred_element_type=F32) + inter[s, p, i] for (s, p, i) in insts}
    yield
    mu = {k: jnp.mean(o[k], axis=-1, keepdims=True) for k in insts}
    cen = {k: o[k] - mu[k] for k in insts}
    var = {k: jnp.mean(cen[k] * cen[k], axis=-1, keepdims=True) for k in insts}
    yield
    for (s, p, i) in insts:
        h = 2 * p + i
        gate = _silu(proj_ref[rows[s], _cols(OFF_RT_GATE, h, RT_DV)].astype(BF16))
        on = cen[s, p, i] * lax.rsqrt(var[s, p, i] + EPS) * rt_g_ref[:, _cols(0, h, RT_DV)]
        mixed_ref[rows[s], _cols(HG_W, h, RT_DV)] = on.astype(BF16) * gate


def _cross_attention(n_streams, ts, proj_ref, mixed_ref, get_mem):
    xa_scale = XA_DH ** -0.5
    insts = [(s, h) for s in range(n_streams) for h in range(XA_H)]
    rows = {s: slice(s * ts, (s + 1) * ts) for s in range(n_streams)}
    mem = {k: get_mem(*k) for k in insts}
    sc = {(s, h): lax.dot_general(proj_ref[rows[s], _cols(OFF_XA_Q, h, XA_DH)].astype(BF16), mem[s, h][0], _NT,
                                  preferred_element_type=F32) * xa_scale for (s, h) in insts}
    yield
    mx = {k: jnp.max(sc[k], axis=-1, keepdims=True) for k in insts}
    yield
    pr = {k: jnp.exp(sc[k] - mx[k]) for k in insts}
    den = {k: jnp.sum(pr[k], axis=-1, keepdims=True) for k in insts}
    yield
    o = {k: jnp.dot(pr[k].astype(BF16), mem[k][1], preferred_element_type=F32) for k in insts}
    yield
    for (s, h) in insts:
        gate = _silu(proj_ref[rows[s], _cols(OFF_XA_GATE, h, XA_DH)].astype(BF16))
        mixed_ref[rows[s], _cols(HG_W + RT_W, h, XA_DH)] = (o[s, h] / den[s, h]).astype(BF16) * gate


def _chunk_cumsum(g):
    tokens, c = g.shape
    x = g.reshape(tokens // SUBLANES, SUBLANES, c)
    sub = lax.broadcasted_iota(jnp.int32, x.shape, 1)
    shift = 1
    while shift < SUBLANES:
        x = x + jnp.where(sub >= shift, pltpu.roll(x, shift, 1), 0.0)
        shift *= 2
    tiles = CHUNK // SUBLANES
    x = x.reshape(tokens // CHUNK, tiles, SUBLANES, c)
    out = [x[:, 0]]
    for j in range(1, tiles):
        total = out[-1][:, SUBLANES - 1:SUBLANES, :]
        out.append(x[:, j] + total)
    return jnp.stack(out, axis=1).reshape(tokens, c)


class _Projection:
    SEGMENTS = ((OFF_HG_F, OFF_HG_I), (OFF_HG_Q, OFF_HG_F), (OFF_HG_I, OFF_RT_Q), (OFF_RT_Q, D_IN))
    TILES_GATES = HG_W // PROJ_N_TILE
    TILES_HGRN = 4 * HG_W // PROJ_N_TILE
    TILES_ALL = D_IN // PROJ_N_TILE

    def __init__(self, x, norm_g_ref, w_in_ref, xn_ref, proj_ref):
        xn_ref[...] = _rms(x, norm_g_ref[...]).astype(BF16)
        self._refs = (xn_ref, w_in_ref, proj_ref)
        self._pending = [n0 for lo, hi in self.SEGMENTS for n0 in range(lo, hi, PROJ_N_TILE)]
        assert len(self._pending) == self.TILES_ALL
        self._done = 0

    def emit(self, n=1):
        xn_ref, w_in_ref, proj_ref = self._refs
        for _ in range(min(n, len(self._pending))):
            cols = slice(self._pending[0], self._pending.pop(0) + PROJ_N_TILE)
            proj_ref[:, cols] = jnp.dot(xn_ref[...], w_in_ref[:, cols], preferred_element_type=F32)
            self._done += 1

    def require(self, n_tiles):
        self.emit(n_tiles - self._done)


def _mix(n_streams, ts, x_ref, y_ref, cos_t, sin_t, mdec_ref, qdec_ref, kdec_ref, rdec_ref,
         get_mem, lb_ref, hg_g_ref, rt_g_ref, w_out_ref, fin_g_ref,
         proj_ref, gc_ref, mixed_ref, get_st, set_st, get_r, set_r, proj):
    proj.require(proj.TILES_GATES)
    lg = lb_ref[...]
    e = jnp.exp(lg - jnp.max(lg, axis=0, keepdims=True))
    lb = e[0:1, :] / jnp.sum(e, axis=0, keepdims=True)
    for c0 in range(0, HG_W, 2 * LANES):
        cols = slice(OFF_HG_F + c0, OFF_HG_F + c0 + 2 * LANES)
        lbc = lb[:, c0:c0 + 2 * LANES]
        half_span = 0.5 * (1.0 - lbc)
        bt = half_span * jnp.tanh(0.5 * proj_ref[:, cols])
        g = jnp.log2(lbc + (half_span + bt))
        proj_ref[:, cols] = half_span - bt
        gc_ref[:, c0:c0 + 2 * LANES] = _chunk_cumsum(g)
        proj.emit(PROJ_TILES_PER_GATE_GROUP)

    proj.require(proj.TILES_HGRN)
    row = lax.broadcasted_iota(jnp.int32, (CHUNK, LANES), 0)
    col = lax.broadcasted_iota(jnp.int32, (CHUNK, LANES), 1)
    half = CHUNK // 2
    causal = (((col <= row) & (col // half == row // half))
              | ((col >= CHUNK) & (col < CHUNK + half) & (row >= half)))
    for c in range(ts // CHUNK):
        chunks = [(s, slice(s * ts + c * CHUNK, s * ts + (c + 1) * CHUNK)) for s in range(n_streams)]
        for i in range(0, n_streams, HGRN_STREAMS_PER_STAGE):
            _hgrn_chunks(proj_ref, gc_ref, mixed_ref, hg_g_ref, chunks[i:i + HGRN_STREAMS_PER_STAGE], causal,
                         get_st, set_st, proj)

    proj.require(proj.TILES_ALL)
    _run_interleaved(
        _retention(n_streams, ts, proj_ref, mixed_ref, cos_t, sin_t, mdec_ref, qdec_ref, kdec_ref, rdec_ref,
                   rt_g_ref, get_r, set_r),
        _cross_attention(n_streams, ts, proj_ref, mixed_ref, get_mem),
        _out_projection(mixed_ref, w_out_ref, y_ref, 0, HG_W, x_ref))
    sumsq = []
    _run_interleaved(_out_projection(mixed_ref, w_out_ref, y_ref, HG_W, D_MIX, y_ref, sumsq))

    inv = lax.rsqrt(sum(sumsq) * (1.0 / D_MODEL) + EPS)
    y_ref[...] = y_ref[...] * inv * fin_g_ref[...]


def _out_projection(mixed_ref, w_out_ref, acc_ref, k_lo, k_hi, base_ref, sumsq=None):
    for n0 in range(0, D_MODEL, OUT_N_TILE):
        cols = slice(n0, n0 + OUT_N_TILE)
        out = base_ref[:, cols] + jnp.dot(mixed_ref[:, k_lo:k_hi], w_out_ref[k_lo:k_hi, cols],
                                          preferred_element_type=F32)
        acc_ref[:, cols] = out
        if sumsq is not None:
            sumsq.append(jnp.sum(out * out, axis=-1, keepdims=True))
        yield


class _StackedRows:
    def __init__(self, refs):
        self._refs = refs
        self._rows = refs[0].shape[0]

    def __getitem__(self, idx):
        cols = idx[1] if isinstance(idx, tuple) else slice(None)
        return jnp.concatenate([r[:, cols] for r in self._refs], axis=0)

    def __setitem__(self, idx, value):
        cols = idx[1] if isinstance(idx, tuple) else slice(None)
        for s, r in enumerate(self._refs):
            r[:, cols] = value[s * self._rows:(s + 1) * self._rows]


def _run_interleaved(*stage_generators):
    pending = list(stage_generators)
    while pending:
        for g in list(pending):
            try:
                next(g)
            except StopIteration:
                pending.remove(g)


def _memory_kv_kernel(mem_ref, g_ref, wk_ref, wv_ref, k_ref, v_ref, kb_ref, vb_ref):
    m = _rms(mem_ref[0], g_ref[...]).astype(BF16)
    k = jnp.dot(m, wk_ref[...].astype(BF16), preferred_element_type=F32)
    v = jnp.dot(m, wv_ref[...].astype(BF16), preferred_element_type=F32)
    for h in range(XA_H):
        k_ref[0, pl.ds(h, N_MEM, stride=XA_H), :] = k[:, _cols(0, h, XA_DH)]
        v_ref[0, pl.ds(h, N_MEM, stride=XA_H), :] = v[:, _cols(0, h, XA_DH)]
    kb_ref[0] = k.astype(BF16)
    vb_ref[0] = v.astype(BF16)


def _prompt_kernel(steps_per_stream, w_in_slab_ref, w_out_slab_ref,
                   x_ref, cos_ref, sin_ref, mdec_ref, qdec_ref, kdec_ref, rdec_ref, mk_ref, mv_ref,
                   norm_g_ref, lb_ref, hg_g_ref, rt_g_ref, fin_g_ref,
                   y_ref, s_out_ref, r_out_ref, w_in_b_slab_ref, w_out_b_slab_ref,
                   proj_ref, xn_ref, gc_ref, mixed_ref, st_ref, w_in_ref, w_out_ref):
    n_streams = x_ref.shape[0]
    t = pl.program_id(0)
    j = t - WEIGHT_PREP_STEPS
    running = t >= WEIGHT_PREP_STEPS

    @pl.when(jnp.logical_not(running))
    def _():
        for slab_ref, out_ref, full_ref in ((w_in_slab_ref, w_in_b_slab_ref, w_in_ref),
                                            (w_out_slab_ref, w_out_b_slab_ref, w_out_ref)):
            n = slab_ref.shape[0]
            slab = slab_ref[...].astype(BF16)
            full_ref[pl.ds(pl.multiple_of(t * n, n), n), :] = slab
            out_ref[...] = slab

    @pl.when(running & (j == 0))
    def _():
        st_ref[...] = jnp.zeros_like(st_ref)
        r_out_ref[...] = jnp.zeros_like(r_out_ref)

    def set_st(s, h, val):
        st_ref[s, h] = val

    def set_r(s, p, val):
        r_out_ref[s, p] = val

    def get_mem(s, h):
        return mk_ref[s, :, _cols(0, h, XA_DH)], mv_ref[s, :, _cols(0, h, XA_DH)]

    def token_block(i, carry):
        rows = pl.ds(pl.multiple_of(i * TOKEN_BLOCK, TOKEN_BLOCK), TOKEN_BLOCK)
        x_blk = _StackedRows([x_ref.at[s, rows] for s in range(n_streams)])
        y_blk = _StackedRows([y_ref.at[s, rows] for s in range(n_streams)])
        cos_t = jnp.concatenate([cos_ref[rows, :]] * n_streams, axis=0)
        sin_t = jnp.concatenate([sin_ref[rows, :]] * n_streams, axis=0)
        proj = _Projection(x_blk[...], norm_g_ref, w_in_ref, xn_ref, proj_ref)
        _mix(
            n_streams, TOKEN_BLOCK, x_blk, y_blk, cos_t, sin_t,
            mdec_ref, qdec_ref, kdec_ref, rdec_ref,
            get_mem, lb_ref, hg_g_ref, rt_g_ref, w_out_ref, fin_g_ref,
            proj_ref, gc_ref, mixed_ref,
            lambda s, h: st_ref[s, h], set_st, lambda s, p: r_out_ref[s, p], set_r, proj)
        return carry

    @pl.when(running)
    def _():
        lax.fori_loop(0, x_ref.shape[1] // TOKEN_BLOCK, token_block, 0)

    @pl.when(running & (j == steps_per_stream - 1))
    def _():
        for s in range(n_streams):
            for h in range(HG_H):
                s_out_ref[s, h] = st_ref[s, h].T


def _sample_kernel(x_ref, cos_ref, sin_ref, mdec_ref, qdec_ref, kdec_ref, rdec_ref, mk_ref, mv_ref,
                   s_in_ref, r_in_ref,
                   norm_g_ref, w_in_ref, lb_ref, hg_g_ref, rt_g_ref, w_out_ref, fin_g_ref,
                   y_ref, s_out_ref, r_out_ref, proj_ref, xn_ref, gc_ref, mixed_ref):
    n_streams = TOKEN_BLOCK // CHUNK

    def set_st(s, h, val):
        s_out_ref[s, h] = val.T

    def set_r(s, p, val):
        r_out_ref[s, p] = val

    def get_mem(s, h):
        head_rows = pl.ds(h, N_MEM, stride=XA_H)
        return mk_ref[s, head_rows, :].astype(BF16), mv_ref[s, head_rows, :].astype(BF16)

    cos_t = jnp.concatenate([cos_ref[...]] * n_streams, axis=0)
    sin_t = jnp.concatenate([sin_ref[...]] * n_streams, axis=0)
    proj = _Projection(x_ref[...], norm_g_ref, w_in_ref, xn_ref, proj_ref)
    _mix(
        n_streams, CHUNK, x_ref, y_ref, cos_t, sin_t, mdec_ref, qdec_ref, kdec_ref, rdec_ref,
        get_mem, lb_ref, hg_g_ref, rt_g_ref, w_out_ref, fin_g_ref,
        proj_ref, gc_ref, mixed_ref,
        lambda s, h: s_in_ref[s, h].T, set_st, lambda s, p: r_in_ref[s, p], set_r, proj)


def _const_spec(shape):
    zeros = (0,) * len(shape)
    return pl.BlockSpec(shape, lambda *_: zeros, pipeline_mode=pl.Buffered(1))


def _layer_scratch(rows):
    return [pltpu.VMEM((rows, D_IN), F32),
            pltpu.VMEM((rows, D_MODEL), BF16),
            pltpu.VMEM((rows, HG_W), F32),
            pltpu.VMEM((rows, D_MIX), BF16)]


def _memory_kv(mem, g, wk, wv):
    b = mem.shape[0]
    out_f = jax.ShapeDtypeStruct((b, N_MEM * XA_H, XA_DH), F32)
    out_b = jax.ShapeDtypeStruct((b, N_MEM, XA_W), BF16)
    blk_f = pl.BlockSpec((1, N_MEM * XA_H, XA_DH), lambda i: (i, 0, 0))
    blk_b = pl.BlockSpec((1, N_MEM, XA_W), lambda i: (i, 0, 0))
    return pl.pallas_call(
        _memory_kv_kernel,
        grid=(b,),
        in_specs=[pl.BlockSpec((1, N_MEM, D_MODEL), lambda i: (i, 0, 0)),
                  pl.BlockSpec((1, D_MODEL), lambda i: (0, 0)),
                  pl.BlockSpec((D_MODEL, XA_W), lambda i: (0, 0)),
                  pl.BlockSpec((D_MODEL, XA_W), lambda i: (0, 0))],
        out_specs=[blk_f, blk_f, blk_b, blk_b],
        out_shape=[out_f, out_f, out_b, out_b],
        name="memory_kv",
    )(mem, g.reshape(1, D_MODEL), wk, wv)


def _prompt_layer(x, mk_b, mv_b, w_in, w_out, params):
    b, t, _ = x.shape
    tb = PROMPT_BLOCKS_PER_STEP * TOKEN_BLOCK
    spp = t // tb
    prep = WEIGHT_PREP_STEPS
    cos_t, sin_t = _rope_tables(np.arange(t))
    mdec, qdec, kdec, rdec = _retention_tables(TOKEN_BLOCK)
    consts = (mdec, qdec, kdec, rdec)
    in_rows, out_rows = D_MODEL // prep, D_MIX // prep

    def slab(step):
        return jnp.minimum(step, prep - 1)

    def run(step):
        return jnp.maximum(step - prep, 0)

    in_specs = ([pl.BlockSpec((in_rows, D_IN), lambda s: (slab(s), 0)),
                 pl.BlockSpec((out_rows, D_MODEL), lambda s: (slab(s), 0)),
                 pl.BlockSpec((b, tb, D_MODEL), lambda s: (0, run(s), 0)),
                 pl.BlockSpec((tb, LANES), lambda s: (run(s), 0)),
                 pl.BlockSpec((tb, LANES), lambda s: (run(s), 0))]
                + [_const_spec(c.shape) for c in consts]
                + [_const_spec(mk_b.shape), _const_spec(mv_b.shape)]
                + [_const_spec(p.shape) for p in params])
    out_shape = [jax.ShapeDtypeStruct((b, t, D_MODEL), F32),
                 jax.ShapeDtypeStruct((b, HG_H, HG_DK, HG_DV), F32),
                 jax.ShapeDtypeStruct((b, RT_H // 2, 2 * RT_DK, RT_DV), F32),
                 jax.ShapeDtypeStruct((D_MODEL, D_IN), BF16),
                 jax.ShapeDtypeStruct((D_MIX, D_MODEL), BF16)]
    out_specs = [pl.BlockSpec((b, tb, D_MODEL), lambda s: (0, run(s), 0)),
                 pl.BlockSpec((b, HG_H, HG_DK, HG_DV), lambda s: (0, 0, 0, 0)),
                 pl.BlockSpec((b, RT_H // 2, 2 * RT_DK, RT_DV), lambda s: (0, 0, 0, 0)),
                 pl.BlockSpec((in_rows, D_IN), lambda s: (slab(s), 0)),
                 pl.BlockSpec((out_rows, D_MODEL), lambda s: (slab(s), 0))]
    return pl.pallas_call(
        functools.partial(_prompt_kernel, spp),
        grid=(prep + spp,),
        in_specs=in_specs,
        out_specs=out_specs,
        out_shape=out_shape,
        scratch_shapes=_layer_scratch(b * TOKEN_BLOCK) + [
                                           pltpu.VMEM((b, HG_H, HG_DV, HG_DK), F32),
                                           pltpu.VMEM((D_MODEL, D_IN), BF16),
                                           pltpu.VMEM((D_MIX, D_MODEL), BF16)],
        compiler_params=pltpu.CompilerParams(dimension_semantics=("arbitrary",),
                                             vmem_limit_bytes=VMEM_LIMIT_BYTES),
        name="prompt_layer",
    )(w_in, w_out, x, cos_t, sin_t, *consts, mk_b, mv_b, *params)


def _sample_layer(x, s_hg, s_rt, mem_k, mem_v, weights):
    b, t, _ = x.shape
    assert t == CHUNK
    tb = TOKEN_BLOCK
    nsb = tb // t
    cos_t, sin_t = _rope_tables(PAST_LEN + np.arange(t))
    mdec, qdec, kdec, rdec = _retention_tables(t)
    consts = (mdec, qdec, kdec, rdec)
    s_rt2 = s_rt.reshape(b, RT_H // 2, 2 * RT_DK, RT_DV)
    mem_spec = pl.BlockSpec((nsb, N_MEM * XA_H, XA_DH), lambda i: (i, 0, 0))
    in_specs = ([pl.BlockSpec((tb, D_MODEL), lambda i: (i, 0)),
                 _const_spec(cos_t.shape), _const_spec(sin_t.shape)]
                + [_const_spec(c.shape) for c in consts]
                + [mem_spec, mem_spec,
                   pl.BlockSpec((nsb, HG_H, HG_DK, HG_DV), lambda i: (i, 0, 0, 0)),
                   pl.BlockSpec((nsb, RT_H // 2, 2 * RT_DK, RT_DV), lambda i: (i, 0, 0, 0))]
                + [_const_spec(w.shape) for w in weights])
    out_shape = [jax.ShapeDtypeStruct((b * t, D_MODEL), F32),
                 jax.ShapeDtypeStruct((b, HG_H, HG_DK, HG_DV), F32),
                 jax.ShapeDtypeStruct((b, RT_H // 2, 2 * RT_DK, RT_DV), F32)]
    out_specs = [pl.BlockSpec((tb, D_MODEL), lambda i: (i, 0)),
                 pl.BlockSpec((nsb, HG_H, HG_DK, HG_DV), lambda i: (i, 0, 0, 0)),
                 pl.BlockSpec((nsb, RT_H // 2, 2 * RT_DK, RT_DV), lambda i: (i, 0, 0, 0))]
    y, s_new, r_new = pl.pallas_call(
        _sample_kernel,
        grid=(b // nsb,),
        in_specs=in_specs,
        out_specs=out_specs,
        out_shape=out_shape,
        scratch_shapes=_layer_scratch(tb),
        compiler_params=pltpu.CompilerParams(dimension_semantics=("arbitrary",),
                                             vmem_limit_bytes=VMEM_LIMIT_BYTES),
        name="sample_layer",
    )(x.reshape(b * t, D_MODEL), cos_t, sin_t, *consts,
      mem_k.reshape(b, N_MEM * XA_H, XA_DH), mem_v.reshape(b, N_MEM * XA_H, XA_DH), s_hg, s_rt2, *weights)
    return y.reshape(b, t, D_MODEL), s_new, r_new


def kernel(x_prompt, x_sample, mem_prompt, state_hgrn, state_ret, cache_mem_k, cache_mem_v, norm_g, w_in,
           lb_logits, hg_norm_g, rt_norm_g, mem_norm_g, w_mem_k, w_mem_v, w_out, final_norm_g):
    depth = w_in.shape[0]
    assert depth == 1, "single-layer step"
    bp = x_prompt.shape[0]
    bs = x_sample.shape[0]
    g_in, g_hg, g_rt = norm_g[0].reshape(1, D_MODEL), hg_norm_g[0].reshape(1, HG_W), rt_norm_g[0].reshape(1, RT_W)
    g_fin = final_norm_g.reshape(1, D_MODEL)
    mk, mv, mk_b, mv_b = _memory_kv(mem_prompt, mem_norm_g[0], w_mem_k[0], w_mem_v[0])
    y_p, s_p, r_p, w_in_b, w_out_b = _prompt_layer(x_prompt, mk_b, mv_b, w_in[0], w_out[0],
                                                   (g_in, lb_logits, g_hg, g_rt, g_fin))
    y_s, s_s, r_s = _sample_layer(x_sample, state_hgrn[0], state_ret[0], cache_mem_k[0], cache_mem_v[0],
                                  (g_in, w_in_b, lb_logits, g_hg, g_rt, w_out_b, g_fin))
    return (y_p, y_s,
            s_p.reshape(1, bp, HG_H, HG_DK, HG_DV),
            r_p.reshape(1, bp, RT_H, RT_DK, RT_DV),
            mk.reshape(1, bp, N_MEM, XA_H, XA_DH),
            mv.reshape(1, bp, N_MEM, XA_H, XA_DH),
            s_s.reshape(1, bs, HG_H, HG_DK, HG_DV),
            r_s.reshape(1, bs, RT_H, RT_DK, RT_DV))
```

```python
import functools

import numpy as np
import jax
import jax.numpy as jnp
from jax import lax
from jax.experimental import pallas as pl
from jax.experimental.pallas import tpu as pltpu

F32 = jnp.float32
BF16 = jnp.bfloat16

D_MODEL = 1024
PAST_LEN = 4096
CHUNK = 64
N_MEM = 256
EPS = 1e-6
ROPE_BASE = 10000.0
HG_H, HG_DK, HG_DV = 8, 128, 128
HG_W = HG_H * HG_DV
RT_H, RT_DK, RT_DV = 4, 64, 128
RT_QK = RT_H * RT_DK
RT_W = RT_H * RT_DV
XA_H, XA_DH = 4, 128
XA_W = XA_H * XA_DH
D_MIX = HG_W + RT_W + XA_W
OFF_HG_Q = 0
OFF_HG_F = OFF_HG_Q + HG_W
OFF_HG_I = OFF_HG_F + HG_W
OFF_HG_GATE = OFF_HG_I + HG_W
OFF_RT_Q = OFF_HG_GATE + HG_W
OFF_RT_K = OFF_RT_Q + RT_QK
OFF_RT_V = OFF_RT_K + RT_QK
OFF_RT_GATE = OFF_RT_V + RT_W
OFF_XA_Q = OFF_RT_GATE + RT_W
OFF_XA_GATE = OFF_XA_Q + XA_W
D_IN = OFF_XA_GATE + XA_W

LANES = 128
SUBLANES = 8
TOKEN_BLOCK = 256
WEIGHT_PREP_STEPS = 16
PROJ_N_TILE = 512
PROJ_TILES_PER_GATE_GROUP = 1
PROJ_TILES_PER_HGRN_STAGE = 1
HGRN_STREAMS_PER_STAGE = 4
PROMPT_HGRN_HEAD_GROUPS = 1
OUT_N_TILE = 512
VMEM_LIMIT_BYTES = 60 * 1024 * 1024
_NT = (((1,), (1,)), ((), ()))
_TN = (((0,), (0,)), ((), ()))


def _silu(x):
    h = 0.5 * x
    return h + h * jnp.tanh(h)


def _rms(x, g):
    return x * lax.rsqrt(jnp.mean(x * x, axis=-1, keepdims=True) + EPS) * g


def _cols(off, i, width):
    return slice(off + i * width, off + (i + 1) * width)


def _rope_tables(pos):
    half = RT_DK // 2
    inv_freq = ROPE_BASE ** (-np.arange(half, dtype=np.float64) / half)
    ang = np.asarray(pos, np.float64)[:, None] * inv_freq[None, :]
    cos, sin = np.cos(ang), np.sin(ang)
    cos_t = np.concatenate([cos, cos, cos, cos], axis=-1)
    sin_t = np.concatenate([-sin, sin, -sin, sin], axis=-1)
    return jnp.asarray(cos_t, F32), jnp.asarray(sin_t, F32)


def _retention_tables(ts):
    h = np.arange(RT_H, dtype=np.float64)
    log_gamma = np.log1p(-np.exp2(-5.0 - h))
    n = np.arange(ts, dtype=np.float64)
    scale = RT_DK ** -0.5
    dist = np.abs(n[:, None] - n[None, :])
    visible = (n[None, :] // CHUNK) <= (n[:, None] // CHUNK)
    mdec = scale * np.exp(log_gamma[:, None, None] * dist[None]) * visible[None]
    qd = np.exp(log_gamma[:, None] * (n[None, :] + 1.0))
    kd = scale * np.exp(log_gamma[:, None] * (ts - 1.0 - n[None, :]))
    rd = np.exp(log_gamma * ts)

    def pair_lanes(a):
        a = np.repeat(a[:, :, None], RT_DK, axis=2)
        return np.concatenate([a[0::2], a[1::2]], axis=2)

    rdec = np.repeat(np.repeat(rd[:, None], RT_DK, axis=1).reshape(RT_H // 2, 2 * RT_DK, 1), LANES, axis=2)
    return (jnp.asarray(mdec, F32), jnp.asarray(pair_lanes(qd), F32), jnp.asarray(pair_lanes(kd), F32),
            jnp.asarray(rdec, F32))


def _hgrn_chunks(proj_ref, gc_ref, mixed_ref, hg_g_ref, chunks, heads, score_mask, get_st, set_st, proj):
    half = CHUNK // 2
    zero_half = jnp.zeros((half, HG_DK), BF16)
    insts = [(s, h) for s, _ in chunks for h in heads]
    rows = dict(chunks)
    q_rel, k_rel, q_abs, k_end, v_b, dec = {}, {}, {}, {}, {}, {}
    for (s, h) in insts:
        q = _silu(proj_ref[rows[s], _cols(OFF_HG_Q, h, HG_DK)])
        k = proj_ref[rows[s], _cols(OFF_HG_F, h, HG_DK)]
        gc = gc_ref[rows[s], _cols(0, h, HG_DK)]
        g_end = gc[CHUNK - 1:CHUNK, :]
        (q1, q2), (k1, k2), (g1, g2) = ((a[:half], a[half:]) for a in (q, k, gc))
        g_m1, g_m2 = gc[half // 2 - 1:half // 2, :], gc[half + half // 2 - 1:half + half // 2, :]
        g_b = gc[half - 1:half, :]
        qd1, kd1 = (q1 * jnp.exp2(g1 - g_m1)).astype(BF16), (k1 * jnp.exp2(g_m1 - g1)).astype(BF16)
        qd2, kd2 = (q2 * jnp.exp2(g2 - g_m2)).astype(BF16), (k2 * jnp.exp2(g_m2 - g2)).astype(BF16)
        qo, ko = (q2 * jnp.exp2(g2 - g_b)).astype(BF16), (k1 * jnp.exp2(g_b - g1)).astype(BF16)
        q_rel[s, h] = jnp.concatenate([jnp.concatenate([qd1, qd2], axis=0),
                                       jnp.concatenate([zero_half, qo], axis=0)], axis=1)
        k_rel[s, h] = jnp.concatenate([jnp.concatenate([kd1, kd2, zero_half, zero_half], axis=0),
                                       jnp.concatenate([zero_half, zero_half, ko, zero_half], axis=0)],
                                      axis=1)
        q_abs[s, h] = (q * jnp.exp2(gc)).astype(BF16)
        k_end[s, h] = (k * jnp.exp2(g_end - gc)).astype(BF16)
        v_b[s, h] = proj_ref[rows[s], _cols(OFF_HG_I, h, HG_DV)].astype(BF16)
        dec[s, h] = jnp.exp2(g_end)
    proj.emit(PROJ_TILES_PER_HGRN_STAGE)
    a = {k: lax.dot_general(q_rel[k], k_rel[k], _NT, preferred_element_type=F32) for k in insts}
    st = {k: get_st(*k) for k in insts}
    v_t = {k: jnp.concatenate([v_b[k], v_b[k][:half], zero_half], axis=0).T for k in insts}
    kv = {k: jnp.dot(v_t[k][:, :CHUNK], k_end[k], preferred_element_type=F32) for k in insts}
    for k in insts:
        set_st(*k, dec[k] * st[k] + kv[k])
    a_m = {k: jnp.where(score_mask, a[k], 0.0).astype(BF16) for k in insts}
    o = {k: lax.dot_general(jnp.concatenate([q_abs[k], a_m[k]], axis=1),
                            jnp.concatenate([st[k].astype(BF16), v_t[k]], axis=1), _NT, preferred_element_type=F32)
         for k in insts}
    ms = {k: jnp.mean(o[k] * o[k], axis=-1, keepdims=True) for k in insts}
    for (s, h) in insts:
        gate = _silu(proj_ref[rows[s], _cols(OFF_HG_GATE, h, HG_DV)].astype(BF16))
        on = o[s, h] * lax.rsqrt(ms[s, h] + EPS) * hg_g_ref[:, _cols(0, h, HG_DV)]
        mixed_ref[rows[s], _cols(0, h, HG_DV)] = on.astype(BF16) * gate


def _retention(n_streams, ts, proj_ref, mixed_ref, cos_t, sin_t, mdec_ref, qdec_ref, kdec_ref, rdec_ref,
               rt_g_ref, get_r, set_r):
    lane = lax.broadcasted_iota(jnp.int32, (ts, LANES), 1)
    first_half = (lane % RT_DK) < (RT_DK // 2)
    low_head = lane < RT_DK
    low_rows = lax.broadcasted_iota(jnp.int32, (2 * RT_DK, RT_DV), 0) < RT_DK

    def rope(xs, cs, sn):
        swapped = jnp.where(first_half, pltpu.roll(xs, LANES - RT_DK // 2, 1), pltpu.roll(xs, RT_DK // 2, 1))
        return xs * cs + swapped * sn

    pairs = [(s, p) for s in range(n_streams) for p in range(RT_H // 2)]
    insts = [(s, p, i) for (s, p) in pairs for i in range(2)]
    rows = {s: slice(s * ts, (s + 1) * ts) for s in range(n_streams)}
    qp, kp_b, qd, kd_b, r_pair, v_pair = {}, {}, {}, {}, {}, {}
    for (s, p) in pairs:
        cs, sn = cos_t[rows[s], :], sin_t[rows[s], :]
        q = rope(proj_ref[rows[s], _cols(OFF_RT_Q, p, LANES)], cs, sn)
        k = rope(proj_ref[rows[s], _cols(OFF_RT_K, p, LANES)], cs, sn)
        qp[s, p] = q
        qd[s, p] = q * qdec_ref[p]
        kp_b[s, p] = k.astype(BF16)
        kd_b[s, p] = (k * kdec_ref[p]).astype(BF16)
        r_pair[s, p] = get_r(s, p)
        v_pair[s, p] = proj_ref[rows[s], _cols(OFF_RT_V, p, 2 * RT_DV)].astype(BF16)
    yield
    sel = {0: low_head, 1: jnp.logical_not(low_head)}
    q_h = {(s, p, i): jnp.where(sel[i], qp[s, p], 0.0).astype(BF16) for (s, p, i) in insts}
    qd_h = {(s, p, i): jnp.where(sel[i], qd[s, p], 0.0).astype(BF16) for (s, p, i) in insts}
    sc = {(s, p, i): lax.dot_general(q_h[s, p, i], kp_b[s, p], _NT, preferred_element_type=F32)
          for (s, p, i) in insts}
    inter = {(s, p, i): jnp.dot(qd_h[s, p, i], r_pair[s, p].astype(BF16), preferred_element_type=F32)
             for (s, p, i) in insts}
    kv = {(s, p): lax.dot_general(kd_b[s, p], v_pair[s, p], _TN, preferred_element_type=F32) for (s, p) in pairs}
    yield
    for (s, p) in pairs:
        dr = jnp.where(low_rows, kv[s, p][:, :RT_DV], kv[s, p][:, RT_DV:])
        set_r(s, p, rdec_ref[p] * r_pair[s, p] + dr)
    sc_b = {(s, p, i): (sc[s, p, i] * mdec_ref[2 * p + i]).astype(BF16) for (s, p, i) in insts}
    yield
    o = {(s, p, i): jnp.dot(sc_b[s, p, i], v_pair[s, p][:, i * RT_DV:(i + 1) * RT_DV],
                            preferred_element_type=F32) + inter[s, p, i] for (s, p, i) in insts}
    yield
    mu = {k: jnp.mean(o[k], axis=-1, keepdims=True) for k in insts}
    cen = {k: o[k] - mu[k] for k in insts}
    var = {k: jnp.mean(cen[k] * cen[k], axis=-1, keepdims=True) for k in insts}
    yield
    for (s, p, i) in insts:
        h = 2 * p + i
        gate = _silu(proj_ref[rows[s], _cols(OFF_RT_GATE, h, RT_DV)].astype(BF16))
        on = cen[s, p, i] * lax.rsqrt(var[s, p, i] + EPS) * rt_g_ref[:, _cols(0, h, RT_DV)]
        mixed_ref[rows[s], _cols(HG_W, h, RT_DV)] = on.astype(BF16) * gate


def _cross_attention(n_streams, ts, proj_ref, mixed_ref, get_mem):
    xa_scale = XA_DH ** -0.5
    insts = [(s, h) for s in range(n_streams) for h in range(XA_H)]
    rows = {s: slice(s * ts, (s + 1) * ts) for s in range(n_streams)}
    mem = {k: get_mem(*k) for k in insts}
    sc = {(s, h): lax.dot_general(proj_ref[rows[s], _cols(OFF_XA_Q, h, XA_DH)].astype(BF16), mem[s, h][0], _NT,
                                  preferred_element_type=F32) * xa_scale for (s, h) in insts}
    yield
    mx = {k: jnp.max(sc[k], axis=-1, keepdims=True) for k in insts}
    yield
    pr = {k: jnp.exp(sc[k] - mx[k]) for k in insts}
    den = {k: jnp.sum(pr[k], axis=-1, keepdims=True) for k in insts}
    yield
    o = {k: jnp.dot(pr[k].astype(BF16), mem[k][1], preferred_element_type=F32) for k in insts}
    yield
    for (s, h) in insts:
        gate = _silu(proj_ref[rows[s], _cols(OFF_XA_GATE, h, XA_DH)].astype(BF16))
        mixed_ref[rows[s], _cols(HG_W + RT_W, h, XA_DH)] = (o[s, h] / den[s, h]).astype(BF16) * gate


def _chunk_cumsum(g):
    tokens, c = g.shape
    x = g.reshape(tokens // SUBLANES, SUBLANES, c)
    sub = lax.broadcasted_iota(jnp.int32, x.shape, 1)
    shift = 1
    while shift < SUBLANES:
        x = x + jnp.where(sub >= shift, pltpu.roll(x, shift, 1), 0.0)
        shift *= 2
    tiles = CHUNK // SUBLANES
    x = x.reshape(tokens // CHUNK, tiles, SUBLANES, c)
    out = [x[:, 0]]
    for j in range(1, tiles):
        total = out[-1][:, SUBLANES - 1:SUBLANES, :]
        out.append(x[:, j] + total)
    return jnp.stack(out, axis=1).reshape(tokens, c)


class _Projection:
    TILES_GATES = HG_W // PROJ_N_TILE
    TILES_ALL = D_IN // PROJ_N_TILE

    def __init__(self, x, norm_g_ref, w_in_ref, xn_ref, proj_ref, done=0, head_groups=1):
        if x is not None:
            xn_ref[...] = _rms(x, norm_g_ref[...]).astype(BF16)
        self._refs = (xn_ref, w_in_ref, proj_ref)
        group_w = HG_W // head_groups
        assert group_w % PROJ_N_TILE == 0
        segments = ([(OFF_HG_F, OFF_HG_F + HG_W)]
                    + [(off + g * group_w, off + (g + 1) * group_w)
                       for g in range(head_groups) for off in (OFF_HG_Q, OFF_HG_I, OFF_HG_GATE)]
                    + [(OFF_RT_Q, D_IN)])
        self._group_tiles = 3 * group_w // PROJ_N_TILE
        self._pending = [n0 for lo, hi in segments for n0 in range(lo, hi, PROJ_N_TILE)]
        assert len(self._pending) == self.TILES_ALL
        del self._pending[:done]
        self._done = done

    def tiles_for_head_group(self, g):
        return self.TILES_GATES + (g + 1) * self._group_tiles

    def emit(self, n=1):
        xn_ref, w_in_ref, proj_ref = self._refs
        for _ in range(min(n, len(self._pending))):
            cols = slice(self._pending[0], self._pending.pop(0) + PROJ_N_TILE)
            proj_ref[:, cols] = jnp.dot(xn_ref[...], w_in_ref[:, cols], preferred_element_type=F32)
            self._done += 1

    def require(self, n_tiles):
        self.emit(n_tiles - self._done)


def _mix(n_streams, ts, x_ref, y_ref, cos_t, sin_t, mdec_ref, qdec_ref, kdec_ref, rdec_ref,
         get_mem, lb_ref, hg_g_ref, rt_g_ref, w_out_ref, fin_g_ref,
         proj_ref, gc_ref, mixed_ref, get_st, set_st, get_r, set_r, proj,
         late_stages=(), fgate_ref=None, hgrn_head_groups=1):
    proj.require(proj.TILES_GATES)
    lg = lb_ref[...]
    e = jnp.exp(lg - jnp.max(lg, axis=0, keepdims=True))
    lb = e[0:1, :] / jnp.sum(e, axis=0, keepdims=True)
    for c0 in range(0, HG_W, 2 * LANES):
        cols = slice(OFF_HG_F + c0, OFF_HG_F + c0 + 2 * LANES)
        lbc = lb[:, c0:c0 + 2 * LANES]
        half_span = 0.5 * (1.0 - lbc)
        pre = proj_ref[:, cols] if fgate_ref is None else fgate_ref[:, c0:c0 + 2 * LANES]
        bt = half_span * jnp.tanh(0.5 * pre)
        g = jnp.log2(lbc + (half_span + bt))
        proj_ref[:, cols] = half_span - bt
        gc_ref[:, c0:c0 + 2 * LANES] = _chunk_cumsum(g)
        proj.emit(PROJ_TILES_PER_GATE_GROUP)

    row = lax.broadcasted_iota(jnp.int32, (CHUNK, LANES), 0)
    col = lax.broadcasted_iota(jnp.int32, (CHUNK, LANES), 1)
    half = CHUNK // 2
    causal = (((col <= row) & (col // half == row // half))
              | ((col >= CHUNK) & (col < CHUNK + half) & (row >= half)))
    group_heads = HG_H // hgrn_head_groups
    for g in range(hgrn_head_groups):
        proj.require(proj.tiles_for_head_group(g))
        heads = range(g * group_heads, (g + 1) * group_heads)
        for c in range(ts // CHUNK):
            chunks = [(s, slice(s * ts + c * CHUNK, s * ts + (c + 1) * CHUNK)) for s in range(n_streams)]
            for i in range(0, n_streams, HGRN_STREAMS_PER_STAGE):
                _hgrn_chunks(proj_ref, gc_ref, mixed_ref, hg_g_ref, chunks[i:i + HGRN_STREAMS_PER_STAGE], heads,
                             causal, get_st, set_st, proj)

    proj.require(proj.TILES_ALL)
    _run_interleaved(
        _retention(n_streams, ts, proj_ref, mixed_ref, cos_t, sin_t, mdec_ref, qdec_ref, kdec_ref, rdec_ref,
                   rt_g_ref, get_r, set_r),
        _cross_attention(n_streams, ts, proj_ref, mixed_ref, get_mem),
        _out_projection(mixed_ref, w_out_ref, y_ref, 0, HG_W, x_ref))
    sumsq = []
    _run_interleaved(_out_projection(mixed_ref, w_out_ref, y_ref, HG_W, D_MIX, y_ref, sumsq), *late_stages)

    inv = lax.rsqrt(sum(sumsq) * (1.0 / D_MODEL) + EPS)
    y_ref[...] = y_ref[...] * inv * fin_g_ref[...]


def _out_projection(mixed_ref, w_out_ref, acc_ref, k_lo, k_hi, base_ref, sumsq=None):
    for n0 in range(0, D_MODEL, OUT_N_TILE):
        cols = slice(n0, n0 + OUT_N_TILE)
        out = base_ref[:, cols] + jnp.dot(mixed_ref[:, k_lo:k_hi], w_out_ref[k_lo:k_hi, cols],
                                          preferred_element_type=F32)
        acc_ref[:, cols] = out
        if sumsq is not None:
            sumsq.append(jnp.sum(out * out, axis=-1, keepdims=True))
        yield


class _StackedRows:
    def __init__(self, refs):
        self._refs = refs
        self._rows = refs[0].shape[0]

    def __getitem__(self, idx):
        cols = idx[1] if isinstance(idx, tuple) else slice(None)
        return jnp.concatenate([r[:, cols] for r in self._refs], axis=0)

    def __setitem__(self, idx, value):
        cols = idx[1] if isinstance(idx, tuple) else slice(None)
        for s, r in enumerate(self._refs):
            r[:, cols] = value[s * self._rows:(s + 1) * self._rows]


def _run_interleaved(*stage_generators):
    pending = list(stage_generators)
    while pending:
        for g in list(pending):
            try:
                next(g)
            except StopIteration:
                pending.remove(g)


def _memory_kv_kernel(mem_ref, g_ref, wk_ref, wv_ref, k_ref, v_ref, kb_ref, vb_ref):
    m = _rms(mem_ref[0], g_ref[...]).astype(BF16)
    k = jnp.dot(m, wk_ref[...].astype(BF16), preferred_element_type=F32)
    v = jnp.dot(m, wv_ref[...].astype(BF16), preferred_element_type=F32)
    for h in range(XA_H):
        k_ref[0, pl.ds(h, N_MEM, stride=XA_H), :] = k[:, _cols(0, h, XA_DH)]
        v_ref[0, pl.ds(h, N_MEM, stride=XA_H), :] = v[:, _cols(0, h, XA_DH)]
    kb_ref[0] = k.astype(BF16)
    vb_ref[0] = v.astype(BF16)


def _prompt_kernel(steps_per_stream, w_in_slab_ref, w_out_slab_ref,
                   x_ref, x_next_ref, cos_ref, sin_ref, mdec_ref, qdec_ref, kdec_ref, rdec_ref, mk_ref, mv_ref,
                   norm_g_ref, lb_ref, hg_g_ref, rt_g_ref, fin_g_ref,
                   y_ref, s_out_ref, r_out_ref, w_in_b_slab_ref, w_out_b_slab_ref,
                   proj_ref, xn_ref, gc_ref, mixed_ref, st_ref, w_in_ref, w_out_ref, fgate_ref):
    n_streams = x_ref.shape[0]
    t = pl.program_id(0)
    j = t - WEIGHT_PREP_STEPS
    running = t >= WEIGHT_PREP_STEPS
    ahead = _Projection.TILES_GATES

    def start_block(src_ref):
        for s in range(n_streams):
            xn_ref[s * TOKEN_BLOCK:(s + 1) * TOKEN_BLOCK, :] = _rms(src_ref[s], norm_g_ref[...]).astype(BF16)
            yield
        for n0 in range(0, HG_W, PROJ_N_TILE):
            fgate_ref[:, n0:n0 + PROJ_N_TILE] = jnp.dot(
                xn_ref[...], w_in_ref[:, OFF_HG_F + n0:OFF_HG_F + n0 + PROJ_N_TILE], preferred_element_type=F32)
            yield

    @pl.when(jnp.logical_not(running))
    def _():
        for slab_ref, out_ref, full_ref in ((w_in_slab_ref, w_in_b_slab_ref, w_in_ref),
                                            (w_out_slab_ref, w_out_b_slab_ref, w_out_ref)):
            n = slab_ref.shape[0]
            slab = slab_ref[...].astype(BF16)
            full_ref[pl.ds(pl.multiple_of(t * n, n), n), :] = slab
            out_ref[...] = slab

    @pl.when(t == WEIGHT_PREP_STEPS - 1)
    def _():
        _run_interleaved(start_block(x_ref))
        st_ref[...] = jnp.zeros_like(st_ref)
        r_out_ref[...] = jnp.zeros_like(r_out_ref)

    def set_st(s, h, val):
        st_ref[s, h] = val

    def set_r(s, p, val):
        r_out_ref[s, p] = val

    def get_mem(s, h):
        return mk_ref[s, :, _cols(0, h, XA_DH)], mv_ref[s, :, _cols(0, h, XA_DH)]

    @pl.when(running)
    def _():
        cos_t = jnp.concatenate([cos_ref[...]] * n_streams, axis=0)
        sin_t = jnp.concatenate([sin_ref[...]] * n_streams, axis=0)
        x_blk = _StackedRows([x_ref.at[s] for s in range(n_streams)])
        y_blk = _StackedRows([y_ref.at[s] for s in range(n_streams)])
        proj = _Projection(None, norm_g_ref, w_in_ref, xn_ref, proj_ref, done=ahead,
                           head_groups=PROMPT_HGRN_HEAD_GROUPS)
        _mix(
            n_streams, TOKEN_BLOCK, x_blk, y_blk, cos_t, sin_t,
            mdec_ref, qdec_ref, kdec_ref, rdec_ref,
            get_mem, lb_ref, hg_g_ref, rt_g_ref, w_out_ref, fin_g_ref,
            proj_ref, gc_ref, mixed_ref,
            lambda s, h: st_ref[s, h], set_st, lambda s, p: r_out_ref[s, p], set_r, proj,
            late_stages=(start_block(x_next_ref),), fgate_ref=fgate_ref,
            hgrn_head_groups=PROMPT_HGRN_HEAD_GROUPS)

    @pl.when(j == steps_per_stream - 1)
    def _():
        for s in range(n_streams):
            for h in range(HG_H):
                s_out_ref[s, h] = st_ref[s, h].T


def _sample_kernel(x_ref, cos_ref, sin_ref, mdec_ref, qdec_ref, kdec_ref, rdec_ref, mk_ref, mv_ref,
                   s_in_ref, r_in_ref,
                   norm_g_ref, w_in_ref, lb_ref, hg_g_ref, rt_g_ref, w_out_ref, fin_g_ref,
                   y_ref, s_out_ref, r_out_ref, proj_ref, xn_ref, gc_ref, mixed_ref):
    n_streams = TOKEN_BLOCK // CHUNK

    def set_st(s, h, val):
        s_out_ref[s, h] = val.T

    def set_r(s, p, val):
        r_out_ref[s, p] = val

    def get_mem(s, h):
        head_rows = pl.ds(h, N_MEM, stride=XA_H)
        return mk_ref[s, head_rows, :].astype(BF16), mv_ref[s, head_rows, :].astype(BF16)

    cos_t = jnp.concatenate([cos_ref[...]] * n_streams, axis=0)
    sin_t = jnp.concatenate([sin_ref[...]] * n_streams, axis=0)
    proj = _Projection(x_ref[...], norm_g_ref, w_in_ref, xn_ref, proj_ref)
    _mix(
        n_streams, CHUNK, x_ref, y_ref, cos_t, sin_t, mdec_ref, qdec_ref, kdec_ref, rdec_ref,
        get_mem, lb_ref, hg_g_ref, rt_g_ref, w_out_ref, fin_g_ref,
        proj_ref, gc_ref, mixed_ref,
        lambda s, h: s_in_ref[s, h].T, set_st, lambda s, p: r_in_ref[s, p], set_r, proj)


def _const_spec(shape):
    zeros = (0,) * len(shape)
    return pl.BlockSpec(shape, lambda *_: zeros, pipeline_mode=pl.Buffered(1))


def _layer_scratch(rows):
    return [pltpu.VMEM((rows, D_IN), F32),
            pltpu.VMEM((rows, D_MODEL), BF16),
            pltpu.VMEM((rows, HG_W), F32),
            pltpu.VMEM((rows, D_MIX), BF16)]


def _memory_kv(mem, g, wk, wv):
    b = mem.shape[0]
    out_f = jax.ShapeDtypeStruct((b, N_MEM * XA_H, XA_DH), F32)
    out_b = jax.ShapeDtypeStruct((b, N_MEM, XA_W), BF16)
    blk_f = pl.BlockSpec((1, N_MEM * XA_H, XA_DH), lambda i: (i, 0, 0))
    blk_b = pl.BlockSpec((1, N_MEM, XA_W), lambda i: (i, 0, 0))
    return pl.pallas_call(
        _memory_kv_kernel,
        grid=(b,),
        in_specs=[pl.BlockSpec((1, N_MEM, D_MODEL), lambda i: (i, 0, 0)),
                  pl.BlockSpec((1, D_MODEL), lambda i: (0, 0)),
                  pl.BlockSpec((D_MODEL, XA_W), lambda i: (0, 0)),
                  pl.BlockSpec((D_MODEL, XA_W), lambda i: (0, 0))],
        out_specs=[blk_f, blk_f, blk_b, blk_b],
        out_shape=[out_f, out_f, out_b, out_b],
        name="memory_kv",
    )(mem, g.reshape(1, D_MODEL), wk, wv)


def _prompt_layer(x, mk_b, mv_b, w_in, w_out, params):
    b, t, _ = x.shape
    tb = TOKEN_BLOCK
    spp = t // tb
    prep = WEIGHT_PREP_STEPS
    cos_t, sin_t = _rope_tables(np.arange(t))
    mdec, qdec, kdec, rdec = _retention_tables(TOKEN_BLOCK)
    consts = (mdec, qdec, kdec, rdec)
    in_rows, out_rows = D_MODEL // prep, D_MIX // prep

    def slab(step):
        return jnp.minimum(step, prep - 1)

    def block(token_step):
        return jnp.clip(token_step, 0, spp - 1)

    def run(step):
        return block(step - prep)

    in_specs = ([pl.BlockSpec((in_rows, D_IN), lambda s: (slab(s), 0)),
                 pl.BlockSpec((out_rows, D_MODEL), lambda s: (slab(s), 0)),
                 pl.BlockSpec((b, tb, D_MODEL), lambda s: (0, run(s), 0)),
                 pl.BlockSpec((b, tb, D_MODEL), lambda s: (0, block(run(s) + 1), 0)),
                 pl.BlockSpec((tb, LANES), lambda s: (run(s), 0)),
                 pl.BlockSpec((tb, LANES), lambda s: (run(s), 0))]
                + [_const_spec(c.shape) for c in consts]
                + [_const_spec(mk_b.shape), _const_spec(mv_b.shape)]
                + [_const_spec(p.shape) for p in params])
    out_shape = [jax.ShapeDtypeStruct((b, t, D_MODEL), F32),
                 jax.ShapeDtypeStruct((b, HG_H, HG_DK, HG_DV), F32),
                 jax.ShapeDtypeStruct((b, RT_H // 2, 2 * RT_DK, RT_DV), F32),
                 jax.ShapeDtypeStruct((D_MODEL, D_IN), BF16),
                 jax.ShapeDtypeStruct((D_MIX, D_MODEL), BF16)]
    out_specs = [pl.BlockSpec((b, tb, D_MODEL), lambda s: (0, run(s), 0)),
                 pl.BlockSpec((b, HG_H, HG_DK, HG_DV), lambda s: (0, 0, 0, 0)),
                 pl.BlockSpec((b, RT_H // 2, 2 * RT_DK, RT_DV), lambda s: (0, 0, 0, 0)),
                 pl.BlockSpec((in_rows, D_IN), lambda s: (slab(s), 0)),
                 pl.BlockSpec((out_rows, D_MODEL), lambda s: (slab(s), 0))]
    return pl.pallas_call(
        functools.partial(_prompt_kernel, spp),
        grid=(prep + spp,),
        in_specs=in_specs,
        out_specs=out_specs,
        out_shape=out_shape,
        scratch_shapes=_layer_scratch(b * TOKEN_BLOCK) + [
                                           pltpu.VMEM((b, HG_H, HG_DV, HG_DK), F32),
                                           pltpu.VMEM((D_MODEL, D_IN), BF16),
                                           pltpu.VMEM((D_MIX, D_MODEL), BF16),
                                           pltpu.VMEM((b * TOKEN_BLOCK, HG_W), F32)],
        compiler_params=pltpu.CompilerParams(dimension_semantics=("arbitrary",),
                                             vmem_limit_bytes=VMEM_LIMIT_BYTES),
        name="prompt_layer",
    )(w_in, w_out, x, x, cos_t, sin_t, *consts, mk_b, mv_b, *params)


def _sample_layer(x, s_hg, s_rt, mem_k, mem_v, weights):
    b, t, _ = x.shape
    assert t == CHUNK
    tb = TOKEN_BLOCK
    nsb = tb // t
    cos_t, sin_t = _rope_tables(PAST_LEN + np.arange(t))
    mdec, qdec, kdec, rdec = _retention_tables(t)
    consts = (mdec, qdec, kdec, rdec)
    s_rt2 = s_rt.reshape(b, RT_H // 2, 2 * RT_DK, RT_DV)
    mem_spec = pl.BlockSpec((nsb, N_MEM * XA_H, XA_DH), lambda i: (i, 0, 0))
    in_specs = ([pl.BlockSpec((tb, D_MODEL), lambda i: (i, 0)),
                 _const_spec(cos_t.shape), _const_spec(sin_t.shape)]
                + [_const_spec(c.shape) for c in consts]
                + [mem_spec, mem_spec,
                   pl.BlockSpec((nsb, HG_H, HG_DK, HG_DV), lambda i: (i, 0, 0, 0)),
                   pl.BlockSpec((nsb, RT_H // 2, 2 * RT_DK, RT_DV), lambda i: (i, 0, 0, 0))]
                + [_const_spec(w.shape) for w in weights])
    out_shape = [jax.ShapeDtypeStruct((b * t, D_MODEL), F32),
                 jax.ShapeDtypeStruct((b, HG_H, HG_DK, HG_DV), F32),
                 jax.ShapeDtypeStruct((b, RT_H // 2, 2 * RT_DK, RT_DV), F32)]
    out_specs = [pl.BlockSpec((tb, D_MODEL), lambda i: (i, 0)),
                 pl.BlockSpec((nsb, HG_H, HG_DK, HG_DV), lambda i: (i, 0, 0, 0)),
                 pl.BlockSpec((nsb, RT_H // 2, 2 * RT_DK, RT_DV), lambda i: (i, 0, 0, 0))]
    y, s_new, r_new = pl.pallas_call(
        _sample_kernel,
        grid=(b // nsb,),
        in_specs=in_specs,
        out_specs=out_specs,
        out_shape=out_shape,
        scratch_shapes=_layer_scratch(tb),
        compiler_params=pltpu.CompilerParams(dimension_semantics=("arbitrary",),
                                             vmem_limit_bytes=VMEM_LIMIT_BYTES),
        name="sample_layer",
    )(x.reshape(b * t, D_MODEL), cos_t, sin_t, *consts,
      mem_k.reshape(b, N_MEM * XA_H, XA_DH), mem_v.reshape(b, N_MEM * XA_H, XA_DH), s_hg, s_rt2, *weights)
    return y.reshape(b, t, D_MODEL), s_new, r_new


def kernel(x_prompt, x_sample, mem_prompt, state_hgrn, state_ret, cache_mem_k, cache_mem_v, norm_g, w_in,
           lb_logits, hg_norm_g, rt_norm_g, mem_norm_g, w_mem_k, w_mem_v, w_out, final_norm_g):
    depth = w_in.shape[0]
    assert depth == 1, "single-layer step"
    bp = x_prompt.shape[0]
    bs = x_sample.shape[0]
    g_in, g_hg, g_rt = norm_g[0].reshape(1, D_MODEL), hg_norm_g[0].reshape(1, HG_W), rt_norm_g[0].reshape(1, RT_W)
    g_fin = final_norm_g.reshape(1, D_MODEL)
    mk, mv, mk_b, mv_b = _memory_kv(mem_prompt, mem_norm_g[0], w_mem_k[0], w_mem_v[0])
    y_p, s_p, r_p, w_in_b, w_out_b = _prompt_layer(x_prompt, mk_b, mv_b, w_in[0], w_out[0],
                                                   (g_in, lb_logits, g_hg, g_rt, g_fin))
    y_s, s_s, r_s = _sample_layer(x_sample, state_hgrn[0], state_ret[0], cache_mem_k[0], cache_mem_v[0],
                                  (g_in, w_in_b, lb_logits, g_hg, g_rt, w_out_b, g_fin))
    return (y_p, y_s,
            s_p.reshape(1, bp, HG_H, HG_DK, HG_DV),
            r_p.reshape(1, bp, RT_H, RT_DK, RT_DV),
            mk.reshape(1, bp, N_MEM, XA_H, XA_DH),
            mv.reshape(1, bp, N_MEM, XA_H, XA_DH),
            s_s.reshape(1, bs, HG_H, HG_DK, HG_DV),
            r_s.reshape(1, bs, RT_H, RT_DK, RT_DV))
```

```python
import functools

import numpy as np
import jax
import jax.numpy as jnp
from jax import lax
from jax.experimental import pallas as pl
from jax.experimental.pallas import tpu as pltpu

F32 = jnp.float32
BF16 = jnp.bfloat16

D_MODEL = 1024
PAST_LEN = 4096
CHUNK = 64
N_MEM = 256
EPS = 1e-6
ROPE_BASE = 10000.0
HG_H, HG_DK, HG_DV = 8, 128, 128
HG_W = HG_H * HG_DV
RT_H, RT_DK, RT_DV = 4, 64, 128
RT_QK = RT_H * RT_DK
RT_W = RT_H * RT_DV
XA_H, XA_DH = 4, 128
XA_W = XA_H * XA_DH
D_MIX = HG_W + RT_W + XA_W
OFF_HG_Q = 0
OFF_HG_F = OFF_HG_Q + HG_W
OFF_HG_I = OFF_HG_F + HG_W
OFF_HG_GATE = OFF_HG_I + HG_W
OFF_RT_Q = OFF_HG_GATE + HG_W
OFF_RT_K = OFF_RT_Q + RT_QK
OFF_RT_V = OFF_RT_K + RT_QK
OFF_RT_GATE = OFF_RT_V + RT_W
OFF_XA_Q = OFF_RT_GATE + RT_W
OFF_XA_GATE = OFF_XA_Q + XA_W
D_IN = OFF_XA_GATE + XA_W

LANES = 128
SUBLANES = 8
TOKEN_BLOCK = 256
WEIGHT_PREP_STEPS = 16
PROJ_N_TILE = 512
PROJ_TILES_PER_GATE_GROUP = 1
PROJ_TILES_PER_HGRN_STAGE = 1
HGRN_STREAMS_PER_STAGE = 4
PROMPT_HGRN_HEAD_GROUPS = 1
OUT_N_TILE = 512
VMEM_LIMIT_BYTES = 60 * 1024 * 1024
_NT = (((1,), (1,)), ((), ()))
_TN = (((0,), (0,)), ((), ()))


def _silu(x):
    h = 0.5 * x
    return h + h * jnp.tanh(h)


def _rms(x, g):
    return x * lax.rsqrt(jnp.mean(x * x, axis=-1, keepdims=True) + EPS) * g


def _cols(off, i, width):
    return slice(off + i * width, off + (i + 1) * width)


def _rope_tables(pos):
    half = RT_DK // 2
    inv_freq = ROPE_BASE ** (-np.arange(half, dtype=np.float64) / half)
    ang = np.asarray(pos, np.float64)[:, None] * inv_freq[None, :]
    cos, sin = np.cos(ang), np.sin(ang)
    cos_t = np.concatenate([cos, cos, cos, cos], axis=-1)
    sin_t = np.concatenate([-sin, sin, -sin, sin], axis=-1)
    return jnp.asarray(cos_t, F32), jnp.asarray(sin_t, F32)


def _retention_tables(ts):
    h = np.arange(RT_H, dtype=np.float64)
    log_gamma = np.log1p(-np.exp2(-5.0 - h))
    n = np.arange(ts, dtype=np.float64)
    scale = RT_DK ** -0.5
    dist = np.abs(n[:, None] - n[None, :])
    visible = (n[None, :] // CHUNK) <= (n[:, None] // CHUNK)
    mdec = scale * np.exp(log_gamma[:, None, None] * dist[None]) * visible[None]
    qd = np.exp(log_gamma[:, None] * (n[None, :] + 1.0))
    kd = scale * np.exp(log_gamma[:, None] * (ts - 1.0 - n[None, :]))
    rd = np.exp(log_gamma * ts)

    def pair_lanes(a):
        a = np.repeat(a[:, :, None], RT_DK, axis=2)
        return np.concatenate([a[0::2], a[1::2]], axis=2)

    rdec = np.repeat(np.repeat(rd[:, None], RT_DK, axis=1).reshape(RT_H // 2, 2 * RT_DK, 1), LANES, axis=2)
    return (jnp.asarray(mdec, F32), jnp.asarray(pair_lanes(qd), F32), jnp.asarray(pair_lanes(kd), F32),
            jnp.asarray(rdec, F32))


def _hgrn_chunks(proj_ref, gc_ref, mixed_ref, hg_g_ref, chunks, heads, score_mask, get_st, set_st, proj):
    half = CHUNK // 2
    zero_half = jnp.zeros((half, HG_DK), BF16)
    insts = [(s, h) for s, _ in chunks for h in heads]
    rows = dict(chunks)
    q_rel, k_rel, q_abs, k_end, v_b, dec = {}, {}, {}, {}, {}, {}
    for (s, h) in insts:
        q = _silu(proj_ref[rows[s], _cols(OFF_HG_Q, h, HG_DK)])
        k = proj_ref[rows[s], _cols(OFF_HG_F, h, HG_DK)]
        gc = gc_ref[rows[s], _cols(0, h, HG_DK)]
        g_end = gc[CHUNK - 1:CHUNK, :]
        (q1, q2), (k1, k2), (g1, g2) = ((a[:half], a[half:]) for a in (q, k, gc))
        g_m1, g_m2 = gc[half // 2 - 1:half // 2, :], gc[half + half // 2 - 1:half + half // 2, :]
        g_b = gc[half - 1:half, :]
        qd1, kd1 = (q1 * jnp.exp2(g1 - g_m1)).astype(BF16), (k1 * jnp.exp2(g_m1 - g1)).astype(BF16)
        qd2, kd2 = (q2 * jnp.exp2(g2 - g_m2)).astype(BF16), (k2 * jnp.exp2(g_m2 - g2)).astype(BF16)
        qo, ko = (q2 * jnp.exp2(g2 - g_b)).astype(BF16), (k1 * jnp.exp2(g_b - g1)).astype(BF16)
        q_rel[s, h] = jnp.concatenate([jnp.concatenate([qd1, qd2], axis=0),
                                       jnp.concatenate([zero_half, qo], axis=0)], axis=1)
        k_rel[s, h] = jnp.concatenate([jnp.concatenate([kd1, kd2, zero_half, zero_half], axis=0),
                                       jnp.concatenate([zero_half, zero_half, ko, zero_half], axis=0)],
                                      axis=1)
        q_abs[s, h] = (q * jnp.exp2(gc)).astype(BF16)
        k_end[s, h] = (k * jnp.exp2(g_end - gc)).astype(BF16)
        v_b[s, h] = proj_ref[rows[s], _cols(OFF_HG_I, h, HG_DV)].astype(BF16)
        dec[s, h] = jnp.exp2(g_end)
    proj.emit(PROJ_TILES_PER_HGRN_STAGE)
    a = {k: lax.dot_general(q_rel[k], k_rel[k], _NT, preferred_element_type=F32) for k in insts}
    st = {k: get_st(*k) for k in insts}
    v_t = {k: jnp.concatenate([v_b[k], v_b[k][:half], zero_half], axis=0).T for k in insts}
    kv = {k: jnp.dot(v_t[k][:, :CHUNK], k_end[k], preferred_element_type=F32) for k in insts}
    for k in insts:
        set_st(*k, dec[k] * st[k] + kv[k])
    a_m = {k: jnp.where(score_mask, a[k], 0.0).astype(BF16) for k in insts}
    o = {k: lax.dot_general(jnp.concatenate([q_abs[k], a_m[k]], axis=1),
                            jnp.concatenate([st[k].astype(BF16), v_t[k]], axis=1), _NT, preferred_element_type=F32)
         for k in insts}
    ms = {k: jnp.mean(o[k] * o[k], axis=-1, keepdims=True) for k in insts}
    for (s, h) in insts:
        gate = _silu(proj_ref[rows[s], _cols(OFF_HG_GATE, h, HG_DV)].astype(BF16))
        on = o[s, h] * lax.rsqrt(ms[s, h] + EPS) * hg_g_ref[:, _cols(0, h, HG_DV)]
        mixed_ref[rows[s], _cols(0, h, HG_DV)] = on.astype(BF16) * gate


def _retention(n_streams, ts, proj_ref, mixed_ref, cos_t, sin_t, mdec_ref, qdec_ref, kdec_ref, rdec_ref,
               rt_g_ref, get_r, set_r):
    lane = lax.broadcasted_iota(jnp.int32, (ts, LANES), 1)
    first_half = (lane % RT_DK) < (RT_DK // 2)
    low_head = lane < RT_DK
    low_rows = lax.broadcasted_iota(jnp.int32, (2 * RT_DK, RT_DV), 0) < RT_DK

    def rope(xs, cs, sn):
        swapped = jnp.where(first_half, pltpu.roll(xs, LANES - RT_DK // 2, 1), pltpu.roll(xs, RT_DK // 2, 1))
        return xs * cs + swapped * sn

    pairs = [(s, p) for s in range(n_streams) for p in range(RT_H // 2)]
    insts = [(s, p, i) for (s, p) in pairs for i in range(2)]
    rows = {s: slice(s * ts, (s + 1) * ts) for s in range(n_streams)}
    qp, kp_b, qd, kd_b, r_pair, v_pair = {}, {}, {}, {}, {}, {}
    for (s, p) in pairs:
        cs, sn = cos_t[rows[s], :], sin_t[rows[s], :]
        q = rope(proj_ref[rows[s], _cols(OFF_RT_Q, p, LANES)], cs, sn)
        k = rope(proj_ref[rows[s], _cols(OFF_RT_K, p, LANES)], cs, sn)
        qp[s, p] = q
        qd[s, p] = q * qdec_ref[p]
        kp_b[s, p] = k.astype(BF16)
        kd_b[s, p] = (k * kdec_ref[p]).astype(BF16)
        r_pair[s, p] = get_r(s, p)
        v_pair[s, p] = proj_ref[rows[s], _cols(OFF_RT_V, p, 2 * RT_DV)].astype(BF16)
    yield
    sel = {0: low_head, 1: jnp.logical_not(low_head)}
    q_h = {(s, p, i): jnp.where(sel[i], qp[s, p], 0.0).astype(BF16) for (s, p, i) in insts}
    qd_h = {(s, p, i): jnp.where(sel[i], qd[s, p], 0.0).astype(BF16) for (s, p, i) in insts}
    sc = {(s, p, i): lax.dot_general(q_h[s, p, i], kp_b[s, p], _NT, preferred_element_type=F32)
          for (s, p, i) in insts}
    inter = {(s, p, i): jnp.dot(qd_h[s, p, i], r_pair[s, p].astype(BF16), preferred_element_type=F32)
             for (s, p, i) in insts}
    kv = {(s, p): lax.dot_general(kd_b[s, p], v_pair[s, p], _TN, preferred_element_type=F32) for (s, p) in pairs}
    yield
    for (s, p) in pairs:
        dr = jnp.where(low_rows, kv[s, p][:, :RT_DV], kv[s, p][:, RT_DV:])
        set_r(s, p, rdec_ref[p] * r_pair[s, p] + dr)
    sc_b = {(s, p, i): (sc[s, p, i] * mdec_ref[2 * p + i]).astype(BF16) for (s, p, i) in insts}
    yield
    o = {(s, p, i): jnp.dot(sc_b[s, p, i], v_pair[s, p][:, i * RT_DV:(i + 1) * RT_DV],
                            preferred_element_type=F32) + inter[s, p, i] for (s, p, i) in insts}
    yield
    mu = {k: jnp.mean(o[k], axis=-1, keepdims=True) for k in insts}
    cen = {k: o[k] - mu[k] for k in insts}
    var = {k: jnp.mean(cen[k] * cen[k], axis=-1, keepdims=True) for k in insts}
    yield
    for (s, p, i) in insts:
        h = 2 * p + i
        gate = _silu(proj_ref[rows[s], _cols(OFF_RT_GATE, h, RT_DV)].astype(BF16))
        on = cen[s, p, i] * lax.rsqrt(var[s, p, i] + EPS) * rt_g_ref[:, _cols(0, h, RT_DV)]
        mixed_ref[rows[s], _cols(HG_W, h, RT_DV)] = on.astype(BF16) * gate


def _cross_attention(n_streams, ts, proj_ref, mixed_ref, get_mem):
    xa_scale = XA_DH ** -0.5
    insts = [(s, h) for s in range(n_streams) for h in range(XA_H)]
    rows = {s: slice(s * ts, (s + 1) * ts) for s in range(n_streams)}
    mem = {k: get_mem(*k) for k in insts}
    sc = {(s, h): lax.dot_general(proj_ref[rows[s], _cols(OFF_XA_Q, h, XA_DH)].astype(BF16), mem[s, h][0], _NT,
                                  preferred_element_type=F32) * xa_scale for (s, h) in insts}
    yield
    mx = {k: jnp.max(sc[k], axis=-1, keepdims=True) for k in insts}
    yield
    pr = {k: jnp.exp(sc[k] - mx[k]) for k in insts}
    den = {k: jnp.sum(pr[k], axis=-1, keepdims=True) for k in insts}
    yield
    o = {k: jnp.dot(pr[k].astype(BF16), mem[k][1], preferred_element_type=F32) for k in insts}
    yield
    for (s, h) in insts:
        gate = _silu(proj_ref[rows[s], _cols(OFF_XA_GATE, h, XA_DH)].astype(BF16))
        mixed_ref[rows[s], _cols(HG_W + RT_W, h, XA_DH)] = (o[s, h] / den[s, h]).astype(BF16) * gate


def _chunk_cumsum(g):
    tokens, c = g.shape
    x = g.reshape(tokens // SUBLANES, SUBLANES, c)
    sub = lax.broadcasted_iota(jnp.int32, x.shape, 1)
    shift = 1
    while shift < SUBLANES:
        x = x + jnp.where(sub >= shift, pltpu.roll(x, shift, 1), 0.0)
        shift *= 2
    tiles = CHUNK // SUBLANES
    x = x.reshape(tokens // CHUNK, tiles, SUBLANES, c)
    out = [x[:, 0]]
    for j in range(1, tiles):
        total = out[-1][:, SUBLANES - 1:SUBLANES, :]
        out.append(x[:, j] + total)
    return jnp.stack(out, axis=1).reshape(tokens, c)


class _Projection:
    TILES_GATES = HG_W // PROJ_N_TILE
    TILES_ALL = D_IN // PROJ_N_TILE

    def __init__(self, x, norm_g_ref, w_in_ref, xn_ref, proj_ref, done=0, head_groups=1):
        if x is not None:
            xn_ref[...] = _rms(x, norm_g_ref[...]).astype(BF16)
        self._refs = (xn_ref, w_in_ref, proj_ref)
        group_w = HG_W // head_groups
        assert group_w % PROJ_N_TILE == 0
        segments = ([(OFF_HG_F, OFF_HG_F + HG_W)]
                    + [(off + g * group_w, off + (g + 1) * group_w)
                       for g in range(head_groups) for off in (OFF_HG_Q, OFF_HG_I, OFF_HG_GATE)]
                    + [(OFF_RT_Q, D_IN)])
        self._group_tiles = 3 * group_w // PROJ_N_TILE
        self._pending = [n0 for lo, hi in segments for n0 in range(lo, hi, PROJ_N_TILE)]
        assert len(self._pending) == self.TILES_ALL
        del self._pending[:done]
        self._done = done

    def tiles_for_head_group(self, g):
        return self.TILES_GATES + (g + 1) * self._group_tiles

    def emit(self, n=1):
        xn_ref, w_in_ref, proj_ref = self._refs
        for _ in range(min(n, len(self._pending))):
            cols = slice(self._pending[0], self._pending.pop(0) + PROJ_N_TILE)
            proj_ref[:, cols] = jnp.dot(xn_ref[...], w_in_ref[:, cols], preferred_element_type=F32)
            self._done += 1

    def require(self, n_tiles):
        self.emit(n_tiles - self._done)


def _mix(n_streams, ts, x_ref, y_ref, cos_t, sin_t, mdec_ref, qdec_ref, kdec_ref, rdec_ref,
         get_mem, lb_ref, hg_g_ref, rt_g_ref, w_out_ref, fin_g_ref,
         proj_ref, gc_ref, mixed_ref, get_st, set_st, get_r, set_r, proj,
         late_stages=(), fgate_ref=None, hgrn_head_groups=1):
    proj.require(proj.TILES_GATES)
    lg = lb_ref[...]
    e = jnp.exp(lg - jnp.max(lg, axis=0, keepdims=True))
    lb = e[0:1, :] / jnp.sum(e, axis=0, keepdims=True)
    for c0 in range(0, HG_W, 2 * LANES):
        cols = slice(OFF_HG_F + c0, OFF_HG_F + c0 + 2 * LANES)
        lbc = lb[:, c0:c0 + 2 * LANES]
        half_span = 0.5 * (1.0 - lbc)
        pre = proj_ref[:, cols] if fgate_ref is None else fgate_ref[:, c0:c0 + 2 * LANES]
        bt = half_span * jnp.tanh(0.5 * pre)
        g = jnp.log2(lbc + (half_span + bt))
        proj_ref[:, cols] = half_span - bt
        gc_ref[:, c0:c0 + 2 * LANES] = _chunk_cumsum(g)
        proj.emit(PROJ_TILES_PER_GATE_GROUP)

    row = lax.broadcasted_iota(jnp.int32, (CHUNK, LANES), 0)
    col = lax.broadcasted_iota(jnp.int32, (CHUNK, LANES), 1)
    half = CHUNK // 2
    causal = (((col <= row) & (col // half == row // half))
              | ((col >= CHUNK) & (col < CHUNK + half) & (row >= half)))
    group_heads = HG_H // hgrn_head_groups
    for g in range(hgrn_head_groups):
        proj.require(proj.tiles_for_head_group(g))
        heads = range(g * group_heads, (g + 1) * group_heads)
        for c in range(ts // CHUNK):
            chunks = [(s, slice(s * ts + c * CHUNK, s * ts + (c + 1) * CHUNK)) for s in range(n_streams)]
            for i in range(0, n_streams, HGRN_STREAMS_PER_STAGE):
                _hgrn_chunks(proj_ref, gc_ref, mixed_ref, hg_g_ref, chunks[i:i + HGRN_STREAMS_PER_STAGE], heads,
                             causal, get_st, set_st, proj)

    proj.require(proj.TILES_ALL)
    def cross_attention_and_its_rows():
        yield from _cross_attention(n_streams, ts, proj_ref, mixed_ref, get_mem)
        yield from _out_projection(mixed_ref, w_out_ref, y_ref, HG_W + RT_W, D_MIX, y_ref)

    _run_interleaved(
        _retention(n_streams, ts, proj_ref, mixed_ref, cos_t, sin_t, mdec_ref, qdec_ref, kdec_ref, rdec_ref,
                   rt_g_ref, get_r, set_r),
        cross_attention_and_its_rows(),
        _out_projection(mixed_ref, w_out_ref, y_ref, 0, HG_W, x_ref))
    sumsq = []
    _run_interleaved(_out_projection(mixed_ref, w_out_ref, y_ref, HG_W, HG_W + RT_W, y_ref, sumsq), *late_stages)

    inv = lax.rsqrt(sum(sumsq) * (1.0 / D_MODEL) + EPS)
    y_ref[...] = y_ref[...] * inv * fin_g_ref[...]


def _out_projection(mixed_ref, w_out_ref, acc_ref, k_lo, k_hi, base_ref, sumsq=None):
    for n0 in range(0, D_MODEL, OUT_N_TILE):
        cols = slice(n0, n0 + OUT_N_TILE)
        out = base_ref[:, cols] + jnp.dot(mixed_ref[:, k_lo:k_hi], w_out_ref[k_lo:k_hi, cols],
                                          preferred_element_type=F32)
        acc_ref[:, cols] = out
        if sumsq is not None:
            sumsq.append(jnp.sum(out * out, axis=-1, keepdims=True))
        yield


class _StackedRows:
    def __init__(self, refs):
        self._refs = refs
        self._rows = refs[0].shape[0]

    def __getitem__(self, idx):
        cols = idx[1] if isinstance(idx, tuple) else slice(None)
        return jnp.concatenate([r[:, cols] for r in self._refs], axis=0)

    def __setitem__(self, idx, value):
        cols = idx[1] if isinstance(idx, tuple) else slice(None)
        for s, r in enumerate(self._refs):
            r[:, cols] = value[s * self._rows:(s + 1) * self._rows]


def _run_interleaved(*stage_generators):
    pending = list(stage_generators)
    while pending:
        for g in list(pending):
            try:
                next(g)
            except StopIteration:
                pending.remove(g)


def _memory_kv_kernel(mem_ref, g_ref, wk_ref, wv_ref, k_ref, v_ref, kb_ref, vb_ref):
    m = _rms(mem_ref[0], g_ref[...]).astype(BF16)
    k = jnp.dot(m, wk_ref[...].astype(BF16), preferred_element_type=F32)
    v = jnp.dot(m, wv_ref[...].astype(BF16), preferred_element_type=F32)
    for h in range(XA_H):
        k_ref[0, pl.ds(h, N_MEM, stride=XA_H), :] = k[:, _cols(0, h, XA_DH)]
        v_ref[0, pl.ds(h, N_MEM, stride=XA_H), :] = v[:, _cols(0, h, XA_DH)]
    kb_ref[0] = k.astype(BF16)
    vb_ref[0] = v.astype(BF16)


def _prompt_kernel(steps_per_stream, w_in_slab_ref, w_out_slab_ref,
                   x_ref, x_next_ref, cos_ref, sin_ref, mdec_ref, qdec_ref, kdec_ref, rdec_ref, mk_ref, mv_ref,
                   norm_g_ref, lb_ref, hg_g_ref, rt_g_ref, fin_g_ref,
                   y_ref, s_out_ref, r_out_ref, w_in_b_slab_ref, w_out_b_slab_ref,
                   proj_ref, xn_ref, gc_ref, mixed_ref, st_ref, w_in_ref, w_out_ref, fgate_ref):
    n_streams = x_ref.shape[0]
    t = pl.program_id(0)
    j = t - WEIGHT_PREP_STEPS
    running = t >= WEIGHT_PREP_STEPS
    ahead = _Projection.TILES_GATES

    def start_block(src_ref):
        for s in range(n_streams):
            xn_ref[s * TOKEN_BLOCK:(s + 1) * TOKEN_BLOCK, :] = _rms(src_ref[s], norm_g_ref[...]).astype(BF16)
            yield
        for n0 in range(0, HG_W, PROJ_N_TILE):
            fgate_ref[:, n0:n0 + PROJ_N_TILE] = jnp.dot(
                xn_ref[...], w_in_ref[:, OFF_HG_F + n0:OFF_HG_F + n0 + PROJ_N_TILE], preferred_element_type=F32)
            yield

    @pl.when(jnp.logical_not(running))
    def _():
        for slab_ref, out_ref, full_ref in ((w_in_slab_ref, w_in_b_slab_ref, w_in_ref),
                                            (w_out_slab_ref, w_out_b_slab_ref, w_out_ref)):
            n = slab_ref.shape[0]
            slab = slab_ref[...].astype(BF16)
            full_ref[pl.ds(pl.multiple_of(t * n, n), n), :] = slab
            out_ref[...] = slab

    @pl.when(t == WEIGHT_PREP_STEPS - 1)
    def _():
        _run_interleaved(start_block(x_ref))
        st_ref[...] = jnp.zeros_like(st_ref)
        r_out_ref[...] = jnp.zeros_like(r_out_ref)

    def set_st(s, h, val):
        st_ref[s, h] = val

    def set_r(s, p, val):
        r_out_ref[s, p] = val

    def get_mem(s, h):
        return mk_ref[s, :, _cols(0, h, XA_DH)], mv_ref[s, :, _cols(0, h, XA_DH)]

    @pl.when(running)
    def _():
        cos_t = jnp.concatenate([cos_ref[...]] * n_streams, axis=0)
        sin_t = jnp.concatenate([sin_ref[...]] * n_streams, axis=0)
        x_blk = _StackedRows([x_ref.at[s] for s in range(n_streams)])
        y_blk = _StackedRows([y_ref.at[s] for s in range(n_streams)])
        proj = _Projection(None, norm_g_ref, w_in_ref, xn_ref, proj_ref, done=ahead,
                           head_groups=PROMPT_HGRN_HEAD_GROUPS)
        _mix(
            n_streams, TOKEN_BLOCK, x_blk, y_blk, cos_t, sin_t,
            mdec_ref, qdec_ref, kdec_ref, rdec_ref,
            get_mem, lb_ref, hg_g_ref, rt_g_ref, w_out_ref, fin_g_ref,
            proj_ref, gc_ref, mixed_ref,
            lambda s, h: st_ref[s, h], set_st, lambda s, p: r_out_ref[s, p], set_r, proj,
            late_stages=(start_block(x_next_ref),), fgate_ref=fgate_ref,
            hgrn_head_groups=PROMPT_HGRN_HEAD_GROUPS)

    @pl.when(j == steps_per_stream - 1)
    def _():
        for s in range(n_streams):
            for h in range(HG_H):
                s_out_ref[s, h] = st_ref[s, h].T


def _sample_kernel(x_ref, cos_ref, sin_ref, mdec_ref, qdec_ref, kdec_ref, rdec_ref, mk_ref, mv_ref,
                   s_in_ref, r_in_ref,
                   norm_g_ref, w_in_ref, lb_ref, hg_g_ref, rt_g_ref, w_out_ref, fin_g_ref,
                   y_ref, s_out_ref, r_out_ref, proj_ref, xn_ref, gc_ref, mixed_ref):
    n_streams = TOKEN_BLOCK // CHUNK

    def set_st(s, h, val):
        s_out_ref[s, h] = val.T

    def set_r(s, p, val):
        r_out_ref[s, p] = val

    def get_mem(s, h):
        head_rows = pl.ds(h, N_MEM, stride=XA_H)
        return mk_ref[s, head_rows, :].astype(BF16), mv_ref[s, head_rows, :].astype(BF16)

    cos_t = jnp.concatenate([cos_ref[...]] * n_streams, axis=0)
    sin_t = jnp.concatenate([sin_ref[...]] * n_streams, axis=0)
    proj = _Projection(x_ref[...], norm_g_ref, w_in_ref, xn_ref, proj_ref)
    _mix(
        n_streams, CHUNK, x_ref, y_ref, cos_t, sin_t, mdec_ref, qdec_ref, kdec_ref, rdec_ref,
        get_mem, lb_ref, hg_g_ref, rt_g_ref, w_out_ref, fin_g_ref,
        proj_ref, gc_ref, mixed_ref,
        lambda s, h: s_in_ref[s, h].T, set_st, lambda s, p: r_in_ref[s, p], set_r, proj)


def _const_spec(shape):
    zeros = (0,) * len(shape)
    return pl.BlockSpec(shape, lambda *_: zeros, pipeline_mode=pl.Buffered(1))


def _layer_scratch(rows):
    return [pltpu.VMEM((rows, D_IN), F32),
            pltpu.VMEM((rows, D_MODEL), BF16),
            pltpu.VMEM((rows, HG_W), F32),
            pltpu.VMEM((rows, D_MIX), BF16)]


def _memory_kv(mem, g, wk, wv):
    b = mem.shape[0]
    out_f = jax.ShapeDtypeStruct((b, N_MEM * XA_H, XA_DH), F32)
    out_b = jax.ShapeDtypeStruct((b, N_MEM, XA_W), BF16)
    blk_f = pl.BlockSpec((1, N_MEM * XA_H, XA_DH), lambda i: (i, 0, 0))
    blk_b = pl.BlockSpec((1, N_MEM, XA_W), lambda i: (i, 0, 0))
    return pl.pallas_call(
        _memory_kv_kernel,
        grid=(b,),
        in_specs=[pl.BlockSpec((1, N_MEM, D_MODEL), lambda i: (i, 0, 0)),
                  pl.BlockSpec((1, D_MODEL), lambda i: (0, 0)),
                  pl.BlockSpec((D_MODEL, XA_W), lambda i: (0, 0)),
                  pl.BlockSpec((D_MODEL, XA_W), lambda i: (0, 0))],
        out_specs=[blk_f, blk_f, blk_b, blk_b],
        out_shape=[out_f, out_f, out_b, out_b],
        name="memory_kv",
    )(mem, g.reshape(1, D_MODEL), wk, wv)


def _prompt_layer(x, mk_b, mv_b, w_in, w_out, params):
    b, t, _ = x.shape
    tb = TOKEN_BLOCK
    spp = t // tb
    prep = WEIGHT_PREP_STEPS
    cos_t, sin_t = _rope_tables(np.arange(t))
    mdec, qdec, kdec, rdec = _retention_tables(TOKEN_BLOCK)
    consts = (mdec, qdec, kdec, rdec)
    in_rows, out_rows = D_MODEL // prep, D_MIX // prep

    def slab(step):
        return jnp.minimum(step, prep - 1)

    def block(token_step):
        return jnp.clip(token_step, 0, spp - 1)

    def run(step):
        return block(step - prep)

    in_specs = ([pl.BlockSpec((in_rows, D_IN), lambda s: (slab(s), 0)),
                 pl.BlockSpec((out_rows, D_MODEL), lambda s: (slab(s), 0)),
                 pl.BlockSpec((b, tb, D_MODEL), lambda s: (0, run(s), 0)),
                 pl.BlockSpec((b, tb, D_MODEL), lambda s: (0, block(run(s) + 1), 0)),
                 pl.BlockSpec((tb, LANES), lambda s: (run(s), 0)),
                 pl.BlockSpec((tb, LANES), lambda s: (run(s), 0))]
                + [_const_spec(c.shape) for c in consts]
                + [_const_spec(mk_b.shape), _const_spec(mv_b.shape)]
                + [_const_spec(p.shape) for p in params])
    out_shape = [jax.ShapeDtypeStruct((b, t, D_MODEL), F32),
                 jax.ShapeDtypeStruct((b, HG_H, HG_DK, HG_DV), F32),
                 jax.ShapeDtypeStruct((b, RT_H // 2, 2 * RT_DK, RT_DV), F32),
                 jax.ShapeDtypeStruct((D_MODEL, D_IN), BF16),
                 jax.ShapeDtypeStruct((D_MIX, D_MODEL), BF16)]
    out_specs = [pl.BlockSpec((b, tb, D_MODEL), lambda s: (0, run(s), 0)),
                 pl.BlockSpec((b, HG_H, HG_DK, HG_DV), lambda s: (0, 0, 0, 0)),
                 pl.BlockSpec((b, RT_H // 2, 2 * RT_DK, RT_DV), lambda s: (0, 0, 0, 0)),
                 pl.BlockSpec((in_rows, D_IN), lambda s: (slab(s), 0)),
                 pl.BlockSpec((out_rows, D_MODEL), lambda s: (slab(s), 0))]
    return pl.pallas_call(
        functools.partial(_prompt_kernel, spp),
        grid=(prep + spp,),
        in_specs=in_specs,
        out_specs=out_specs,
        out_shape=out_shape,
        scratch_shapes=_layer_scratch(b * TOKEN_BLOCK) + [
                                           pltpu.VMEM((b, HG_H, HG_DV, HG_DK), F32),
                                           pltpu.VMEM((D_MODEL, D_IN), BF16),
                                           pltpu.VMEM((D_MIX, D_MODEL), BF16),
                                           pltpu.VMEM((b * TOKEN_BLOCK, HG_W), F32)],
        compiler_params=pltpu.CompilerParams(dimension_semantics=("arbitrary",),
                                             vmem_limit_bytes=VMEM_LIMIT_BYTES),
        name="prompt_layer",
    )(w_in, w_out, x, x, cos_t, sin_t, *consts, mk_b, mv_b, *params)


def _sample_layer(x, s_hg, s_rt, mem_k, mem_v, weights):
    b, t, _ = x.shape
    assert t == CHUNK
    tb = TOKEN_BLOCK
    nsb = tb // t
    cos_t, sin_t = _rope_tables(PAST_LEN + np.arange(t))
    mdec, qdec, kdec, rdec = _retention_tables(t)
    consts = (mdec, qdec, kdec, rdec)
    s_rt2 = s_rt.reshape(b, RT_H // 2, 2 * RT_DK, RT_DV)
    mem_spec = pl.BlockSpec((nsb, N_MEM * XA_H, XA_DH), lambda i: (i, 0, 0))
    in_specs = ([pl.BlockSpec((tb, D_MODEL), lambda i: (i, 0)),
                 _const_spec(cos_t.shape), _const_spec(sin_t.shape)]
                + [_const_spec(c.shape) for c in consts]
                + [mem_spec, mem_spec,
                   pl.BlockSpec((nsb, HG_H, HG_DK, HG_DV), lambda i: (i, 0, 0, 0)),
                   pl.BlockSpec((nsb, RT_H // 2, 2 * RT_DK, RT_DV), lambda i: (i, 0, 0, 0))]
                + [_const_spec(w.shape) for w in weights])
    out_shape = [jax.ShapeDtypeStruct((b * t, D_MODEL), F32),
                 jax.ShapeDtypeStruct((b, HG_H, HG_DK, HG_DV), F32),
                 jax.ShapeDtypeStruct((b, RT_H // 2, 2 * RT_DK, RT_DV), F32)]
    out_specs = [pl.BlockSpec((tb, D_MODEL), lambda i: (i, 0)),
                 pl.BlockSpec((nsb, HG_H, HG_DK, HG_DV), lambda i: (i, 0, 0, 0)),
                 pl.BlockSpec((nsb, RT_H // 2, 2 * RT_DK, RT_DV), lambda i: (i, 0, 0, 0))]
    y, s_new, r_new = pl.pallas_call(
        _sample_kernel,
        grid=(b // nsb,),
        in_specs=in_specs,
        out_specs=out_specs,
        out_shape=out_shape,
        scratch_shapes=_layer_scratch(tb),
        compiler_params=pltpu.CompilerParams(dimension_semantics=("arbitrary",),
                                             vmem_limit_bytes=VMEM_LIMIT_BYTES),
        name="sample_layer",
    )(x.reshape(b * t, D_MODEL), cos_t, sin_t, *consts,
      mem_k.reshape(b, N_MEM * XA_H, XA_DH), mem_v.reshape(b, N_MEM * XA_H, XA_DH), s_hg, s_rt2, *weights)
    return y.reshape(b, t, D_MODEL), s_new, r_new


def kernel(x_prompt, x_sample, mem_prompt, state_hgrn, state_ret, cache_mem_k, cache_mem_v, norm_g, w_in,
           lb_logits, hg_norm_g, rt_norm_g, mem_norm_g, w_mem_k, w_mem_v, w_out, final_norm_g):
    depth = w_in.shape[0]
    assert depth == 1, "single-layer step"
    bp = x_prompt.shape[0]
    bs = x_sample.shape[0]
    g_in, g_hg, g_rt = norm_g[0].reshape(1, D_MODEL), hg_norm_g[0].reshape(1, HG_W), rt_norm_g[0].reshape(1, RT_W)
    g_fin = final_norm_g.reshape(1, D_MODEL)
    mk, mv, mk_b, mv_b = _memory_kv(mem_prompt, mem_norm_g[0], w_mem_k[0], w_mem_v[0])
    y_p, s_p, r_p, w_in_b, w_out_b = _prompt_layer(x_prompt, mk_b, mv_b, w_in[0], w_out[0],
                                                   (g_in, lb_logits, g_hg, g_rt, g_fin))
    y_s, s_s, r_s = _sample_layer(x_sample, state_hgrn[0], state_ret[0], cache_mem_k[0], cache_mem_v[0],
                                  (g_in, w_in_b, lb_logits, g_hg, g_rt, w_out_b, g_fin))
    return (y_p, y_s,
            s_p.reshape(1, bp, HG_H, HG_DK, HG_DV),
            r_p.reshape(1, bp, RT_H, RT_DK, RT_DV),
            mk.reshape(1, bp, N_MEM, XA_H, XA_DH),
            mv.reshape(1, bp, N_MEM, XA_H, XA_DH),
            s_s.reshape(1, bs, HG_H, HG_DK, HG_DV),
            r_s.reshape(1, bs, RT_H, RT_DK, RT_DV))
```

```python
import functools

import numpy as np
import jax
import jax.numpy as jnp
from jax import lax
from jax.experimental import pallas as pl
from jax.experimental.pallas import tpu as pltpu

F32 = jnp.float32
BF16 = jnp.bfloat16

D_MODEL = 1024
PAST_LEN = 4096
CHUNK = 64
N_MEM = 256
EPS = 1e-6
ROPE_BASE = 10000.0
HG_H, HG_DK, HG_DV = 8, 128, 128
HG_W = HG_H * HG_DV
RT_H, RT_DK, RT_DV = 4, 64, 128
RT_QK = RT_H * RT_DK
RT_W = RT_H * RT_DV
XA_H, XA_DH = 4, 128
XA_W = XA_H * XA_DH
D_MIX = HG_W + RT_W + XA_W
OFF_HG_Q = 0
OFF_HG_F = OFF_HG_Q + HG_W
OFF_HG_I = OFF_HG_F + HG_W
OFF_HG_GATE = OFF_HG_I + HG_W
OFF_RT_Q = OFF_HG_GATE + HG_W
OFF_RT_K = OFF_RT_Q + RT_QK
OFF_RT_V = OFF_RT_K + RT_QK
OFF_RT_GATE = OFF_RT_V + RT_W
OFF_XA_Q = OFF_RT_GATE + RT_W
OFF_XA_GATE = OFF_XA_Q + XA_W
D_IN = OFF_XA_GATE + XA_W

LANES = 128
SUBLANES = 8
TOKEN_BLOCK = 256
PROMPT_BLOCKS_PER_STEP = 1
WEIGHT_PREP_STEPS = 16
PROJ_N_TILE = 512
PROJ_TILES_PER_GATE_GROUP = 1
PROJ_TILES_PER_HGRN_STAGE = 1
HGRN_STREAMS_PER_STAGE = 4
OUT_N_TILE = 512
VMEM_LIMIT_BYTES = 56 * 1024 * 1024
_NT = (((1,), (1,)), ((), ()))
_TN = (((0,), (0,)), ((), ()))


def _silu(x):
    h = 0.5 * x
    return h + h * jnp.tanh(h)


def _rms(x, g):
    return x * lax.rsqrt(jnp.mean(x * x, axis=-1, keepdims=True) + EPS) * g


def _cols(off, i, width):
    return slice(off + i * width, off + (i + 1) * width)


def _rope_tables(pos):
    half = RT_DK // 2
    inv_freq = ROPE_BASE ** (-np.arange(half, dtype=np.float64) / half)
    ang = np.asarray(pos, np.float64)[:, None] * inv_freq[None, :]
    cos, sin = np.cos(ang), np.sin(ang)
    cos_t = np.concatenate([cos, cos, cos, cos], axis=-1)
    sin_t = np.concatenate([-sin, sin, -sin, sin], axis=-1)
    return jnp.asarray(cos_t, F32), jnp.asarray(sin_t, F32)


def _retention_tables(ts):
    h = np.arange(RT_H, dtype=np.float64)
    log_gamma = np.log1p(-np.exp2(-5.0 - h))
    n = np.arange(ts, dtype=np.float64)
    scale = RT_DK ** -0.5
    dist = np.abs(n[:, None] - n[None, :])
    visible = (n[None, :] // CHUNK) <= (n[:, None] // CHUNK)
    mdec = scale * np.exp(log_gamma[:, None, None] * dist[None]) * visible[None]
    qd = np.exp(log_gamma[:, None] * (n[None, :] + 1.0))
    kd = scale * np.exp(log_gamma[:, None] * (ts - 1.0 - n[None, :]))
    rd = np.exp(log_gamma * ts)

    def pair_lanes(a):
        a = np.repeat(a[:, :, None], RT_DK, axis=2)
        return np.concatenate([a[0::2], a[1::2]], axis=2)

    rdec = np.repeat(np.repeat(rd[:, None], RT_DK, axis=1).reshape(RT_H // 2, 2 * RT_DK, 1), LANES, axis=2)
    return (jnp.asarray(mdec, F32), jnp.asarray(pair_lanes(qd), F32), jnp.asarray(pair_lanes(kd), F32),
            jnp.asarray(rdec, F32))


def _hgrn_chunks(proj_ref, gc_ref, mixed_ref, hg_g_ref, chunks, score_mask, get_st, set_st, proj):
    half = CHUNK // 2
    zero_half = jnp.zeros((half, HG_DK), BF16)
    insts = [(s, h) for s, _ in chunks for h in range(HG_H)]
    rows = dict(chunks)
    q_rel, k_rel, q_abs, k_end, v_b, dec = {}, {}, {}, {}, {}, {}
    for (s, h) in insts:
        q = _silu(proj_ref[rows[s], _cols(OFF_HG_Q, h, HG_DK)])
        k = proj_ref[rows[s], _cols(OFF_HG_F, h, HG_DK)]
        gc = gc_ref[rows[s], _cols(0, h, HG_DK)]
        g_end = gc[CHUNK - 1:CHUNK, :]
        (q1, q2), (k1, k2), (g1, g2) = ((a[:half], a[half:]) for a in (q, k, gc))
        g_m1, g_m2 = gc[half // 2 - 1:half // 2, :], gc[half + half // 2 - 1:half + half // 2, :]
        g_b = gc[half - 1:half, :]
        qd1, kd1 = (q1 * jnp.exp2(g1 - g_m1)).astype(BF16), (k1 * jnp.exp2(g_m1 - g1)).astype(BF16)
        qd2, kd2 = (q2 * jnp.exp2(g2 - g_m2)).astype(BF16), (k2 * jnp.exp2(g_m2 - g2)).astype(BF16)
        qo, ko = (q2 * jnp.exp2(g2 - g_b)).astype(BF16), (k1 * jnp.exp2(g_b - g1)).astype(BF16)
        q_rel[s, h] = jnp.concatenate([jnp.concatenate([qd1, qd2], axis=0),
                                       jnp.concatenate([zero_half, qo], axis=0)], axis=1)
        k_rel[s, h] = jnp.concatenate([jnp.concatenate([kd1, kd2, zero_half, zero_half], axis=0),
                                       jnp.concatenate([zero_half, zero_half, ko, zero_half], axis=0)],
                                      axis=1)
        q_abs[s, h] = (q * jnp.exp2(gc)).astype(BF16)
        k_end[s, h] = (k * jnp.exp2(g_end - gc)).astype(BF16)
        v_b[s, h] = proj_ref[rows[s], _cols(OFF_HG_I, h, HG_DV)].astype(BF16)
        dec[s, h] = jnp.exp2(g_end)
    proj.emit(PROJ_TILES_PER_HGRN_STAGE)
    a = {k: lax.dot_general(q_rel[k], k_rel[k], _NT, preferred_element_type=F32) for k in insts}
    st = {k: get_st(*k) for k in insts}
    v_t = {k: jnp.concatenate([v_b[k], v_b[k][:half], zero_half], axis=0).T for k in insts}
    kv = {k: jnp.dot(v_t[k][:, :CHUNK], k_end[k], preferred_element_type=F32) for k in insts}
    for k in insts:
        set_st(*k, dec[k] * st[k] + kv[k])
    a_m = {k: jnp.where(score_mask, a[k], 0.0).astype(BF16) for k in insts}
    o = {k: lax.dot_general(jnp.concatenate([q_abs[k], a_m[k]], axis=1),
                            jnp.concatenate([st[k].astype(BF16), v_t[k]], axis=1), _NT, preferred_element_type=F32)
         for k in insts}
    ms = {k: jnp.mean(o[k] * o[k], axis=-1, keepdims=True) for k in insts}
    for (s, h) in insts:
        gate = _silu(proj_ref[rows[s], _cols(OFF_HG_GATE, h, HG_DV)].astype(BF16))
        on = o[s, h] * lax.rsqrt(ms[s, h] + EPS) * hg_g_ref[:, _cols(0, h, HG_DV)]
        mixed_ref[rows[s], _cols(0, h, HG_DV)] = on.astype(BF16) * gate


def _retention(n_streams, ts, proj_ref, mixed_ref, cos_t, sin_t, mdec_ref, qdec_ref, kdec_ref, rdec_ref,
               rt_g_ref, get_r, set_r):
    lane = lax.broadcasted_iota(jnp.int32, (ts, LANES), 1)
    first_half = (lane % RT_DK) < (RT_DK // 2)
    low_head = lane < RT_DK
    low_rows = lax.broadcasted_iota(jnp.int32, (2 * RT_DK, RT_DV), 0) < RT_DK

    def rope(xs, cs, sn):
        swapped = jnp.where(first_half, pltpu.roll(xs, LANES - RT_DK // 2, 1), pltpu.roll(xs, RT_DK // 2, 1))
        return xs * cs + swapped * sn

    pairs = [(s, p) for s in range(n_streams) for p in range(RT_H // 2)]
    insts = [(s, p, i) for (s, p) in pairs for i in range(2)]
    rows = {s: slice(s * ts, (s + 1) * ts) for s in range(n_streams)}
    qp, kp_b, qd, kd_b, r_pair, v_pair = {}, {}, {}, {}, {}, {}
    for (s, p) in pairs:
        cs, sn = cos_t[rows[s], :], sin_t[rows[s], :]
        q = rope(proj_ref[rows[s], _cols(OFF_RT_Q, p, LANES)], cs, sn)
        k = rope(proj_ref[rows[s], _cols(OFF_RT_K, p, LANES)], cs, sn)
        qp[s, p] = q
        qd[s, p] = q * qdec_ref[p]
        kp_b[s, p] = k.astype(BF16)
        kd_b[s, p] = (k * kdec_ref[p]).astype(BF16)
        r_pair[s, p] = get_r(s, p)
        v_pair[s, p] = proj_ref[rows[s], _cols(OFF_RT_V, p, 2 * RT_DV)].astype(BF16)
    yield
    sel = {0: low_head, 1: jnp.logical_not(low_head)}
    q_h = {(s, p, i): jnp.where(sel[i], qp[s, p], 0.0).astype(BF16) for (s, p, i) in insts}
    qd_h = {(s, p, i): jnp.where(sel[i], qd[s, p], 0.0).astype(BF16) for (s, p, i) in insts}
    sc = {(s, p, i): lax.dot_general(q_h[s, p, i], kp_b[s, p], _NT, preferred_element_type=F32)
          for (s, p, i) in insts}
    inter = {(s, p, i): jnp.dot(qd_h[s, p, i], r_pair[s, p].astype(BF16), preferred_element_type=F32)
             for (s, p, i) in insts}
    kv = {(s, p): lax.dot_general(kd_b[s, p], v_pair[s, p], _TN, preferred_element_type=F32) for (s, p) in pairs}
    yield
    for (s, p) in pairs:
        dr = jnp.where(low_rows, kv[s, p][:, :RT_DV], kv[s, p][:, RT_DV:])
        set_r(s, p, rdec_ref[p] * r_pair[s, p] + dr)
    sc_b = {(s, p, i): (sc[s, p, i] * mdec_ref[2 * p + i]).astype(BF16) for (s, p, i) in insts}
    yield
    o = {(s, p, i): jnp.dot(sc_b[s, p, i], v_pair[s, p][:, i * RT_DV:(i + 1) * RT_DV],
                            preferred_element_type=F32) + inter[s, p, i] for (s, p, i) in insts}
    yield
    mu = {k: jnp.mean(o[k], axis=-1, keepdims=True) for k in insts}
    cen = {k: o[k] - mu[k] for k in insts}
    var = {k: jnp.mean(cen[k] * cen[k], axis=-1, keepdims=True) for k in insts}
    yield
    for (s, p, i) in insts:
        h = 2 * p + i
        gate = _silu(proj_ref[rows[s], _cols(OFF_RT_GATE, h, RT_DV)].astype(BF16))
        on = cen[s, p, i] * lax.rsqrt(var[s, p, i] + EPS) * rt_g_ref[:, _cols(0, h, RT_DV)]
        mixed_ref[rows[s], _cols(HG_W, h, RT_DV)] = on.astype(BF16) * gate


def _cross_attention(n_streams, ts, proj_ref, mixed_ref, get_mem):
    xa_scale = XA_DH ** -0.5
    insts = [(s, h) for s in range(n_streams) for h in range(XA_H)]
    rows = {s: slice(s * ts, (s + 1) * ts) for s in range(n_streams)}
    mem = {k: get_mem(*k) for k in insts}
    sc = {(s, h): lax.dot_general(proj_ref[rows[s], _cols(OFF_XA_Q, h, XA_DH)].astype(BF16), mem[s, h][0], _NT,
                                  preferred_element_type=F32) * xa_scale for (s, h) in insts}
    yield
    mx = {k: jnp.max(sc[k], axis=-1, keepdims=True) for k in insts}
    yield
    pr = {k: jnp.exp(sc[k] - mx[k]) for k in insts}
    den = {k: jnp.sum(pr[k], axis=-1, keepdims=True) for k in insts}
    yield
    o = {k: jnp.dot(pr[k].astype(BF16), mem[k][1], preferred_element_type=F32) for k in insts}
    yield
    for (s, h) in insts:
        gate = _silu(proj_ref[rows[s], _cols(OFF_XA_GATE, h, XA_DH)].astype(BF16))
        mixed_ref[rows[s], _cols(HG_W + RT_W, h, XA_DH)] = (o[s, h] / den[s, h]).astype(BF16) * gate


def _chunk_cumsum(g):
    tokens, c = g.shape
    x = g.reshape(tokens // SUBLANES, SUBLANES, c)
    sub = lax.broadcasted_iota(jnp.int32, x.shape, 1)
    shift = 1
    while shift < SUBLANES:
        x = x + jnp.where(sub >= shift, pltpu.roll(x, shift, 1), 0.0)
        shift *= 2
    tiles = CHUNK // SUBLANES
    x = x.reshape(tokens // CHUNK, tiles, SUBLANES, c)
    out = [x[:, 0]]
    for j in range(1, tiles):
        total = out[-1][:, SUBLANES - 1:SUBLANES, :]
        out.append(x[:, j] + total)
    return jnp.stack(out, axis=1).reshape(tokens, c)


class _Projection:
    SEGMENTS = ((OFF_HG_F, OFF_HG_I), (OFF_HG_Q, OFF_HG_F), (OFF_HG_I, OFF_RT_Q), (OFF_RT_Q, D_IN))
    TILES_GATES = HG_W // PROJ_N_TILE
    TILES_HGRN = 4 * HG_W // PROJ_N_TILE
    TILES_ALL = D_IN // PROJ_N_TILE

    def __init__(self, x, norm_g_ref, w_in_ref, xn_ref, proj_ref):
        xn_ref[...] = _rms(x, norm_g_ref[...]).astype(BF16)
        self._refs = (xn_ref, w_in_ref, proj_ref)
        self._pending = [n0 for lo, hi in self.SEGMENTS for n0 in range(lo, hi, PROJ_N_TILE)]
        assert len(self._pending) == self.TILES_ALL
        self._done = 0

    def emit(self, n=1):
        xn_ref, w_in_ref, proj_ref = self._refs
        for _ in range(min(n, len(self._pending))):
            cols = slice(self._pending[0], self._pending.pop(0) + PROJ_N_TILE)
            proj_ref[:, cols] = jnp.dot(xn_ref[...], w_in_ref[:, cols], preferred_element_type=F32)
            self._done += 1

    def require(self, n_tiles):
        self.emit(n_tiles - self._done)


def _mix(n_streams, ts, x_ref, y_ref, cos_t, sin_t, mdec_ref, qdec_ref, kdec_ref, rdec_ref,
         get_mem, lb_ref, hg_g_ref, rt_g_ref, w_out_ref, fin_g_ref,
         proj_ref, gc_ref, mixed_ref, get_st, set_st, get_r, set_r, proj):
    proj.require(proj.TILES_GATES)
    lg = lb_ref[...]
    e = jnp.exp(lg - jnp.max(lg, axis=0, keepdims=True))
    lb = e[0:1, :] / jnp.sum(e, axis=0, keepdims=True)
    for c0 in range(0, HG_W, 2 * LANES):
        cols = slice(OFF_HG_F + c0, OFF_HG_F + c0 + 2 * LANES)
        lbc = lb[:, c0:c0 + 2 * LANES]
        half_span = 0.5 * (1.0 - lbc)
        bt = half_span * jnp.tanh(0.5 * proj_ref[:, cols])
        g = jnp.log2(lbc + (half_span + bt))
        proj_ref[:, cols] = half_span - bt
        gc_ref[:, c0:c0 + 2 * LANES] = _chunk_cumsum(g)
        proj.emit(PROJ_TILES_PER_GATE_GROUP)

    proj.require(proj.TILES_HGRN)
    row = lax.broadcasted_iota(jnp.int32, (CHUNK, LANES), 0)
    col = lax.broadcasted_iota(jnp.int32, (CHUNK, LANES), 1)
    half = CHUNK // 2
    causal = (((col <= row) & (col // half == row // half))
              | ((col >= CHUNK) & (col < CHUNK + half) & (row >= half)))
    for c in range(ts // CHUNK):
        chunks = [(s, slice(s * ts + c * CHUNK, s * ts + (c + 1) * CHUNK)) for s in range(n_streams)]
        for i in range(0, n_streams, HGRN_STREAMS_PER_STAGE):
            _hgrn_chunks(proj_ref, gc_ref, mixed_ref, hg_g_ref, chunks[i:i + HGRN_STREAMS_PER_STAGE], causal,
                         get_st, set_st, proj)

    proj.require(proj.TILES_ALL)

    def cross_attention_and_its_rows():
        yield from _cross_attention(n_streams, ts, proj_ref, mixed_ref, get_mem)
        yield from _out_projection(mixed_ref, w_out_ref, y_ref, HG_W + RT_W, D_MIX, y_ref)

    _run_interleaved(
        _retention(n_streams, ts, proj_ref, mixed_ref, cos_t, sin_t, mdec_ref, qdec_ref, kdec_ref, rdec_ref,
                   rt_g_ref, get_r, set_r),
        cross_attention_and_its_rows(),
        _out_projection(mixed_ref, w_out_ref, y_ref, 0, HG_W, x_ref))
    sumsq = []
    _run_interleaved(_out_projection(mixed_ref, w_out_ref, y_ref, HG_W, HG_W + RT_W, y_ref, sumsq))

    inv = lax.rsqrt(sum(sumsq) * (1.0 / D_MODEL) + EPS)
    y_ref[...] = y_ref[...] * inv * fin_g_ref[...]


def _out_projection(mixed_ref, w_out_ref, acc_ref, k_lo, k_hi, base_ref, sumsq=None):
    for n0 in range(0, D_MODEL, OUT_N_TILE):
        cols = slice(n0, n0 + OUT_N_TILE)
        out = base_ref[:, cols] + jnp.dot(mixed_ref[:, k_lo:k_hi], w_out_ref[k_lo:k_hi, cols],
                                          preferred_element_type=F32)
        acc_ref[:, cols] = out
        if sumsq is not None:
            sumsq.append(jnp.sum(out * out, axis=-1, keepdims=True))
        yield


class _StackedRows:
    def __init__(self, refs):
        self._refs = refs
        self._rows = refs[0].shape[0]

    def __getitem__(self, idx):
        cols = idx[1] if isinstance(idx, tuple) else slice(None)
        return jnp.concatenate([r[:, cols] for r in self._refs], axis=0)

    def __setitem__(self, idx, value):
        cols = idx[1] if isinstance(idx, tuple) else slice(None)
        for s, r in enumerate(self._refs):
            r[:, cols] = value[s * self._rows:(s + 1) * self._rows]


def _run_interleaved(*stage_generators):
    pending = list(stage_generators)
    while pending:
        for g in list(pending):
            try:
                next(g)
            except StopIteration:
                pending.remove(g)


def _memory_kv_kernel(mem_ref, g_ref, wk_ref, wv_ref, k_ref, v_ref, kb_ref, vb_ref):
    m = _rms(mem_ref[0], g_ref[...]).astype(BF16)
    k = jnp.dot(m, wk_ref[...].astype(BF16), preferred_element_type=F32)
    v = jnp.dot(m, wv_ref[...].astype(BF16), preferred_element_type=F32)
    for h in range(XA_H):
        k_ref[0, pl.ds(h, N_MEM, stride=XA_H), :] = k[:, _cols(0, h, XA_DH)]
        v_ref[0, pl.ds(h, N_MEM, stride=XA_H), :] = v[:, _cols(0, h, XA_DH)]
    kb_ref[0] = k.astype(BF16)
    vb_ref[0] = v.astype(BF16)


def _prompt_kernel(steps_per_stream, w_in_slab_ref, w_out_slab_ref,
                   x_ref, cos_ref, sin_ref, mdec_ref, qdec_ref, kdec_ref, rdec_ref, mk_ref, mv_ref,
                   norm_g_ref, lb_ref, hg_g_ref, rt_g_ref, fin_g_ref,
                   y_ref, s_out_ref, r_out_ref, w_in_b_slab_ref, w_out_b_slab_ref,
                   proj_ref, xn_ref, gc_ref, mixed_ref, st_ref, w_in_ref, w_out_ref):
    n_streams = x_ref.shape[0]
    t = pl.program_id(0)
    j = t - WEIGHT_PREP_STEPS
    running = t >= WEIGHT_PREP_STEPS

    @pl.when(jnp.logical_not(running))
    def _():
        for slab_ref, out_ref, full_ref in ((w_in_slab_ref, w_in_b_slab_ref, w_in_ref),
                                            (w_out_slab_ref, w_out_b_slab_ref, w_out_ref)):
            n = slab_ref.shape[0]
            slab = slab_ref[...].astype(BF16)
            full_ref[pl.ds(pl.multiple_of(t * n, n), n), :] = slab
            out_ref[...] = slab

    @pl.when(running & (j == 0))
    def _():
        st_ref[...] = jnp.zeros_like(st_ref)
        r_out_ref[...] = jnp.zeros_like(r_out_ref)

    def set_st(s, h, val):
        st_ref[s, h] = val

    def set_r(s, p, val):
        r_out_ref[s, p] = val

    def get_mem(s, h):
        return mk_ref[s, :, _cols(0, h, XA_DH)], mv_ref[s, :, _cols(0, h, XA_DH)]

    def token_block(i, carry):
        rows = pl.ds(pl.multiple_of(i * TOKEN_BLOCK, TOKEN_BLOCK), TOKEN_BLOCK)
        x_blk = _StackedRows([x_ref.at[s, rows] for s in range(n_streams)])
        y_blk = _StackedRows([y_ref.at[s, rows] for s in range(n_streams)])
        cos_t = jnp.concatenate([cos_ref[rows, :]] * n_streams, axis=0)
        sin_t = jnp.concatenate([sin_ref[rows, :]] * n_streams, axis=0)
        proj = _Projection(x_blk[...], norm_g_ref, w_in_ref, xn_ref, proj_ref)
        _mix(
            n_streams, TOKEN_BLOCK, x_blk, y_blk, cos_t, sin_t,
            mdec_ref, qdec_ref, kdec_ref, rdec_ref,
            get_mem, lb_ref, hg_g_ref, rt_g_ref, w_out_ref, fin_g_ref,
            proj_ref, gc_ref, mixed_ref,
            lambda s, h: st_ref[s, h], set_st, lambda s, p: r_out_ref[s, p], set_r, proj)
        return carry

    @pl.when(running)
    def _():
        lax.fori_loop(0, x_ref.shape[1] // TOKEN_BLOCK, token_block, 0)

    @pl.when(running & (j == steps_per_stream - 1))
    def _():
        for s in range(n_streams):
            for h in range(HG_H):
                s_out_ref[s, h] = st_ref[s, h].T


def _sample_kernel(x_ref, cos_ref, sin_ref, mdec_ref, qdec_ref, kdec_ref, rdec_ref, mk_ref, mv_ref,
                   s_in_ref, r_in_ref,
                   norm_g_ref, w_in_ref, lb_ref, hg_g_ref, rt_g_ref, w_out_ref, fin_g_ref,
                   y_ref, s_out_ref, r_out_ref, proj_ref, xn_ref, gc_ref, mixed_ref):
    n_streams = TOKEN_BLOCK // CHUNK

    def set_st(s, h, val):
        s_out_ref[s, h] = val.T

    def set_r(s, p, val):
        r_out_ref[s, p] = val

    def get_mem(s, h):
        head_rows = pl.ds(h, N_MEM, stride=XA_H)
        return mk_ref[s, head_rows, :].astype(BF16), mv_ref[s, head_rows, :].astype(BF16)

    cos_t = jnp.concatenate([cos_ref[...]] * n_streams, axis=0)
    sin_t = jnp.concatenate([sin_ref[...]] * n_streams, axis=0)
    proj = _Projection(x_ref[...], norm_g_ref, w_in_ref, xn_ref, proj_ref)
    _mix(
        n_streams, CHUNK, x_ref, y_ref, cos_t, sin_t, mdec_ref, qdec_ref, kdec_ref, rdec_ref,
        get_mem, lb_ref, hg_g_ref, rt_g_ref, w_out_ref, fin_g_ref,
        proj_ref, gc_ref, mixed_ref,
        lambda s, h: s_in_ref[s, h].T, set_st, lambda s, p: r_in_ref[s, p], set_r, proj)


def _const_spec(shape):
    zeros = (0,) * len(shape)
    return pl.BlockSpec(shape, lambda *_: zeros, pipeline_mode=pl.Buffered(1))


def _layer_scratch(rows):
    return [pltpu.VMEM((rows, D_IN), F32),
            pltpu.VMEM((rows, D_MODEL), BF16),
            pltpu.VMEM((rows, HG_W), F32),
            pltpu.VMEM((rows, D_MIX), BF16)]


def _memory_kv(mem, g, wk, wv):
    b = mem.shape[0]
    out_f = jax.ShapeDtypeStruct((b, N_MEM * XA_H, XA_DH), F32)
    out_b = jax.ShapeDtypeStruct((b, N_MEM, XA_W), BF16)
    blk_f = pl.BlockSpec((1, N_MEM * XA_H, XA_DH), lambda i: (i, 0, 0))
    blk_b = pl.BlockSpec((1, N_MEM, XA_W), lambda i: (i, 0, 0))
    return pl.pallas_call(
        _memory_kv_kernel,
        grid=(b,),
        in_specs=[pl.BlockSpec((1, N_MEM, D_MODEL), lambda i: (i, 0, 0)),
                  pl.BlockSpec((1, D_MODEL), lambda i: (0, 0)),
                  pl.BlockSpec((D_MODEL, XA_W), lambda i: (0, 0)),
                  pl.BlockSpec((D_MODEL, XA_W), lambda i: (0, 0))],
        out_specs=[blk_f, blk_f, blk_b, blk_b],
        out_shape=[out_f, out_f, out_b, out_b],
        name="memory_kv",
    )(mem, g.reshape(1, D_MODEL), wk, wv)


def _prompt_layer(x, mk_b, mv_b, w_in, w_out, params):
    b, t, _ = x.shape
    tb = PROMPT_BLOCKS_PER_STEP * TOKEN_BLOCK
    spp = t // tb
    prep = WEIGHT_PREP_STEPS
    cos_t, sin_t = _rope_tables(np.arange(t))
    mdec, qdec, kdec, rdec = _retention_tables(TOKEN_BLOCK)
    consts = (mdec, qdec, kdec, rdec)
    in_rows, out_rows = D_MODEL // prep, D_MIX // prep

    def slab(step):
        return jnp.minimum(step, prep - 1)

    def run(step):
        return jnp.maximum(step - prep, 0)

    in_specs = ([pl.BlockSpec((in_rows, D_IN), lambda s: (slab(s), 0)),
                 pl.BlockSpec((out_rows, D_MODEL), lambda s: (slab(s), 0)),
                 pl.BlockSpec((b, tb, D_MODEL), lambda s: (0, run(s), 0)),
                 pl.BlockSpec((tb, LANES), lambda s: (run(s), 0)),
                 pl.BlockSpec((tb, LANES), lambda s: (run(s), 0))]
                + [_const_spec(c.shape) for c in consts]
                + [_const_spec(mk_b.shape), _const_spec(mv_b.shape)]
                + [_const_spec(p.shape) for p in params])
    out_shape = [jax.ShapeDtypeStruct((b, t, D_MODEL), F32),
                 jax.ShapeDtypeStruct((b, HG_H, HG_DK, HG_DV), F32),
                 jax.ShapeDtypeStruct((b, RT_H // 2, 2 * RT_DK, RT_DV), F32),
                 jax.ShapeDtypeStruct((D_MODEL, D_IN), BF16),
                 jax.ShapeDtypeStruct((D_MIX, D_MODEL), BF16)]
    out_specs = [pl.BlockSpec((b, tb, D_MODEL), lambda s: (0, run(s), 0)),
                 pl.BlockSpec((b, HG_H, HG_DK, HG_DV), lambda s: (0, 0, 0, 0)),
                 pl.BlockSpec((b, RT_H // 2, 2 * RT_DK, RT_DV), lambda s: (0, 0, 0, 0)),
                 pl.BlockSpec((in_rows, D_IN), lambda s: (slab(s), 0)),
                 pl.BlockSpec((out_rows, D_MODEL), lambda s: (slab(s), 0))]
    return pl.pallas_call(
        functools.partial(_prompt_kernel, spp),
        grid=(prep + spp,),
        in_specs=in_specs,
        out_specs=out_specs,
        out_shape=out_shape,
        scratch_shapes=_layer_scratch(b * TOKEN_BLOCK) + [
                                           pltpu.VMEM((b, HG_H, HG_DV, HG_DK), F32),
                                           pltpu.VMEM((D_MODEL, D_IN), BF16),
                                           pltpu.VMEM((D_MIX, D_MODEL), BF16)],
        compiler_params=pltpu.CompilerParams(dimension_semantics=("arbitrary",),
                                             vmem_limit_bytes=VMEM_LIMIT_BYTES),
        name="prompt_layer",
    )(w_in, w_out, x, cos_t, sin_t, *consts, mk_b, mv_b, *params)


def _sample_layer(x, s_hg, s_rt, mem_k, mem_v, weights):
    b, t, _ = x.shape
    assert t == CHUNK
    tb = TOKEN_BLOCK
    nsb = tb // t
    cos_t, sin_t = _rope_tables(PAST_LEN + np.arange(t))
    mdec, qdec, kdec, rdec = _retention_tables(t)
    consts = (mdec, qdec, kdec, rdec)
    s_rt2 = s_rt.reshape(b, RT_H // 2, 2 * RT_DK, RT_DV)
    mem_spec = pl.BlockSpec((nsb, N_MEM * XA_H, XA_DH), lambda i: (i, 0, 0))
    in_specs = ([pl.BlockSpec((tb, D_MODEL), lambda i: (i, 0)),
                 _const_spec(cos_t.shape), _const_spec(sin_t.shape)]
                + [_const_spec(c.shape) for c in consts]
                + [mem_spec, mem_spec,
                   pl.BlockSpec((nsb, HG_H, HG_DK, HG_DV), lambda i: (i, 0, 0, 0)),
                   pl.BlockSpec((nsb, RT_H // 2, 2 * RT_DK, RT_DV), lambda i: (i, 0, 0, 0))]
                + [_const_spec(w.shape) for w in weights])
    out_shape = [jax.ShapeDtypeStruct((b * t, D_MODEL), F32),
                 jax.ShapeDtypeStruct((b, HG_H, HG_DK, HG_DV), F32),
                 jax.ShapeDtypeStruct((b, RT_H // 2, 2 * RT_DK, RT_DV), F32)]
    out_specs = [pl.BlockSpec((tb, D_MODEL), lambda i: (i, 0)),
                 pl.BlockSpec((nsb, HG_H, HG_DK, HG_DV), lambda i: (i, 0, 0, 0)),
                 pl.BlockSpec((nsb, RT_H // 2, 2 * RT_DK, RT_DV), lambda i: (i, 0, 0, 0))]
    y, s_new, r_new = pl.pallas_call(
        _sample_kernel,
        grid=(b // nsb,),
        in_specs=in_specs,
        out_specs=out_specs,
        out_shape=out_shape,
        scratch_shapes=_layer_scratch(tb),
        compiler_params=pltpu.CompilerParams(dimension_semantics=("arbitrary",),
                                             vmem_limit_bytes=VMEM_LIMIT_BYTES),
        name="sample_layer",
    )(x.reshape(b * t, D_MODEL), cos_t, sin_t, *consts,
      mem_k.reshape(b, N_MEM * XA_H, XA_DH), mem_v.reshape(b, N_MEM * XA_H, XA_DH), s_hg, s_rt2, *weights)
    return y.reshape(b, t, D_MODEL), s_new, r_new


def kernel(x_prompt, x_sample, mem_prompt, state_hgrn, state_ret, cache_mem_k, cache_mem_v, norm_g, w_in,
           lb_logits, hg_norm_g, rt_norm_g, mem_norm_g, w_mem_k, w_mem_v, w_out, final_norm_g):
    depth = w_in.shape[0]
    assert depth == 1, "single-layer step"
    bp = x_prompt.shape[0]
    bs = x_sample.shape[0]
    g_in, g_hg, g_rt = norm_g[0].reshape(1, D_MODEL), hg_norm_g[0].reshape(1, HG_W), rt_norm_g[0].reshape(1, RT_W)
    g_fin = final_norm_g.reshape(1, D_MODEL)
    mk, mv, mk_b, mv_b = _memory_kv(mem_prompt, mem_norm_g[0], w_mem_k[0], w_mem_v[0])
    y_p, s_p, r_p, w_in_b, w_out_b = _prompt_layer(x_prompt, mk_b, mv_b, w_in[0], w_out[0],
                                                   (g_in, lb_logits, g_hg, g_rt, g_fin))
    y_s, s_s, r_s = _sample_layer(x_sample, state_hgrn[0], state_ret[0], cache_mem_k[0], cache_mem_v[0],
                                  (g_in, w_in_b, lb_logits, g_hg, g_rt, w_out_b, g_fin))
    return (y_p, y_s,
            s_p.reshape(1, bp, HG_H, HG_DK, HG_DV),
            r_p.reshape(1, bp, RT_H, RT_DK, RT_DV),
            mk.reshape(1, bp, N_MEM, XA_H, XA_DH),
            mv.reshape(1, bp, N_MEM, XA_H, XA_DH),
            s_s.reshape(1, bs, HG_H, HG_DK, HG_DV),
            r_s.reshape(1, bs, RT_H, RT_DK, RT_DV))
```

```python
import functools

import numpy as np
import jax
import jax.numpy as jnp
from jax import lax
from jax.experimental import pallas as pl
from jax.experimental.pallas import tpu as pltpu

F32 = jnp.float32
BF16 = jnp.bfloat16

D_MODEL = 1024
PAST_LEN = 4096
CHUNK = 64
N_MEM = 256
EPS = 1e-6
ROPE_BASE = 10000.0
HG_H, HG_DK, HG_DV = 8, 128, 128
HG_W = HG_H * HG_DV
RT_H, RT_DK, RT_DV = 4, 64, 128
RT_QK = RT_H * RT_DK
RT_W = RT_H * RT_DV
XA_H, XA_DH = 4, 128
XA_W = XA_H * XA_DH
D_MIX = HG_W + RT_W + XA_W
OFF_HG_Q = 0
OFF_HG_F = OFF_HG_Q + HG_W
OFF_HG_I = OFF_HG_F + HG_W
OFF_HG_GATE = OFF_HG_I + HG_W
OFF_RT_Q = OFF_HG_GATE + HG_W
OFF_RT_K = OFF_RT_Q + RT_QK
OFF_RT_V = OFF_RT_K + RT_QK
OFF_RT_GATE = OFF_RT_V + RT_W
OFF_XA_Q = OFF_RT_GATE + RT_W
OFF_XA_GATE = OFF_XA_Q + XA_W
D_IN = OFF_XA_GATE + XA_W

LANES = 128
SUBLANES = 8
TOKEN_BLOCK = 256
PROMPT_BLOCKS_PER_STEP = 1
WEIGHT_PREP_STEPS = 8
PROJ_N_TILE = 512
PROJ_TILES_PER_GATE_GROUP = 1
PROJ_TILES_PER_HGRN_STAGE = 1
HGRN_STREAMS_PER_STAGE = 4
OUT_N_TILE = 512
VMEM_LIMIT_BYTES = 60 * 1024 * 1024
_NT = (((1,), (1,)), ((), ()))
_TN = (((0,), (0,)), ((), ()))


def _silu(x):
    h = 0.5 * x
    return h + h * jnp.tanh(h)


def _rms(x, g):
    return x * lax.rsqrt(jnp.mean(x * x, axis=-1, keepdims=True) + EPS) * g


def _cols(off, i, width):
    return slice(off + i * width, off + (i + 1) * width)


def _rope_tables(pos):
    half = RT_DK // 2
    inv_freq = ROPE_BASE ** (-np.arange(half, dtype=np.float64) / half)
    ang = np.asarray(pos, np.float64)[:, None] * inv_freq[None, :]
    cos, sin = np.cos(ang), np.sin(ang)
    cos_t = np.concatenate([cos, cos, cos, cos], axis=-1)
    sin_t = np.concatenate([-sin, sin, -sin, sin], axis=-1)
    return jnp.asarray(cos_t, F32), jnp.asarray(sin_t, F32)


def _retention_tables(ts):
    h = np.arange(RT_H, dtype=np.float64)
    log_gamma = np.log1p(-np.exp2(-5.0 - h))
    n = np.arange(ts, dtype=np.float64)
    scale = RT_DK ** -0.5
    dist = np.abs(n[:, None] - n[None, :])
    visible = (n[None, :] // CHUNK) <= (n[:, None] // CHUNK)
    mdec = scale * np.exp(log_gamma[:, None, None] * dist[None]) * visible[None]
    qd = np.exp(log_gamma[:, None] * (n[None, :] + 1.0))
    kd = scale * np.exp(log_gamma[:, None] * (ts - 1.0 - n[None, :]))
    rd = np.exp(log_gamma * ts)

    def pair_lanes(a):
        a = np.repeat(a[:, :, None], RT_DK, axis=2)
        return np.concatenate([a[0::2], a[1::2]], axis=2)

    rdec = np.repeat(np.repeat(rd[:, None], RT_DK, axis=1).reshape(RT_H // 2, 2 * RT_DK, 1), LANES, axis=2)
    return (jnp.asarray(mdec, F32), jnp.asarray(pair_lanes(qd), F32), jnp.asarray(pair_lanes(kd), F32),
            jnp.asarray(rdec, F32))


def _hgrn_chunks(proj_ref, gc_ref, mixed_ref, hg_g_ref, chunks, score_mask, get_st, set_st, proj):
    half = CHUNK // 2
    zero_half = jnp.zeros((half, HG_DK), BF16)
    insts = [(s, h) for s, _ in chunks for h in range(HG_H)]
    rows = dict(chunks)
    q_rel, k_rel, q_abs, k_end, v_b, dec = {}, {}, {}, {}, {}, {}
    for (s, h) in insts:
        q = _silu(proj_ref[rows[s], _cols(OFF_HG_Q, h, HG_DK)])
        k = proj_ref[rows[s], _cols(OFF_HG_F, h, HG_DK)]
        gc = gc_ref[rows[s], _cols(0, h, HG_DK)]
        g_end = gc[CHUNK - 1:CHUNK, :]
        (q1, q2), (k1, k2), (g1, g2) = ((a[:half], a[half:]) for a in (q, k, gc))
        g_m1, g_m2 = gc[half // 2 - 1:half // 2, :], gc[half + half // 2 - 1:half + half // 2, :]
        g_b = gc[half - 1:half, :]
        qd1, kd1 = (q1 * jnp.exp2(g1 - g_m1)).astype(BF16), (k1 * jnp.exp2(g_m1 - g1)).astype(BF16)
        qd2, kd2 = (q2 * jnp.exp2(g2 - g_m2)).astype(BF16), (k2 * jnp.exp2(g_m2 - g2)).astype(BF16)
        qo, ko = (q2 * jnp.exp2(g2 - g_b)).astype(BF16), (k1 * jnp.exp2(g_b - g1)).astype(BF16)
        q_rel[s, h] = jnp.concatenate([jnp.concatenate([qd1, qd2], axis=0),
                                       jnp.concatenate([zero_half, qo], axis=0)], axis=1)
        k_rel[s, h] = jnp.concatenate([jnp.concatenate([kd1, kd2, zero_half, zero_half], axis=0),
                                       jnp.concatenate([zero_half, zero_half, ko, zero_half], axis=0)],
                                      axis=1)
        q_abs[s, h] = (q * jnp.exp2(gc)).astype(BF16)
        k_end[s, h] = (k * jnp.exp2(g_end - gc)).astype(BF16)
        v_b[s, h] = proj_ref[rows[s], _cols(OFF_HG_I, h, HG_DV)].astype(BF16)
        dec[s, h] = jnp.exp2(g_end)
    proj.emit(PROJ_TILES_PER_HGRN_STAGE)
    a = {k: lax.dot_general(q_rel[k], k_rel[k], _NT, preferred_element_type=F32) for k in insts}
    st = {k: get_st(*k) for k in insts}
    v_t = {k: jnp.concatenate([v_b[k], v_b[k][:half], zero_half], axis=0).T for k in insts}
    kv = {k: jnp.dot(v_t[k][:, :CHUNK], k_end[k], preferred_element_type=F32) for k in insts}
    for k in insts:
        set_st(*k, dec[k] * st[k] + kv[k])
    a_m = {k: jnp.where(score_mask, a[k], 0.0).astype(BF16) for k in insts}
    o = {k: lax.dot_general(jnp.concatenate([q_abs[k], a_m[k]], axis=1),
                            jnp.concatenate([st[k].astype(BF16), v_t[k]], axis=1), _NT, preferred_element_type=F32)
         for k in insts}
    ms = {k: jnp.mean(o[k] * o[k], axis=-1, keepdims=True) for k in insts}
    for (s, h) in insts:
        gate = _silu(proj_ref[rows[s], _cols(OFF_HG_GATE, h, HG_DV)].astype(BF16))
        on = o[s, h] * lax.rsqrt(ms[s, h] + EPS) * hg_g_ref[:, _cols(0, h, HG_DV)]
        mixed_ref[rows[s], _cols(0, h, HG_DV)] = on.astype(BF16) * gate


def _retention(n_streams, ts, proj_ref, mixed_ref, cos_t, sin_t, mdec_ref, qdec_ref, kdec_ref, rdec_ref,
               rt_g_ref, get_r, set_r):
    lane = lax.broadcasted_iota(jnp.int32, (ts, LANES), 1)
    first_half = (lane % RT_DK) < (RT_DK // 2)
    low_head = lane < RT_DK
    low_rows = lax.broadcasted_iota(jnp.int32, (2 * RT_DK, RT_DV), 0) < RT_DK

    def rope(xs, cs, sn):
        swapped = jnp.where(first_half, pltpu.roll(xs, LANES - RT_DK // 2, 1), pltpu.roll(xs, RT_DK // 2, 1))
        return xs * cs + swapped * sn

    pairs = [(s, p) for s in range(n_streams) for p in range(RT_H // 2)]
    insts = [(s, p, i) for (s, p) in pairs for i in range(2)]
    rows = {s: slice(s * ts, (s + 1) * ts) for s in range(n_streams)}
    qp, kp_b, qd, kd_b, r_pair, v_pair = {}, {}, {}, {}, {}, {}
    for (s, p) in pairs:
        cs, sn = cos_t[rows[s], :], sin_t[rows[s], :]
        q = rope(proj_ref[rows[s], _cols(OFF_RT_Q, p, LANES)], cs, sn)
        k = rope(proj_ref[rows[s], _cols(OFF_RT_K, p, LANES)], cs, sn)
        qp[s, p] = q
        qd[s, p] = q * qdec_ref[p]
        kp_b[s, p] = k.astype(BF16)
        kd_b[s, p] = (k * kdec_ref[p]).astype(BF16)
        r_pair[s, p] = get_r(s, p)
        v_pair[s, p] = proj_ref[rows[s], _cols(OFF_RT_V, p, 2 * RT_DV)].astype(BF16)
    yield
    sel = {0: low_head, 1: jnp.logical_not(low_head)}
    q_h = {(s, p, i): jnp.where(sel[i], qp[s, p], 0.0).astype(BF16) for (s, p, i) in insts}
    qd_h = {(s, p, i): jnp.where(sel[i], qd[s, p], 0.0).astype(BF16) for (s, p, i) in insts}
    sc = {(s, p, i): lax.dot_general(q_h[s, p, i], kp_b[s, p], _NT, preferred_element_type=F32)
          for (s, p, i) in insts}
    inter = {(s, p, i): jnp.dot(qd_h[s, p, i], r_pair[s, p].astype(BF16), preferred_element_type=F32)
             for (s, p, i) in insts}
    kv = {(s, p): lax.dot_general(kd_b[s, p], v_pair[s, p], _TN, preferred_element_type=F32) for (s, p) in pairs}
    yield
    for (s, p) in pairs:
        dr = jnp.where(low_rows, kv[s, p][:, :RT_DV], kv[s, p][:, RT_DV:])
        set_r(s, p, rdec_ref[p] * r_pair[s, p] + dr)
    sc_b = {(s, p, i): (sc[s, p, i] * mdec_ref[2 * p + i]).astype(BF16) for (s, p, i) in insts}
    yield
    o = {(s, p, i): jnp.dot(sc_b[s, p, i], v_pair[s, p][:, i * RT_DV:(i + 1) * RT_DV],
                            preferred_element_type=F32) + inter[s, p, i] for (s, p, i) in insts}
    yield
    mu = {k: jnp.mean(o[k], axis=-1, keepdims=True) for k in insts}
    cen = {k: o[k] - mu[k] for k in insts}
    var = {k: jnp.mean(cen[k] * cen[k], axis=-1, keepdims=True) for k in insts}
    yield
    for (s, p, i) in insts:
        h = 2 * p + i
        gate = _silu(proj_ref[rows[s], _cols(OFF_RT_GATE, h, RT_DV)].astype(BF16))
        on = cen[s, p, i] * lax.rsqrt(var[s, p, i] + EPS) * rt_g_ref[:, _cols(0, h, RT_DV)]
        mixed_ref[rows[s], _cols(HG_W, h, RT_DV)] = on.astype(BF16) * gate


def _cross_attention(n_streams, ts, proj_ref, mixed_ref, get_mem):
    xa_scale = XA_DH ** -0.5
    insts = [(s, h) for s in range(n_streams) for h in range(XA_H)]
    rows = {s: slice(s * ts, (s + 1) * ts) for s in range(n_streams)}
    mem = {k: get_mem(*k) for k in insts}
    sc = {(s, h): lax.dot_general(proj_ref[rows[s], _cols(OFF_XA_Q, h, XA_DH)].astype(BF16), mem[s, h][0], _NT,
                                  preferred_element_type=F32) * xa_scale for (s, h) in insts}
    yield
    mx = {k: jnp.max(sc[k], axis=-1, keepdims=True) for k in insts}
    yield
    pr = {k: jnp.exp(sc[k] - mx[k]) for k in insts}
    den = {k: jnp.sum(pr[k], axis=-1, keepdims=True) for k in insts}
    yield
    o = {k: jnp.dot(pr[k].astype(BF16), mem[k][1], preferred_element_type=F32) for k in insts}
    yield
    for (s, h) in insts:
        gate = _silu(proj_ref[rows[s], _cols(OFF_XA_GATE, h, XA_DH)].astype(BF16))
        mixed_ref[rows[s], _cols(HG_W + RT_W, h, XA_DH)] = (o[s, h] / den[s, h]).astype(BF16) * gate


def _chunk_cumsum(g):
    tokens, c = g.shape
    x = g.reshape(tokens // SUBLANES, SUBLANES, c)
    sub = lax.broadcasted_iota(jnp.int32, x.shape, 1)
    shift = 1
    while shift < SUBLANES:
        x = x + jnp.where(sub >= shift, pltpu.roll(x, shift, 1), 0.0)
        shift *= 2
    tiles = CHUNK // SUBLANES
    x = x.reshape(tokens // CHUNK, tiles, SUBLANES, c)
    out = [x[:, 0]]
    for j in range(1, tiles):
        total = out[-1][:, SUBLANES - 1:SUBLANES, :]
        out.append(x[:, j] + total)
    return jnp.stack(out, axis=1).reshape(tokens, c)


class _Projection:
    SEGMENTS = ((OFF_HG_F, OFF_HG_I), (OFF_HG_Q, OFF_HG_F), (OFF_HG_I, OFF_RT_Q), (OFF_RT_Q, D_IN))
    TILES_GATES = HG_W // PROJ_N_TILE
    TILES_HGRN = 4 * HG_W // PROJ_N_TILE
    TILES_ALL = D_IN // PROJ_N_TILE

    def __init__(self, x, norm_g_ref, w_in_ref, xn_ref, proj_ref):
        xn_ref[...] = _rms(x, norm_g_ref[...]).astype(BF16)
        self._refs = (xn_ref, w_in_ref, proj_ref)
        self._pending = [n0 for lo, hi in self.SEGMENTS for n0 in range(lo, hi, PROJ_N_TILE)]
        assert len(self._pending) == self.TILES_ALL
        self._done = 0

    def emit(self, n=1):
        xn_ref, w_in_ref, proj_ref = self._refs
        for _ in range(min(n, len(self._pending))):
            cols = slice(self._pending[0], self._pending.pop(0) + PROJ_N_TILE)
            proj_ref[:, cols] = jnp.dot(xn_ref[...], w_in_ref[:, cols], preferred_element_type=F32)
            self._done += 1

    def require(self, n_tiles):
        self.emit(n_tiles - self._done)


def _mix(n_streams, ts, x_ref, y_ref, cos_t, sin_t, mdec_ref, qdec_ref, kdec_ref, rdec_ref,
         get_mem, lb_ref, hg_g_ref, rt_g_ref, w_out_ref, fin_g_ref,
         proj_ref, gc_ref, mixed_ref, get_st, set_st, get_r, set_r, proj):
    proj.require(proj.TILES_GATES)
    lg = lb_ref[...]
    e = jnp.exp(lg - jnp.max(lg, axis=0, keepdims=True))
    lb = e[0:1, :] / jnp.sum(e, axis=0, keepdims=True)
    for c0 in range(0, HG_W, 2 * LANES):
        cols = slice(OFF_HG_F + c0, OFF_HG_F + c0 + 2 * LANES)
        lbc = lb[:, c0:c0 + 2 * LANES]
        half_span = 0.5 * (1.0 - lbc)
        bt = half_span * jnp.tanh(0.5 * proj_ref[:, cols])
        g = jnp.log2(lbc + (half_span + bt))
        proj_ref[:, cols] = half_span - bt
        gc_ref[:, c0:c0 + 2 * LANES] = _chunk_cumsum(g)
        proj.emit(PROJ_TILES_PER_GATE_GROUP)

    proj.require(proj.TILES_HGRN)
    row = lax.broadcasted_iota(jnp.int32, (CHUNK, LANES), 0)
    col = lax.broadcasted_iota(jnp.int32, (CHUNK, LANES), 1)
    half = CHUNK // 2
    causal = (((col <= row) & (col // half == row // half))
              | ((col >= CHUNK) & (col < CHUNK + half) & (row >= half)))
    for c in range(ts // CHUNK):
        chunks = [(s, slice(s * ts + c * CHUNK, s * ts + (c + 1) * CHUNK)) for s in range(n_streams)]
        for i in range(0, n_streams, HGRN_STREAMS_PER_STAGE):
            _hgrn_chunks(proj_ref, gc_ref, mixed_ref, hg_g_ref, chunks[i:i + HGRN_STREAMS_PER_STAGE], causal,
                         get_st, set_st, proj)

    proj.require(proj.TILES_ALL)

    def cross_attention_and_its_rows():
        yield from _cross_attention(n_streams, ts, proj_ref, mixed_ref, get_mem)
        yield from _out_projection(mixed_ref, w_out_ref, y_ref, HG_W + RT_W, D_MIX, y_ref)

    _run_interleaved(
        _retention(n_streams, ts, proj_ref, mixed_ref, cos_t, sin_t, mdec_ref, qdec_ref, kdec_ref, rdec_ref,
                   rt_g_ref, get_r, set_r),
        cross_attention_and_its_rows(),
        _out_projection(mixed_ref, w_out_ref, y_ref, 0, HG_W, x_ref))
    sumsq = []
    _run_interleaved(_out_projection(mixed_ref, w_out_ref, y_ref, HG_W, HG_W + RT_W, y_ref, sumsq))

    inv = lax.rsqrt(sum(sumsq) * (1.0 / D_MODEL) + EPS)
    y_ref[...] = y_ref[...] * inv * fin_g_ref[...]


def _out_projection(mixed_ref, w_out_ref, acc_ref, k_lo, k_hi, base_ref, sumsq=None):
    for n0 in range(0, D_MODEL, OUT_N_TILE):
        cols = slice(n0, n0 + OUT_N_TILE)
        out = base_ref[:, cols] + jnp.dot(mixed_ref[:, k_lo:k_hi], w_out_ref[k_lo:k_hi, cols],
                                          preferred_element_type=F32)
        acc_ref[:, cols] = out
        if sumsq is not None:
            sumsq.append(jnp.sum(out * out, axis=-1, keepdims=True))
        yield


class _StackedRows:
    def __init__(self, refs):
        self._refs = refs
        self._rows = refs[0].shape[0]

    def __getitem__(self, idx):
        cols = idx[1] if isinstance(idx, tuple) else slice(None)
        return jnp.concatenate([r[:, cols] for r in self._refs], axis=0)

    def __setitem__(self, idx, value):
        cols = idx[1] if isinstance(idx, tuple) else slice(None)
        for s, r in enumerate(self._refs):
            r[:, cols] = value[s * self._rows:(s + 1) * self._rows]


def _run_interleaved(*stage_generators):
    pending = list(stage_generators)
    while pending:
        for g in list(pending):
            try:
                next(g)
            except StopIteration:
                pending.remove(g)


def _memory_kv_kernel(mem_ref, g_ref, wk_ref, wv_ref, k_ref, v_ref, kb_ref, vb_ref):
    m = _rms(mem_ref[0], g_ref[...]).astype(BF16)
    k = jnp.dot(m, wk_ref[...].astype(BF16), preferred_element_type=F32)
    v = jnp.dot(m, wv_ref[...].astype(BF16), preferred_element_type=F32)
    for h in range(XA_H):
        k_ref[0, pl.ds(h, N_MEM, stride=XA_H), :] = k[:, _cols(0, h, XA_DH)]
        v_ref[0, pl.ds(h, N_MEM, stride=XA_H), :] = v[:, _cols(0, h, XA_DH)]
    kb_ref[0] = k.astype(BF16)
    vb_ref[0] = v.astype(BF16)


def _prompt_kernel(steps_per_stream, w_in_slab_ref, w_out_slab_ref,
                   x_ref, cos_ref, sin_ref, mdec_ref, qdec_ref, kdec_ref, rdec_ref, mk_ref, mv_ref,
                   norm_g_ref, lb_ref, hg_g_ref, rt_g_ref, fin_g_ref,
                   y_ref, s_out_ref, r_out_ref, w_in_b_slab_ref, w_out_b_slab_ref,
                   proj_ref, xn_ref, gc_ref, mixed_ref, st_ref, w_in_ref, w_out_ref):
    n_streams = x_ref.shape[0]
    t = pl.program_id(0)
    j = t - WEIGHT_PREP_STEPS
    running = t >= WEIGHT_PREP_STEPS

    @pl.when(jnp.logical_not(running))
    def _():
        for slab_ref, out_ref, full_ref in ((w_in_slab_ref, w_in_b_slab_ref, w_in_ref),
                                            (w_out_slab_ref, w_out_b_slab_ref, w_out_ref)):
            n = slab_ref.shape[0]
            slab = slab_ref[...].astype(BF16)
            full_ref[pl.ds(pl.multiple_of(t * n, n), n), :] = slab
            out_ref[...] = slab

    @pl.when(running & (j == 0))
    def _():
        st_ref[...] = jnp.zeros_like(st_ref)
        r_out_ref[...] = jnp.zeros_like(r_out_ref)

    def set_st(s, h, val):
        st_ref[s, h] = val

    def set_r(s, p, val):
        r_out_ref[s, p] = val

    def get_mem(s, h):
        return mk_ref[s, :, _cols(0, h, XA_DH)], mv_ref[s, :, _cols(0, h, XA_DH)]

    def token_block(i, carry):
        rows = pl.ds(pl.multiple_of(i * TOKEN_BLOCK, TOKEN_BLOCK), TOKEN_BLOCK)
        x_blk = _StackedRows([x_ref.at[s, rows] for s in range(n_streams)])
        y_blk = _StackedRows([y_ref.at[s, rows] for s in range(n_streams)])
        cos_t = jnp.concatenate([cos_ref[rows, :]] * n_streams, axis=0)
        sin_t = jnp.concatenate([sin_ref[rows, :]] * n_streams, axis=0)
        proj = _Projection(x_blk[...], norm_g_ref, w_in_ref, xn_ref, proj_ref)
        _mix(
            n_streams, TOKEN_BLOCK, x_blk, y_blk, cos_t, sin_t,
            mdec_ref, qdec_ref, kdec_ref, rdec_ref,
            get_mem, lb_ref, hg_g_ref, rt_g_ref, w_out_ref, fin_g_ref,
            proj_ref, gc_ref, mixed_ref,
            lambda s, h: st_ref[s, h], set_st, lambda s, p: r_out_ref[s, p], set_r, proj)
        return carry

    @pl.when(running)
    def _():
        lax.fori_loop(0, x_ref.shape[1] // TOKEN_BLOCK, token_block, 0)

    @pl.when(running & (j == steps_per_stream - 1))
    def _():
        for s in range(n_streams):
            for h in range(HG_H):
                s_out_ref[s, h] = st_ref[s, h].T


def _sample_kernel(x_ref, cos_ref, sin_ref, mdec_ref, qdec_ref, kdec_ref, rdec_ref, mk_ref, mv_ref,
                   s_in_ref, r_in_ref,
                   norm_g_ref, w_in_ref, lb_ref, hg_g_ref, rt_g_ref, w_out_ref, fin_g_ref,
                   y_ref, s_out_ref, r_out_ref, proj_ref, xn_ref, gc_ref, mixed_ref):
    n_streams = TOKEN_BLOCK // CHUNK

    def set_st(s, h, val):
        s_out_ref[s, h] = val.T

    def set_r(s, p, val):
        r_out_ref[s, p] = val

    def get_mem(s, h):
        head_rows = pl.ds(h, N_MEM, stride=XA_H)
        return mk_ref[s, head_rows, :].astype(BF16), mv_ref[s, head_rows, :].astype(BF16)

    cos_t = jnp.concatenate([cos_ref[...]] * n_streams, axis=0)
    sin_t = jnp.concatenate([sin_ref[...]] * n_streams, axis=0)
    proj = _Projection(x_ref[...], norm_g_ref, w_in_ref, xn_ref, proj_ref)
    _mix(
        n_streams, CHUNK, x_ref, y_ref, cos_t, sin_t, mdec_ref, qdec_ref, kdec_ref, rdec_ref,
        get_mem, lb_ref, hg_g_ref, rt_g_ref, w_out_ref, fin_g_ref,
        proj_ref, gc_ref, mixed_ref,
        lambda s, h: s_in_ref[s, h].T, set_st, lambda s, p: r_in_ref[s, p], set_r, proj)


def _const_spec(shape):
    zeros = (0,) * len(shape)
    return pl.BlockSpec(shape, lambda *_: zeros, pipeline_mode=pl.Buffered(1))


def _layer_scratch(rows):
    return [pltpu.VMEM((rows, D_IN), F32),
            pltpu.VMEM((rows, D_MODEL), BF16),
            pltpu.VMEM((rows, HG_W), F32),
            pltpu.VMEM((rows, D_MIX), BF16)]


def _memory_kv(mem, g, wk, wv):
    b = mem.shape[0]
    out_f = jax.ShapeDtypeStruct((b, N_MEM * XA_H, XA_DH), F32)
    out_b = jax.ShapeDtypeStruct((b, N_MEM, XA_W), BF16)
    blk_f = pl.BlockSpec((1, N_MEM * XA_H, XA_DH), lambda i: (i, 0, 0))
    blk_b = pl.BlockSpec((1, N_MEM, XA_W), lambda i: (i, 0, 0))
    return pl.pallas_call(
        _memory_kv_kernel,
        grid=(b,),
        in_specs=[pl.BlockSpec((1, N_MEM, D_MODEL), lambda i: (i, 0, 0)),
                  pl.BlockSpec((1, D_MODEL), lambda i: (0, 0)),
                  pl.BlockSpec((D_MODEL, XA_W), lambda i: (0, 0)),
                  pl.BlockSpec((D_MODEL, XA_W), lambda i: (0, 0))],
        out_specs=[blk_f, blk_f, blk_b, blk_b],
        out_shape=[out_f, out_f, out_b, out_b],
        name="memory_kv",
    )(mem, g.reshape(1, D_MODEL), wk, wv)


def _prompt_layer(x, mk_b, mv_b, w_in, w_out, params):
    b, t, _ = x.shape
    tb = PROMPT_BLOCKS_PER_STEP * TOKEN_BLOCK
    spp = t // tb
    prep = WEIGHT_PREP_STEPS
    cos_t, sin_t = _rope_tables(np.arange(t))
    mdec, qdec, kdec, rdec = _retention_tables(TOKEN_BLOCK)
    consts = (mdec, qdec, kdec, rdec)
    in_rows, out_rows = D_MODEL // prep, D_MIX // prep

    def slab(step):
        return jnp.minimum(step, prep - 1)

    def run(step):
        return jnp.maximum(step - prep, 0)

    in_specs = ([pl.BlockSpec((in_rows, D_IN), lambda s: (slab(s), 0)),
                 pl.BlockSpec((out_rows, D_MODEL), lambda s: (slab(s), 0)),
                 pl.BlockSpec((b, tb, D_MODEL), lambda s: (0, run(s), 0)),
                 pl.BlockSpec((tb, LANES), lambda s: (run(s), 0)),
                 pl.BlockSpec((tb, LANES), lambda s: (run(s), 0))]
                + [_const_spec(c.shape) for c in consts]
                + [_const_spec(mk_b.shape), _const_spec(mv_b.shape)]
                + [_const_spec(p.shape) for p in params])
    out_shape = [jax.ShapeDtypeStruct((b, t, D_MODEL), F32),
                 jax.ShapeDtypeStruct((b, HG_H, HG_DK, HG_DV), F32),
                 jax.ShapeDtypeStruct((b, RT_H // 2, 2 * RT_DK, RT_DV), F32),
                 jax.ShapeDtypeStruct((D_MODEL, D_IN), BF16),
                 jax.ShapeDtypeStruct((D_MIX, D_MODEL), BF16)]
    out_specs = [pl.BlockSpec((b, tb, D_MODEL), lambda s: (0, run(s), 0)),
                 pl.BlockSpec((b, HG_H, HG_DK, HG_DV), lambda s: (0, 0, 0, 0)),
                 pl.BlockSpec((b, RT_H // 2, 2 * RT_DK, RT_DV), lambda s: (0, 0, 0, 0)),
                 pl.BlockSpec((in_rows, D_IN), lambda s: (slab(s), 0)),
                 pl.BlockSpec((out_rows, D_MODEL), lambda s: (slab(s), 0))]
    return pl.pallas_call(
        functools.partial(_prompt_kernel, spp),
        grid=(prep + spp,),
        in_specs=in_specs,
        out_specs=out_specs,
        out_shape=out_shape,
        scratch_shapes=_layer_scratch(b * TOKEN_BLOCK) + [
                                           pltpu.VMEM((b, HG_H, HG_DV, HG_DK), F32),
                                           pltpu.VMEM((D_MODEL, D_IN), BF16),
                                           pltpu.VMEM((D_MIX, D_MODEL), BF16)],
        compiler_params=pltpu.CompilerParams(dimension_semantics=("arbitrary",),
                                             vmem_limit_bytes=VMEM_LIMIT_BYTES),
        name="prompt_layer",
    )(w_in, w_out, x, cos_t, sin_t, *consts, mk_b, mv_b, *params)


def _sample_layer(x, s_hg, s_rt, mem_k, mem_v, weights):
    b, t, _ = x.shape
    assert t == CHUNK
    tb = TOKEN_BLOCK
    nsb = tb // t
    cos_t, sin_t = _rope_tables(PAST_LEN + np.arange(t))
    mdec, qdec, kdec, rdec = _retention_tables(t)
    consts = (mdec, qdec, kdec, rdec)
    s_rt2 = s_rt.reshape(b, RT_H // 2, 2 * RT_DK, RT_DV)
    mem_spec = pl.BlockSpec((nsb, N_MEM * XA_H, XA_DH), lambda i: (i, 0, 0))
    in_specs = ([pl.BlockSpec((tb, D_MODEL), lambda i: (i, 0)),
                 _const_spec(cos_t.shape), _const_spec(sin_t.shape)]
                + [_const_spec(c.shape) for c in consts]
                + [mem_spec, mem_spec,
                   pl.BlockSpec((nsb, HG_H, HG_DK, HG_DV), lambda i: (i, 0, 0, 0)),
                   pl.BlockSpec((nsb, RT_H // 2, 2 * RT_DK, RT_DV), lambda i: (i, 0, 0, 0))]
                + [_const_spec(w.shape) for w in weights])
    out_shape = [jax.ShapeDtypeStruct((b * t, D_MODEL), F32),
                 jax.ShapeDtypeStruct((b, HG_H, HG_DK, HG_DV), F32),
                 jax.ShapeDtypeStruct((b, RT_H // 2, 2 * RT_DK, RT_DV), F32)]
    out_specs = [pl.BlockSpec((tb, D_MODEL), lambda i: (i, 0)),
                 pl.BlockSpec((nsb, HG_H, HG_DK, HG_DV), lambda i: (i, 0, 0, 0)),
                 pl.BlockSpec((nsb, RT_H // 2, 2 * RT_DK, RT_DV), lambda i: (i, 0, 0, 0))]
    y, s_new, r_new = pl.pallas_call(
        _sample_kernel,
        grid=(b // nsb,),
        in_specs=in_specs,
        out_specs=out_specs,
        out_shape=out_shape,
        scratch_shapes=_layer_scratch(tb),
        compiler_params=pltpu.CompilerParams(dimension_semantics=("arbitrary",),
                                             vmem_limit_bytes=VMEM_LIMIT_BYTES),
        name="sample_layer",
    )(x.reshape(b * t, D_MODEL), cos_t, sin_t, *consts,
      mem_k.reshape(b, N_MEM * XA_H, XA_DH), mem_v.reshape(b, N_MEM * XA_H, XA_DH), s_hg, s_rt2, *weights)
    return y.reshape(b, t, D_MODEL), s_new, r_new


def kernel(x_prompt, x_sample, mem_prompt, state_hgrn, state_ret, cache_mem_k, cache_mem_v, norm_g, w_in,
           lb_logits, hg_norm_g, rt_norm_g, mem_norm_g, w_mem_k, w_mem_v, w_out, final_norm_g):
    depth = w_in.shape[0]
    assert depth == 1, "single-layer step"
    bp = x_prompt.shape[0]
    bs = x_sample.shape[0]
    g_in, g_hg, g_rt = norm_g[0].reshape(1, D_MODEL), hg_norm_g[0].reshape(1, HG_W), rt_norm_g[0].reshape(1, RT_W)
    g_fin = final_norm_g.reshape(1, D_MODEL)
    mk, mv, mk_b, mv_b = _memory_kv(mem_prompt, mem_norm_g[0], w_mem_k[0], w_mem_v[0])
    y_p, s_p, r_p, w_in_b, w_out_b = _prompt_layer(x_prompt, mk_b, mv_b, w_in[0], w_out[0],
                                                   (g_in, lb_logits, g_hg, g_rt, g_fin))
    y_s, s_s, r_s = _sample_layer(x_sample, state_hgrn[0], state_ret[0], cache_mem_k[0], cache_mem_v[0],
                                  (g_in, w_in_b, lb_logits, g_hg, g_rt, w_out_b, g_fin))
    return (y_p, y_s,
            s_p.reshape(1, bp, HG_H, HG_DK, HG_DV),
            r_p.reshape(1, bp, RT_H, RT_DK, RT_DV),
            mk.reshape(1, bp, N_MEM, XA_H, XA_DH),
            mv.reshape(1, bp, N_MEM, XA_H, XA_DH),
            s_s.reshape(1, bs, HG_H, HG_DK, HG_DV),
            r_s.reshape(1, bs, RT_H, RT_DK, RT_DV))
```

```python
import functools

import numpy as np
import jax
import jax.numpy as jnp
from jax import lax
from jax.experimental import pallas as pl
from jax.experimental.pallas import tpu as pltpu

F32 = jnp.float32
BF16 = jnp.bfloat16

D_MODEL = 1024
PAST_LEN = 4096
CHUNK = 64
N_MEM = 256
EPS = 1e-6
ROPE_BASE = 10000.0
HG_H, HG_DK, HG_DV = 8, 128, 128
HG_W = HG_H * HG_DV
RT_H, RT_DK, RT_DV = 4, 64, 128
RT_QK = RT_H * RT_DK
RT_W = RT_H * RT_DV
XA_H, XA_DH = 4, 128
XA_W = XA_H * XA_DH
D_MIX = HG_W + RT_W + XA_W
OFF_HG_Q = 0
OFF_HG_F = OFF_HG_Q + HG_W
OFF_HG_I = OFF_HG_F + HG_W
OFF_HG_GATE = OFF_HG_I + HG_W
OFF_RT_Q = OFF_HG_GATE + HG_W
OFF_RT_K = OFF_RT_Q + RT_QK
OFF_RT_V = OFF_RT_K + RT_QK
OFF_RT_GATE = OFF_RT_V + RT_W
OFF_XA_Q = OFF_RT_GATE + RT_W
OFF_XA_GATE = OFF_XA_Q + XA_W
D_IN = OFF_XA_GATE + XA_W

LANES = 128
SUBLANES = 8
TOKEN_BLOCK = 256
PROMPT_BLOCKS_PER_STEP = 1
WEIGHT_PREP_STEPS = 8
PROJ_N_TILE = 512
PROJ_TILES_PER_GATE_GROUP = 1
PROJ_TILES_PER_HGRN_STAGE = 1
HGRN_STREAMS_PER_STAGE = 4
OUT_N_TILE = 512
VMEM_LIMIT_BYTES = 60 * 1024 * 1024
_NT = (((1,), (1,)), ((), ()))
_TN = (((0,), (0,)), ((), ()))


def _silu(x):
    h = 0.5 * x
    return h + h * jnp.tanh(h)


def _rms(x, g):
    return x * lax.rsqrt(jnp.mean(x * x, axis=-1, keepdims=True) + EPS) * g


def _cols(off, i, width):
    return slice(off + i * width, off + (i + 1) * width)


def _rope_tables(pos):
    half = RT_DK // 2
    inv_freq = ROPE_BASE ** (-np.arange(half, dtype=np.float64) / half)
    ang = np.asarray(pos, np.float64)[:, None] * inv_freq[None, :]
    cos, sin = np.cos(ang), np.sin(ang)
    cos_t = np.concatenate([cos, cos, cos, cos], axis=-1)
    sin_t = np.concatenate([-sin, sin, -sin, sin], axis=-1)
    return jnp.asarray(cos_t, F32), jnp.asarray(sin_t, F32)


def _retention_tables(ts):
    h = np.arange(RT_H, dtype=np.float64)
    log_gamma = np.log1p(-np.exp2(-5.0 - h))
    n = np.arange(ts, dtype=np.float64)
    scale = RT_DK ** -0.5
    dist = np.abs(n[:, None] - n[None, :])
    visible = (n[None, :] // CHUNK) <= (n[:, None] // CHUNK)
    mdec = scale * np.exp(log_gamma[:, None, None] * dist[None]) * visible[None]
    qd = np.exp(log_gamma[:, None] * (n[None, :] + 1.0))
    kd = scale * np.exp(log_gamma[:, None] * (ts - 1.0 - n[None, :]))
    rd = np.exp(log_gamma * ts)

    def pair_lanes(a):
        a = np.repeat(a[:, :, None], RT_DK, axis=2)
        return np.concatenate([a[0::2], a[1::2]], axis=2)

    rdec = np.repeat(np.repeat(rd[:, None], RT_DK, axis=1).reshape(RT_H // 2, 2 * RT_DK, 1), LANES, axis=2)
    return (jnp.asarray(mdec, F32), jnp.asarray(pair_lanes(qd), F32), jnp.asarray(pair_lanes(kd), F32),
            jnp.asarray(rdec, F32))


def _hgrn_chunks(proj_ref, gc_ref, mixed_ref, hg_g_ref, chunks, score_mask, get_st, set_st, proj):
    half = CHUNK // 2
    zero_half = jnp.zeros((half, HG_DK), BF16)
    insts = [(s, h) for s, _ in chunks for h in range(HG_H)]
    rows = dict(chunks)
    q_rel, k_rel, q_abs, k_end, v_b, dec = {}, {}, {}, {}, {}, {}
    for (s, h) in insts:
        q = _silu(proj_ref[rows[s], _cols(OFF_HG_Q, h, HG_DK)])
        k = proj_ref[rows[s], _cols(OFF_HG_F, h, HG_DK)]
        gc = gc_ref[rows[s], _cols(0, h, HG_DK)]
        g_end = gc[CHUNK - 1:CHUNK, :]
        (q1, q2), (k1, k2), (g1, g2) = ((a[:half], a[half:]) for a in (q, k, gc))
        g_m1, g_m2 = gc[half // 2 - 1:half // 2, :], gc[half + half // 2 - 1:half + half // 2, :]
        g_b = gc[half - 1:half, :]
        qd1, kd1 = (q1 * jnp.exp2(g1 - g_m1)).astype(BF16), (k1 * jnp.exp2(g_m1 - g1)).astype(BF16)
        qd2, kd2 = (q2 * jnp.exp2(g2 - g_m2)).astype(BF16), (k2 * jnp.exp2(g_m2 - g2)).astype(BF16)
        qo, ko = (q2 * jnp.exp2(g2 - g_b)).astype(BF16), (k1 * jnp.exp2(g_b - g1)).astype(BF16)
        q_rel[s, h] = jnp.concatenate([jnp.concatenate([qd1, qd2], axis=0),
                                       jnp.concatenate([zero_half, qo], axis=0)], axis=1)
        k_rel[s, h] = jnp.concatenate([jnp.concatenate([kd1, kd2, zero_half, zero_half], axis=0),
                                       jnp.concatenate([zero_half, zero_half, ko, zero_half], axis=0)],
                                      axis=1)
        q_abs[s, h] = (q * jnp.exp2(gc)).astype(BF16)
        k_end[s, h] = (k * jnp.exp2(g_end - gc)).astype(BF16)
        v_b[s, h] = proj_ref[rows[s], _cols(OFF_HG_I, h, HG_DV)].astype(BF16)
        dec[s, h] = jnp.exp2(g_end)
    proj.emit(PROJ_TILES_PER_HGRN_STAGE)
    a = {k: lax.dot_general(q_rel[k], k_rel[k], _NT, preferred_element_type=F32) for k in insts}
    st = {k: get_st(*k) for k in insts}
    v_t = {k: jnp.concatenate([v_b[k], v_b[k][:half], zero_half], axis=0).T for k in insts}
    kv = {k: jnp.dot(v_t[k][:, :CHUNK], k_end[k], preferred_element_type=F32) for k in insts}
    for k in insts:
        set_st(*k, dec[k] * st[k] + kv[k])
    a_m = {k: jnp.where(score_mask, a[k], 0.0).astype(BF16) for k in insts}
    o = {k: lax.dot_general(jnp.concatenate([q_abs[k], a_m[k]], axis=1),
                            jnp.concatenate([st[k].astype(BF16), v_t[k]], axis=1), _NT, preferred_element_type=F32)
         for k in insts}
    ms = {k: jnp.mean(o[k] * o[k], axis=-1, keepdims=True) for k in insts}
    for (s, h) in insts:
        gate = _silu(proj_ref[rows[s], _cols(OFF_HG_GATE, h, HG_DV)].astype(BF16))
        on = o[s, h] * lax.rsqrt(ms[s, h] + EPS) * hg_g_ref[:, _cols(0, h, HG_DV)]
        mixed_ref[rows[s], _cols(0, h, HG_DV)] = on.astype(BF16) * gate


def _retention(n_streams, ts, proj_ref, mixed_ref, cos_t, sin_t, mdec_ref, qdec_ref, kdec_ref, rdec_ref,
               rt_g_ref, get_r, set_r):
    lane = lax.broadcasted_iota(jnp.int32, (ts, LANES), 1)
    first_half = (lane % RT_DK) < (RT_DK // 2)
    low_head = lane < RT_DK
    low_rows = lax.broadcasted_iota(jnp.int32, (2 * RT_DK, RT_DV), 0) < RT_DK

    def rope(xs, cs, sn):
        swapped = jnp.where(first_half, pltpu.roll(xs, LANES - RT_DK // 2, 1), pltpu.roll(xs, RT_DK // 2, 1))
        return xs * cs + swapped * sn

    pairs = [(s, p) for s in range(n_streams) for p in range(RT_H // 2)]
    insts = [(s, p, i) for (s, p) in pairs for i in range(2)]
    rows = {s: slice(s * ts, (s + 1) * ts) for s in range(n_streams)}
    qp, kp_b, qd, kd_b, r_pair, v_pair = {}, {}, {}, {}, {}, {}
    for (s, p) in pairs:
        cs, sn = cos_t[rows[s], :], sin_t[rows[s], :]
        q = rope(proj_ref[rows[s], _cols(OFF_RT_Q, p, LANES)], cs, sn)
        k = rope(proj_ref[rows[s], _cols(OFF_RT_K, p, LANES)], cs, sn)
        qp[s, p] = q
        qd[s, p] = q * qdec_ref[p]
        kp_b[s, p] = k.astype(BF16)
        kd_b[s, p] = (k * kdec_ref[p]).astype(BF16)
        r_pair[s, p] = get_r(s, p)
        v_pair[s, p] = proj_ref[rows[s], _cols(OFF_RT_V, p, 2 * RT_DV)].astype(BF16)
    yield
    sel = {0: low_head, 1: jnp.logical_not(low_head)}
    q_h = {(s, p, i): jnp.where(sel[i], qp[s, p], 0.0).astype(BF16) for (s, p, i) in insts}
    qd_h = {(s, p, i): jnp.where(sel[i], qd[s, p], 0.0).astype(BF16) for (s, p, i) in insts}
    sc = {(s, p, i): lax.dot_general(q_h[s, p, i], kp_b[s, p], _NT, preferred_element_type=F32)
          for (s, p, i) in insts}
    inter = {(s, p, i): jnp.dot(qd_h[s, p, i], r_pair[s, p].astype(BF16), preferred_element_type=F32)
             for (s, p, i) in insts}
    kv = {(s, p): lax.dot_general(kd_b[s, p], v_pair[s, p], _TN, preferred_element_type=F32) for (s, p) in pairs}
    yield
    for (s, p) in pairs:
        dr = jnp.where(low_rows, kv[s, p][:, :RT_DV], kv[s, p][:, RT_DV:])
        set_r(s, p, rdec_ref[p] * r_pair[s, p] + dr)
    sc_b = {(s, p, i): (sc[s, p, i] * mdec_ref[2 * p + i]).astype(BF16) for (s, p, i) in insts}
    yield
    o = {(s, p, i): jnp.dot(sc_b[s, p, i], v_pair[s, p][:, i * RT_DV:(i + 1) * RT_DV],
                            preferred_element_type=F32) + inter[s, p, i] for (s, p, i) in insts}
    yield
    mu = {k: jnp.mean(o[k], axis=-1, keepdims=True) for k in insts}
    cen = {k: o[k] - mu[k] for k in insts}
    var = {k: jnp.mean(cen[k] * cen[k], axis=-1, keepdims=True) for k in insts}
    yield
    for (s, p, i) in insts:
        h = 2 * p + i
        gate = _silu(proj_ref[rows[s], _cols(OFF_RT_GATE, h, RT_DV)].astype(BF16))
        on = cen[s, p, i] * lax.rsqrt(var[s, p, i] + EPS) * rt_g_ref[:, _cols(0, h, RT_DV)]
        mixed_ref[rows[s], _cols(HG_W, h, RT_DV)] = on.astype(BF16) * gate


def _cross_attention(n_streams, ts, proj_ref, mixed_ref, get_mem):
    xa_scale = XA_DH ** -0.5
    insts = [(s, h) for s in range(n_streams) for h in range(XA_H)]
    rows = {s: slice(s * ts, (s + 1) * ts) for s in range(n_streams)}
    mem = {k: get_mem(*k) for k in insts}
    sc = {(s, h): lax.dot_general(proj_ref[rows[s], _cols(OFF_XA_Q, h, XA_DH)].astype(BF16), mem[s, h][0], _NT,
                                  preferred_element_type=F32) * xa_scale for (s, h) in insts}
    yield
    mx = {k: jnp.max(sc[k], axis=-1, keepdims=True) for k in insts}
    yield
    pr = {k: jnp.exp(sc[k] - mx[k]) for k in insts}
    den = {k: jnp.sum(pr[k], axis=-1, keepdims=True) for k in insts}
    yield
    o = {k: jnp.dot(pr[k].astype(BF16), mem[k][1], preferred_element_type=F32) for k in insts}
    yield
    for (s, h) in insts:
        gate = _silu(proj_ref[rows[s], _cols(OFF_XA_GATE, h, XA_DH)].astype(BF16))
        mixed_ref[rows[s], _cols(HG_W + RT_W, h, XA_DH)] = (o[s, h] / den[s, h]).astype(BF16) * gate


def _chunk_cumsum(g):
    tokens, c = g.shape
    x = g.reshape(tokens // SUBLANES, SUBLANES, c)
    sub = lax.broadcasted_iota(jnp.int32, x.shape, 1)
    shift = 1
    while shift < SUBLANES:
        x = x + jnp.where(sub >= shift, pltpu.roll(x, shift, 1), 0.0)
        shift *= 2
    tiles = CHUNK // SUBLANES
    x = x.reshape(tokens // CHUNK, tiles, SUBLANES, c)
    out = [x[:, 0]]
    for j in range(1, tiles):
        total = out[-1][:, SUBLANES - 1:SUBLANES, :]
        out.append(x[:, j] + total)
    return jnp.stack(out, axis=1).reshape(tokens, c)


class _Projection:
    SEGMENTS = ((OFF_HG_F, OFF_HG_I), (OFF_HG_Q, OFF_HG_F), (OFF_HG_I, OFF_RT_Q), (OFF_RT_Q, D_IN))
    TILES_GATES = HG_W // PROJ_N_TILE
    TILES_HGRN = 4 * HG_W // PROJ_N_TILE
    TILES_ALL = D_IN // PROJ_N_TILE

    def __init__(self, x, norm_g_ref, w_in_ref, xn_ref, proj_ref):
        xn_ref[...] = _rms(x, norm_g_ref[...]).astype(BF16)
        self._refs = (xn_ref, w_in_ref, proj_ref)
        self._pending = [n0 for lo, hi in self.SEGMENTS for n0 in range(lo, hi, PROJ_N_TILE)]
        assert len(self._pending) == self.TILES_ALL
        self._done = 0

    def emit(self, n=1):
        xn_ref, w_in_ref, proj_ref = self._refs
        for _ in range(min(n, len(self._pending))):
            cols = slice(self._pending[0], self._pending.pop(0) + PROJ_N_TILE)
            proj_ref[:, cols] = jnp.dot(xn_ref[...], w_in_ref[:, cols], preferred_element_type=F32)
            self._done += 1

    def require(self, n_tiles):
        self.emit(n_tiles - self._done)


def _mix(n_streams, ts, x_ref, y_ref, cos_t, sin_t, mdec_ref, qdec_ref, kdec_ref, rdec_ref,
         get_mem, lb_ref, hg_g_ref, rt_g_ref, w_out_ref, fin_g_ref,
         proj_ref, gc_ref, mixed_ref, get_st, set_st, get_r, set_r, proj):
    proj.require(proj.TILES_GATES)
    lg = lb_ref[...]
    e = jnp.exp(lg - jnp.max(lg, axis=0, keepdims=True))
    lb = e[0:1, :] / jnp.sum(e, axis=0, keepdims=True)
    for c0 in range(0, HG_W, 2 * LANES):
        cols = slice(OFF_HG_F + c0, OFF_HG_F + c0 + 2 * LANES)
        lbc = lb[:, c0:c0 + 2 * LANES]
        half_span = 0.5 * (1.0 - lbc)
        bt = half_span * jnp.tanh(0.5 * proj_ref[:, cols])
        g = jnp.log2(lbc + (half_span + bt))
        proj_ref[:, cols] = half_span - bt
        gc_ref[:, c0:c0 + 2 * LANES] = _chunk_cumsum(g)
        proj.emit(PROJ_TILES_PER_GATE_GROUP)

    proj.require(proj.TILES_HGRN)
    row = lax.broadcasted_iota(jnp.int32, (CHUNK, LANES), 0)
    col = lax.broadcasted_iota(jnp.int32, (CHUNK, LANES), 1)
    half = CHUNK // 2
    causal = (((col <= row) & (col // half == row // half))
              | ((col >= CHUNK) & (col < CHUNK + half) & (row >= half)))
    for c in range(ts // CHUNK):
        chunks = [(s, slice(s * ts + c * CHUNK, s * ts + (c + 1) * CHUNK)) for s in range(n_streams)]
        for i in range(0, n_streams, HGRN_STREAMS_PER_STAGE):
            _hgrn_chunks(proj_ref, gc_ref, mixed_ref, hg_g_ref, chunks[i:i + HGRN_STREAMS_PER_STAGE], causal,
                         get_st, set_st, proj)

    proj.require(proj.TILES_ALL)

    def cross_attention_and_its_rows():
        yield from _cross_attention(n_streams, ts, proj_ref, mixed_ref, get_mem)
        yield from _out_projection(mixed_ref, w_out_ref, y_ref, HG_W + RT_W, D_MIX, y_ref)

    _run_interleaved(
        _retention(n_streams, ts, proj_ref, mixed_ref, cos_t, sin_t, mdec_ref, qdec_ref, kdec_ref, rdec_ref,
                   rt_g_ref, get_r, set_r),
        cross_attention_and_its_rows(),
        _out_projection(mixed_ref, w_out_ref, y_ref, 0, HG_W, x_ref))
    sumsq = []
    _run_interleaved(_out_projection(mixed_ref, w_out_ref, y_ref, HG_W, HG_W + RT_W, y_ref, sumsq))

    inv = lax.rsqrt(sum(sumsq) * (1.0 / D_MODEL) + EPS)
    y_ref[...] = y_ref[...] * inv * fin_g_ref[...]


def _out_projection(mixed_ref, w_out_ref, acc_ref, k_lo, k_hi, base_ref, sumsq=None):
    for n0 in range(0, D_MODEL, OUT_N_TILE):
        cols = slice(n0, n0 + OUT_N_TILE)
        out = base_ref[:, cols] + jnp.dot(mixed_ref[:, k_lo:k_hi], w_out_ref[k_lo:k_hi, cols],
                                          preferred_element_type=F32)
        acc_ref[:, cols] = out
        if sumsq is not None:
            sumsq.append(jnp.sum(out * out, axis=-1, keepdims=True))
        yield


class _StackedRows:
    def __init__(self, refs):
        self._refs = refs
        self._rows = refs[0].shape[0]

    def __getitem__(self, idx):
        cols = idx[1] if isinstance(idx, tuple) else slice(None)
        return jnp.concatenate([r[:, cols] for r in self._refs], axis=0)

    def __setitem__(self, idx, value):
        cols = idx[1] if isinstance(idx, tuple) else slice(None)
        for s, r in enumerate(self._refs):
            r[:, cols] = value[s * self._rows:(s + 1) * self._rows]


def _run_interleaved(*stage_generators):
    pending = list(stage_generators)
    while pending:
        for g in list(pending):
            try:
                next(g)
            except StopIteration:
                pending.remove(g)


def _memory_kv_kernel(mem_ref, g_ref, wk_ref, wv_ref, k_ref, v_ref, kb_ref, vb_ref):
    wk, wv = wk_ref[...].astype(BF16), wv_ref[...].astype(BF16)
    for s in range(mem_ref.shape[0]):
        m = _rms(mem_ref[s], g_ref[...]).astype(BF16)
        k = jnp.dot(m, wk, preferred_element_type=F32)
        v = jnp.dot(m, wv, preferred_element_type=F32)
        for h in range(XA_H):
            k_ref[s, pl.ds(h, N_MEM, stride=XA_H), :] = k[:, _cols(0, h, XA_DH)]
            v_ref[s, pl.ds(h, N_MEM, stride=XA_H), :] = v[:, _cols(0, h, XA_DH)]
        kb_ref[s] = k.astype(BF16)
        vb_ref[s] = v.astype(BF16)


def _prompt_kernel(steps_per_stream, w_in_slab_ref, w_out_slab_ref,
                   x_ref, cos_ref, sin_ref, mdec_ref, qdec_ref, kdec_ref, rdec_ref, mk_ref, mv_ref,
                   norm_g_ref, lb_ref, hg_g_ref, rt_g_ref, fin_g_ref,
                   y_ref, s_out_ref, r_out_ref, w_in_b_slab_ref, w_out_b_slab_ref,
                   proj_ref, xn_ref, gc_ref, mixed_ref, st_ref, w_in_ref, w_out_ref):
    n_streams = x_ref.shape[0]
    t = pl.program_id(0)
    j = t - WEIGHT_PREP_STEPS
    running = t >= WEIGHT_PREP_STEPS

    @pl.when(jnp.logical_not(running))
    def _():
        for slab_ref, out_ref, full_ref in ((w_in_slab_ref, w_in_b_slab_ref, w_in_ref),
                                            (w_out_slab_ref, w_out_b_slab_ref, w_out_ref)):
            n = slab_ref.shape[0]
            slab = slab_ref[...].astype(BF16)
            full_ref[pl.ds(pl.multiple_of(t * n, n), n), :] = slab
            out_ref[...] = slab

    @pl.when(running & (j == 0))
    def _():
        st_ref[...] = jnp.zeros_like(st_ref)
        r_out_ref[...] = jnp.zeros_like(r_out_ref)

    def set_st(s, h, val):
        st_ref[s, h] = val

    def set_r(s, p, val):
        r_out_ref[s, p] = val

    def get_mem(s, h):
        return mk_ref[s, :, _cols(0, h, XA_DH)], mv_ref[s, :, _cols(0, h, XA_DH)]

    def token_block(i, carry):
        rows = pl.ds(pl.multiple_of(i * TOKEN_BLOCK, TOKEN_BLOCK), TOKEN_BLOCK)
        x_blk = _StackedRows([x_ref.at[s, rows] for s in range(n_streams)])
        y_blk = _StackedRows([y_ref.at[s, rows] for s in range(n_streams)])
        cos_t = jnp.concatenate([cos_ref[rows, :]] * n_streams, axis=0)
        sin_t = jnp.concatenate([sin_ref[rows, :]] * n_streams, axis=0)
        proj = _Projection(x_blk[...], norm_g_ref, w_in_ref, xn_ref, proj_ref)
        _mix(
            n_streams, TOKEN_BLOCK, x_blk, y_blk, cos_t, sin_t,
            mdec_ref, qdec_ref, kdec_ref, rdec_ref,
            get_mem, lb_ref, hg_g_ref, rt_g_ref, w_out_ref, fin_g_ref,
            proj_ref, gc_ref, mixed_ref,
            lambda s, h: st_ref[s, h], set_st, lambda s, p: r_out_ref[s, p], set_r, proj)
        return carry

    @pl.when(running)
    def _():
        lax.fori_loop(0, x_ref.shape[1] // TOKEN_BLOCK, token_block, 0)

    @pl.when(running & (j == steps_per_stream - 1))
    def _():
        for s in range(n_streams):
            for h in range(HG_H):
                s_out_ref[s, h] = st_ref[s, h].T


def _sample_kernel(x_ref, cos_ref, sin_ref, mdec_ref, qdec_ref, kdec_ref, rdec_ref, mk_ref, mv_ref,
                   s_in_ref, r_in_ref,
                   norm_g_ref, w_in_ref, lb_ref, hg_g_ref, rt_g_ref, w_out_ref, fin_g_ref,
                   y_ref, s_out_ref, r_out_ref, proj_ref, xn_ref, gc_ref, mixed_ref):
    n_streams = TOKEN_BLOCK // CHUNK

    def set_st(s, h, val):
        s_out_ref[s, h] = val.T

    def set_r(s, p, val):
        r_out_ref[s, p] = val

    def get_mem(s, h):
        head_rows = pl.ds(h, N_MEM, stride=XA_H)
        return mk_ref[s, head_rows, :].astype(BF16), mv_ref[s, head_rows, :].astype(BF16)

    cos_t = jnp.concatenate([cos_ref[...]] * n_streams, axis=0)
    sin_t = jnp.concatenate([sin_ref[...]] * n_streams, axis=0)
    proj = _Projection(x_ref[...], norm_g_ref, w_in_ref, xn_ref, proj_ref)
    _mix(
        n_streams, CHUNK, x_ref, y_ref, cos_t, sin_t, mdec_ref, qdec_ref, kdec_ref, rdec_ref,
        get_mem, lb_ref, hg_g_ref, rt_g_ref, w_out_ref, fin_g_ref,
        proj_ref, gc_ref, mixed_ref,
        lambda s, h: s_in_ref[s, h].T, set_st, lambda s, p: r_in_ref[s, p], set_r, proj)


def _const_spec(shape):
    zeros = (0,) * len(shape)
    return pl.BlockSpec(shape, lambda *_: zeros, pipeline_mode=pl.Buffered(1))


def _layer_scratch(rows):
    return [pltpu.VMEM((rows, D_IN), F32),
            pltpu.VMEM((rows, D_MODEL), BF16),
            pltpu.VMEM((rows, HG_W), F32),
            pltpu.VMEM((rows, D_MIX), BF16)]


def _memory_kv(mem, g, wk, wv):
    b = mem.shape[0]
    out_f = jax.ShapeDtypeStruct((b, N_MEM * XA_H, XA_DH), F32)
    out_b = jax.ShapeDtypeStruct((b, N_MEM, XA_W), BF16)
    blk_f = pl.BlockSpec((b, N_MEM * XA_H, XA_DH), lambda i: (0, 0, 0))
    blk_b = pl.BlockSpec((b, N_MEM, XA_W), lambda i: (0, 0, 0))
    return pl.pallas_call(
        _memory_kv_kernel,
        grid=(1,),
        in_specs=[pl.BlockSpec((b, N_MEM, D_MODEL), lambda i: (0, 0, 0)),
                  pl.BlockSpec((1, D_MODEL), lambda i: (0, 0)),
                  pl.BlockSpec((D_MODEL, XA_W), lambda i: (0, 0)),
                  pl.BlockSpec((D_MODEL, XA_W), lambda i: (0, 0))],
        out_specs=[blk_f, blk_f, blk_b, blk_b],
        out_shape=[out_f, out_f, out_b, out_b],
        compiler_params=pltpu.CompilerParams(vmem_limit_bytes=VMEM_LIMIT_BYTES),
        name="memory_kv",
    )(mem, g.reshape(1, D_MODEL), wk, wv)


def _prompt_layer(x, mk_b, mv_b, w_in, w_out, params):
    b, t, _ = x.shape
    tb = PROMPT_BLOCKS_PER_STEP * TOKEN_BLOCK
    spp = t // tb
    prep = WEIGHT_PREP_STEPS
    cos_t, sin_t = _rope_tables(np.arange(t))
    mdec, qdec, kdec, rdec = _retention_tables(TOKEN_BLOCK)
    consts = (mdec, qdec, kdec, rdec)
    in_rows, out_rows = D_MODEL // prep, D_MIX // prep

    def slab(step):
        return jnp.minimum(step, prep - 1)

    def run(step):
        return jnp.maximum(step - prep, 0)

    in_specs = ([pl.BlockSpec((in_rows, D_IN), lambda s: (slab(s), 0)),
                 pl.BlockSpec((out_rows, D_MODEL), lambda s: (slab(s), 0)),
                 pl.BlockSpec((b, tb, D_MODEL), lambda s: (0, run(s), 0)),
                 pl.BlockSpec((tb, LANES), lambda s: (run(s), 0)),
                 pl.BlockSpec((tb, LANES), lambda s: (run(s), 0))]
                + [_const_spec(c.shape) for c in consts]
                + [_const_spec(mk_b.shape), _const_spec(mv_b.shape)]
                + [_const_spec(p.shape) for p in params])
    out_shape = [jax.ShapeDtypeStruct((b, t, D_MODEL), F32),
                 jax.ShapeDtypeStruct((b, HG_H, HG_DK, HG_DV), F32),
                 jax.ShapeDtypeStruct((b, RT_H // 2, 2 * RT_DK, RT_DV), F32),
                 jax.ShapeDtypeStruct((D_MODEL, D_IN), BF16),
                 jax.ShapeDtypeStruct((D_MIX, D_MODEL), BF16)]
    out_specs = [pl.BlockSpec((b, tb, D_MODEL), lambda s: (0, run(s), 0)),
                 pl.BlockSpec((b, HG_H, HG_DK, HG_DV), lambda s: (0, 0, 0, 0)),
                 pl.BlockSpec((b, RT_H // 2, 2 * RT_DK, RT_DV), lambda s: (0, 0, 0, 0)),
                 pl.BlockSpec((in_rows, D_IN), lambda s: (slab(s), 0)),
                 pl.BlockSpec((out_rows, D_MODEL), lambda s: (slab(s), 0))]
    return pl.pallas_call(
        functools.partial(_prompt_kernel, spp),
        grid=(prep + spp,),
        in_specs=in_specs,
        out_specs=out_specs,
        out_shape=out_shape,
        scratch_shapes=_layer_scratch(b * TOKEN_BLOCK) + [
                                           pltpu.VMEM((b, HG_H, HG_DV, HG_DK), F32),
                                           pltpu.VMEM((D_MODEL, D_IN), BF16),
                                           pltpu.VMEM((D_MIX, D_MODEL), BF16)],
        compiler_params=pltpu.CompilerParams(dimension_semantics=("arbitrary",),
                                             vmem_limit_bytes=VMEM_LIMIT_BYTES),
        name="prompt_layer",
    )(w_in, w_out, x, cos_t, sin_t, *consts, mk_b, mv_b, *params)


def _sample_layer(x, s_hg, s_rt, mem_k, mem_v, weights):
    b, t, _ = x.shape
    assert t == CHUNK
    tb = TOKEN_BLOCK
    nsb = tb // t
    cos_t, sin_t = _rope_tables(PAST_LEN + np.arange(t))
    mdec, qdec, kdec, rdec = _retention_tables(t)
    consts = (mdec, qdec, kdec, rdec)
    s_rt2 = s_rt.reshape(b, RT_H // 2, 2 * RT_DK, RT_DV)
    mem_spec = pl.BlockSpec((nsb, N_MEM * XA_H, XA_DH), lambda i: (i, 0, 0))
    in_specs = ([pl.BlockSpec((tb, D_MODEL), lambda i: (i, 0)),
                 _const_spec(cos_t.shape), _const_spec(sin_t.shape)]
                + [_const_spec(c.shape) for c in consts]
                + [mem_spec, mem_spec,
                   pl.BlockSpec((nsb, HG_H, HG_DK, HG_DV), lambda i: (i, 0, 0, 0)),
                   pl.BlockSpec((nsb, RT_H // 2, 2 * RT_DK, RT_DV), lambda i: (i, 0, 0, 0))]
                + [_const_spec(w.shape) for w in weights])
    out_shape = [jax.ShapeDtypeStruct((b * t, D_MODEL), F32),
                 jax.ShapeDtypeStruct((b, HG_H, HG_DK, HG_DV), F32),
                 jax.ShapeDtypeStruct((b, RT_H // 2, 2 * RT_DK, RT_DV), F32)]
    out_specs = [pl.BlockSpec((tb, D_MODEL), lambda i: (i, 0)),
                 pl.BlockSpec((nsb, HG_H, HG_DK, HG_DV), lambda i: (i, 0, 0, 0)),
                 pl.BlockSpec((nsb, RT_H // 2, 2 * RT_DK, RT_DV), lambda i: (i, 0, 0, 0))]
    y, s_new, r_new = pl.pallas_call(
        _sample_kernel,
        grid=(b // nsb,),
        in_specs=in_specs,
        out_specs=out_specs,
        out_shape=out_shape,
        scratch_shapes=_layer_scratch(tb),
        compiler_params=pltpu.CompilerParams(dimension_semantics=("arbitrary",),
                                             vmem_limit_bytes=VMEM_LIMIT_BYTES),
        name="sample_layer",
    )(x.reshape(b * t, D_MODEL), cos_t, sin_t, *consts,
      mem_k.reshape(b, N_MEM * XA_H, XA_DH), mem_v.reshape(b, N_MEM * XA_H, XA_DH), s_hg, s_rt2, *weights)
    return y.reshape(b, t, D_MODEL), s_new, r_new


def kernel(x_prompt, x_sample, mem_prompt, state_hgrn, state_ret, cache_mem_k, cache_mem_v, norm_g, w_in,
           lb_logits, hg_norm_g, rt_norm_g, mem_norm_g, w_mem_k, w_mem_v, w_out, final_norm_g):
    depth = w_in.shape[0]
    assert depth == 1, "single-layer step"
    bp = x_prompt.shape[0]
    bs = x_sample.shape[0]
    g_in, g_hg, g_rt = norm_g[0].reshape(1, D_MODEL), hg_norm_g[0].reshape(1, HG_W), rt_norm_g[0].reshape(1, RT_W)
    g_fin = final_norm_g.reshape(1, D_MODEL)
    mk, mv, mk_b, mv_b = _memory_kv(mem_prompt, mem_norm_g[0], w_mem_k[0], w_mem_v[0])
    y_p, s_p, r_p, w_in_b, w_out_b = _prompt_layer(x_prompt, mk_b, mv_b, w_in[0], w_out[0],
                                                   (g_in, lb_logits, g_hg, g_rt, g_fin))
    y_s, s_s, r_s = _sample_layer(x_sample, state_hgrn[0], state_ret[0], cache_mem_k[0], cache_mem_v[0],
                                  (g_in, w_in_b, lb_logits, g_hg, g_rt, w_out_b, g_fin))
    return (y_p, y_s,
            s_p.reshape(1, bp, HG_H, HG_DK, HG_DV),
            r_p.reshape(1, bp, RT_H, RT_DK, RT_DV),
            mk.reshape(1, bp, N_MEM, XA_H, XA_DH),
            mv.reshape(1, bp, N_MEM, XA_H, XA_DH),
            s_s.reshape(1, bs, HG_H, HG_DK, HG_DV),
            r_s.reshape(1, bs, RT_H, RT_DK, RT_DV))
```

```python
import functools

import numpy as np
import jax
import jax.numpy as jnp
from jax import lax
from jax.experimental import pallas as pl
from jax.experimental.pallas import tpu as pltpu

F32 = jnp.float32
BF16 = jnp.bfloat16

D_MODEL = 1024
PAST_LEN = 4096
CHUNK = 64
N_MEM = 256
EPS = 1e-6
ROPE_BASE = 10000.0
HG_H, HG_DK, HG_DV = 8, 128, 128
HG_W = HG_H * HG_DV
RT_H, RT_DK, RT_DV = 4, 64, 128
RT_QK = RT_H * RT_DK
RT_W = RT_H * RT_DV
XA_H, XA_DH = 4, 128
XA_W = XA_H * XA_DH
D_MIX = HG_W + RT_W + XA_W
OFF_HG_Q = 0
OFF_HG_F = OFF_HG_Q + HG_W
OFF_HG_I = OFF_HG_F + HG_W
OFF_HG_GATE = OFF_HG_I + HG_W
OFF_RT_Q = OFF_HG_GATE + HG_W
OFF_RT_K = OFF_RT_Q + RT_QK
OFF_RT_V = OFF_RT_K + RT_QK
OFF_RT_GATE = OFF_RT_V + RT_W
OFF_XA_Q = OFF_RT_GATE + RT_W
OFF_XA_GATE = OFF_XA_Q + XA_W
D_IN = OFF_XA_GATE + XA_W

LANES = 128
SUBLANES = 8
TOKEN_BLOCK = 256
PROMPT_BLOCKS_PER_STEP = 1
WEIGHT_PREP_STEPS = 8
PROJ_N_TILE = 512
PROJ_TILES_PER_GATE_GROUP = 1
PROJ_TILES_PER_HGRN_STAGE = 1
HGRN_STREAMS_PER_STAGE = 4
OUT_N_TILE = 512
VMEM_LIMIT_BYTES = 60 * 1024 * 1024
_NT = (((1,), (1,)), ((), ()))
_TN = (((0,), (0,)), ((), ()))


def _silu(x):
    h = 0.5 * x
    return h + h * jnp.tanh(h)


def _rms(x, g):
    return x * lax.rsqrt(jnp.mean(x * x, axis=-1, keepdims=True) + EPS) * g


def _cols(off, i, width):
    return slice(off + i * width, off + (i + 1) * width)


def _rope_tables(pos):
    half = RT_DK // 2
    inv_freq = ROPE_BASE ** (-np.arange(half, dtype=np.float64) / half)
    ang = np.asarray(pos, np.float64)[:, None] * inv_freq[None, :]
    cos, sin = np.cos(ang), np.sin(ang)
    cos_t = np.concatenate([cos, cos, cos, cos], axis=-1)
    sin_t = np.concatenate([-sin, sin, -sin, sin], axis=-1)
    return jnp.asarray(cos_t, F32), jnp.asarray(sin_t, F32)


def _retention_tables(ts):
    h = np.arange(RT_H, dtype=np.float64)
    log_gamma = np.log1p(-np.exp2(-5.0 - h))
    n = np.arange(ts, dtype=np.float64)
    scale = RT_DK ** -0.5
    dist = np.abs(n[:, None] - n[None, :])
    visible = (n[None, :] // CHUNK) <= (n[:, None] // CHUNK)
    mdec = scale * np.exp(log_gamma[:, None, None] * dist[None]) * visible[None]
    qd = np.exp(log_gamma[:, None] * (n[None, :] + 1.0))
    kd = scale * np.exp(log_gamma[:, None] * (ts - 1.0 - n[None, :]))
    rd = np.exp(log_gamma * ts)

    def pair_lanes(a):
        a = np.repeat(a[:, :, None], RT_DK, axis=2)
        return np.concatenate([a[0::2], a[1::2]], axis=2)

    rdec = np.repeat(np.repeat(rd[:, None], RT_DK, axis=1).reshape(RT_H // 2, 2 * RT_DK, 1), LANES, axis=2)
    return (jnp.asarray(mdec, F32), jnp.asarray(pair_lanes(qd), F32), jnp.asarray(pair_lanes(kd), F32),
            jnp.asarray(rdec, F32))


def _hgrn_chunks(proj_ref, gc_ref, mixed_ref, hg_g_ref, chunks, score_mask, get_st, set_st, proj):
    half = CHUNK // 2
    zero_half = jnp.zeros((half, HG_DK), BF16)
    insts = [(s, h) for s, _ in chunks for h in range(HG_H)]
    rows = dict(chunks)
    q_rel, k_rel, q_abs, k_end, v_b, dec = {}, {}, {}, {}, {}, {}
    for (s, h) in insts:
        q = _silu(proj_ref[rows[s], _cols(OFF_HG_Q, h, HG_DK)])
        k = proj_ref[rows[s], _cols(OFF_HG_F, h, HG_DK)]
        gc = gc_ref[rows[s], _cols(0, h, HG_DK)]
        g_end = gc[CHUNK - 1:CHUNK, :]
        (q1, q2), (k1, k2), (g1, g2) = ((a[:half], a[half:]) for a in (q, k, gc))
        g_m1, g_m2 = gc[half // 2 - 1:half // 2, :], gc[half + half // 2 - 1:half + half // 2, :]
        g_b = gc[half - 1:half, :]
        qd1, kd1 = (q1 * jnp.exp2(g1 - g_m1)).astype(BF16), (k1 * jnp.exp2(g_m1 - g1)).astype(BF16)
        qd2, kd2 = (q2 * jnp.exp2(g2 - g_m2)).astype(BF16), (k2 * jnp.exp2(g_m2 - g2)).astype(BF16)
        qo, ko = (q2 * jnp.exp2(g2 - g_b)).astype(BF16), (k1 * jnp.exp2(g_b - g1)).astype(BF16)
        q_rel[s, h] = jnp.concatenate([jnp.concatenate([qd1, qd2], axis=0),
                                       jnp.concatenate([zero_half, qo], axis=0)], axis=1)
        k_rel[s, h] = jnp.concatenate([jnp.concatenate([kd1, kd2, zero_half, zero_half], axis=0),
                                       jnp.concatenate([zero_half, zero_half, ko, zero_half], axis=0)],
                                      axis=1)
        q_abs[s, h] = (q * jnp.exp2(gc)).astype(BF16)
        k_end[s, h] = (k * jnp.exp2(g_end - gc)).astype(BF16)
        v_b[s, h] = proj_ref[rows[s], _cols(OFF_HG_I, h, HG_DV)].astype(BF16)
        dec[s, h] = jnp.exp2(g_end)
    proj.emit(PROJ_TILES_PER_HGRN_STAGE)
    a = {k: lax.dot_general(q_rel[k], k_rel[k], _NT, preferred_element_type=F32) for k in insts}
    st = {k: get_st(*k) for k in insts}
    v_t = {k: jnp.concatenate([v_b[k], v_b[k][:half], zero_half], axis=0).T for k in insts}
    kv = {k: jnp.dot(v_t[k][:, :CHUNK], k_end[k], preferred_element_type=F32) for k in insts}
    for k in insts:
        set_st(*k, dec[k] * st[k] + kv[k])
    a_m = {k: jnp.where(score_mask, a[k], 0.0).astype(BF16) for k in insts}
    o = {k: lax.dot_general(jnp.concatenate([q_abs[k], a_m[k]], axis=1),
                            jnp.concatenate([st[k].astype(BF16), v_t[k]], axis=1), _NT, preferred_element_type=F32)
         for k in insts}
    ms = {k: jnp.mean(o[k] * o[k], axis=-1, keepdims=True) for k in insts}
    for (s, h) in insts:
        gate = _silu(proj_ref[rows[s], _cols(OFF_HG_GATE, h, HG_DV)].astype(BF16))
        on = o[s, h] * lax.rsqrt(ms[s, h] + EPS) * hg_g_ref[:, _cols(0, h, HG_DV)]
        mixed_ref[rows[s], _cols(0, h, HG_DV)] = on.astype(BF16) * gate


def _retention(n_streams, ts, proj_ref, mixed_ref, cos_t, sin_t, mdec_ref, qdec_ref, kdec_ref, rdec_ref,
               rt_g_ref, get_r, set_r):
    lane = lax.broadcasted_iota(jnp.int32, (ts, LANES), 1)
    first_half = (lane % RT_DK) < (RT_DK // 2)
    low_head = lane < RT_DK
    low_rows = lax.broadcasted_iota(jnp.int32, (2 * RT_DK, RT_DV), 0) < RT_DK

    def rope(xs, cs, sn):
        swapped = jnp.where(first_half, pltpu.roll(xs, LANES - RT_DK // 2, 1), pltpu.roll(xs, RT_DK // 2, 1))
        return xs * cs + swapped * sn

    pairs = [(s, p) for s in range(n_streams) for p in range(RT_H // 2)]
    insts = [(s, p, i) for (s, p) in pairs for i in range(2)]
    rows = {s: slice(s * ts, (s + 1) * ts) for s in range(n_streams)}
    qp, kp_b, qd, kd_b, r_pair, v_pair = {}, {}, {}, {}, {}, {}
    for (s, p) in pairs:
        cs, sn = cos_t[rows[s], :], sin_t[rows[s], :]
        q = rope(proj_ref[rows[s], _cols(OFF_RT_Q, p, LANES)], cs, sn)
        k = rope(proj_ref[rows[s], _cols(OFF_RT_K, p, LANES)], cs, sn)
        qp[s, p] = q
        qd[s, p] = q * qdec_ref[p]
        kp_b[s, p] = k.astype(BF16)
        kd_b[s, p] = (k * kdec_ref[p]).astype(BF16)
        r_pair[s, p] = get_r(s, p)
        v_pair[s, p] = proj_ref[rows[s], _cols(OFF_RT_V, p, 2 * RT_DV)].astype(BF16)
    yield
    sel = {0: low_head, 1: jnp.logical_not(low_head)}
    q_h = {(s, p, i): jnp.where(sel[i], qp[s, p], 0.0).astype(BF16) for (s, p, i) in insts}
    qd_h = {(s, p, i): jnp.where(sel[i], qd[s, p], 0.0).astype(BF16) for (s, p, i) in insts}
    sc = {(s, p, i): lax.dot_general(q_h[s, p, i], kp_b[s, p], _NT, preferred_element_type=F32)
          for (s, p, i) in insts}
    inter = {(s, p, i): jnp.dot(qd_h[s, p, i], r_pair[s, p].astype(BF16), preferred_element_type=F32)
             for (s, p, i) in insts}
    kv = {(s, p): lax.dot_general(kd_b[s, p], v_pair[s, p], _TN, preferred_element_type=F32) for (s, p) in pairs}
    yield
    for (s, p) in pairs:
        dr = jnp.where(low_rows, kv[s, p][:, :RT_DV], kv[s, p][:, RT_DV:])
        set_r(s, p, rdec_ref[p] * r_pair[s, p] + dr)
    sc_b = {(s, p, i): (sc[s, p, i] * mdec_ref[2 * p + i]).astype(BF16) for (s, p, i) in insts}
    yield
    o = {(s, p, i): jnp.dot(sc_b[s, p, i], v_pair[s, p][:, i * RT_DV:(i + 1) * RT_DV],
                            preferred_element_type=F32) + inter[s, p, i] for (s, p, i) in insts}
    yield
    mu = {k: jnp.mean(o[k], axis=-1, keepdims=True) for k in insts}
    cen = {k: o[k] - mu[k] for k in insts}
    var = {k: jnp.mean(cen[k] * cen[k], axis=-1, keepdims=True) for k in insts}
    yield
    for (s, p, i) in insts:
        h = 2 * p + i
        gate = _silu(proj_ref[rows[s], _cols(OFF_RT_GATE, h, RT_DV)].astype(BF16))
        on = cen[s, p, i] * lax.rsqrt(var[s, p, i] + EPS) * rt_g_ref[:, _cols(0, h, RT_DV)]
        mixed_ref[rows[s], _cols(HG_W, h, RT_DV)] = on.astype(BF16) * gate


def _cross_attention(n_streams, ts, proj_ref, mixed_ref, get_mem):
    xa_scale = XA_DH ** -0.5
    insts = [(s, h) for s in range(n_streams) for h in range(XA_H)]
    rows = {s: slice(s * ts, (s + 1) * ts) for s in range(n_streams)}
    mem = {k: get_mem(*k) for k in insts}
    sc = {(s, h): lax.dot_general(proj_ref[rows[s], _cols(OFF_XA_Q, h, XA_DH)].astype(BF16), mem[s, h][0], _NT,
                                  preferred_element_type=F32) * xa_scale for (s, h) in insts}
    yield
    mx = {k: jnp.max(sc[k], axis=-1, keepdims=True) for k in insts}
    yield
    pr = {k: jnp.exp(sc[k] - mx[k]) for k in insts}
    den = {k: jnp.sum(pr[k], axis=-1, keepdims=True) for k in insts}
    yield
    o = {k: jnp.dot(pr[k].astype(BF16), mem[k][1], preferred_element_type=F32) for k in insts}
    yield
    for (s, h) in insts:
        gate = _silu(proj_ref[rows[s], _cols(OFF_XA_GATE, h, XA_DH)].astype(BF16))
        mixed_ref[rows[s], _cols(HG_W + RT_W, h, XA_DH)] = (o[s, h] / den[s, h]).astype(BF16) * gate


def _chunk_cumsum(g):
    tokens, c = g.shape
    x = g.reshape(tokens // SUBLANES, SUBLANES, c)
    sub = lax.broadcasted_iota(jnp.int32, x.shape, 1)
    shift = 1
    while shift < SUBLANES:
        x = x + jnp.where(sub >= shift, pltpu.roll(x, shift, 1), 0.0)
        shift *= 2
    tiles = CHUNK // SUBLANES
    x = x.reshape(tokens // CHUNK, tiles, SUBLANES, c)
    out = [x[:, 0]]
    for j in range(1, tiles):
        total = out[-1][:, SUBLANES - 1:SUBLANES, :]
        out.append(x[:, j] + total)
    return jnp.stack(out, axis=1).reshape(tokens, c)


class _Projection:
    SEGMENTS = ((OFF_HG_F, OFF_HG_I), (OFF_HG_Q, OFF_HG_F), (OFF_HG_I, OFF_RT_Q), (OFF_RT_Q, D_IN))
    TILES_GATES = HG_W // PROJ_N_TILE
    TILES_HGRN = 4 * HG_W // PROJ_N_TILE
    TILES_ALL = D_IN // PROJ_N_TILE

    def __init__(self, x, norm_g_ref, w_in_ref, xn_ref, proj_ref):
        xn_ref[...] = _rms(x, norm_g_ref[...]).astype(BF16)
        self._refs = (xn_ref, w_in_ref, proj_ref)
        self._pending = [n0 for lo, hi in self.SEGMENTS for n0 in range(lo, hi, PROJ_N_TILE)]
        assert len(self._pending) == self.TILES_ALL
        self._done = 0

    def emit(self, n=1):
        xn_ref, w_in_ref, proj_ref = self._refs
        for _ in range(min(n, len(self._pending))):
            cols = slice(self._pending[0], self._pending.pop(0) + PROJ_N_TILE)
            proj_ref[:, cols] = jnp.dot(xn_ref[...], w_in_ref[:, cols], preferred_element_type=F32)
            self._done += 1

    def require(self, n_tiles):
        self.emit(n_tiles - self._done)


def _mix(n_streams, ts, x_ref, y_ref, cos_t, sin_t, mdec_ref, qdec_ref, kdec_ref, rdec_ref,
         get_mem, lb_ref, hg_g_ref, rt_g_ref, w_out_ref, fin_g_ref,
         proj_ref, gc_ref, mixed_ref, get_st, set_st, get_r, set_r, proj):
    proj.require(proj.TILES_GATES)
    lg = lb_ref[...]
    e = jnp.exp(lg - jnp.max(lg, axis=0, keepdims=True))
    lb = e[0:1, :] / jnp.sum(e, axis=0, keepdims=True)
    for c0 in range(0, HG_W, 2 * LANES):
        cols = slice(OFF_HG_F + c0, OFF_HG_F + c0 + 2 * LANES)
        lbc = lb[:, c0:c0 + 2 * LANES]
        half_span = 0.5 * (1.0 - lbc)
        bt = half_span * jnp.tanh(0.5 * proj_ref[:, cols])
        g = jnp.log2(lbc + (half_span + bt))
        proj_ref[:, cols] = half_span - bt
        gc_ref[:, c0:c0 + 2 * LANES] = _chunk_cumsum(g)
        proj.emit(PROJ_TILES_PER_GATE_GROUP)

    proj.require(proj.TILES_HGRN)
    row = lax.broadcasted_iota(jnp.int32, (CHUNK, LANES), 0)
    col = lax.broadcasted_iota(jnp.int32, (CHUNK, LANES), 1)
    half = CHUNK // 2
    causal = (((col <= row) & (col // half == row // half))
              | ((col >= CHUNK) & (col < CHUNK + half) & (row >= half)))
    for c in range(ts // CHUNK):
        chunks = [(s, slice(s * ts + c * CHUNK, s * ts + (c + 1) * CHUNK)) for s in range(n_streams)]
        for i in range(0, n_streams, HGRN_STREAMS_PER_STAGE):
            _hgrn_chunks(proj_ref, gc_ref, mixed_ref, hg_g_ref, chunks[i:i + HGRN_STREAMS_PER_STAGE], causal,
                         get_st, set_st, proj)

    proj.require(proj.TILES_ALL)

    def cross_attention_and_its_rows():
        yield from _cross_attention(n_streams, ts, proj_ref, mixed_ref, get_mem)
        yield from _out_projection(mixed_ref, w_out_ref, y_ref, HG_W + RT_W, D_MIX, y_ref)

    _run_interleaved(
        _retention(n_streams, ts, proj_ref, mixed_ref, cos_t, sin_t, mdec_ref, qdec_ref, kdec_ref, rdec_ref,
                   rt_g_ref, get_r, set_r),
        cross_attention_and_its_rows(),
        _out_projection(mixed_ref, w_out_ref, y_ref, 0, HG_W, x_ref))
    sumsq = []
    _run_interleaved(_out_projection(mixed_ref, w_out_ref, y_ref, HG_W, HG_W + RT_W, y_ref, sumsq))

    inv = lax.rsqrt(sum(sumsq) * (1.0 / D_MODEL) + EPS)
    y_ref[...] = y_ref[...] * inv * fin_g_ref[...]


def _out_projection(mixed_ref, w_out_ref, acc_ref, k_lo, k_hi, base_ref, sumsq=None):
    for n0 in range(0, D_MODEL, OUT_N_TILE):
        cols = slice(n0, n0 + OUT_N_TILE)
        out = base_ref[:, cols] + jnp.dot(mixed_ref[:, k_lo:k_hi], w_out_ref[k_lo:k_hi, cols],
                                          preferred_element_type=F32)
        acc_ref[:, cols] = out
        if sumsq is not None:
            sumsq.append(jnp.sum(out * out, axis=-1, keepdims=True))
        yield


class _StackedRows:
    def __init__(self, refs):
        self._refs = refs
        self._rows = refs[0].shape[0]

    def __getitem__(self, idx):
        cols = idx[1] if isinstance(idx, tuple) else slice(None)
        return jnp.concatenate([r[:, cols] for r in self._refs], axis=0)

    def __setitem__(self, idx, value):
        cols = idx[1] if isinstance(idx, tuple) else slice(None)
        for s, r in enumerate(self._refs):
            r[:, cols] = value[s * self._rows:(s + 1) * self._rows]


def _run_interleaved(*stage_generators):
    pending = list(stage_generators)
    while pending:
        for g in list(pending):
            try:
                next(g)
            except StopIteration:
                pending.remove(g)


def _memory_kv_kernel(mem_ref, g_ref, wk_ref, wv_ref, k_ref, v_ref, kb_ref, vb_ref):
    m = _rms(mem_ref[0], g_ref[...]).astype(BF16)
    k = jnp.dot(m, wk_ref[...].astype(BF16), preferred_element_type=F32)
    v = jnp.dot(m, wv_ref[...].astype(BF16), preferred_element_type=F32)
    for h in range(XA_H):
        k_ref[0, pl.ds(h, N_MEM, stride=XA_H), :] = k[:, _cols(0, h, XA_DH)]
        v_ref[0, pl.ds(h, N_MEM, stride=XA_H), :] = v[:, _cols(0, h, XA_DH)]
    kb_ref[0] = k.astype(BF16)
    vb_ref[0] = v.astype(BF16)


def _prompt_kernel(steps_per_stream, w_in_slab_ref, w_out_slab_ref,
                   x_ref, cos_ref, sin_ref, mdec_ref, qdec_ref, kdec_ref, rdec_ref, mk_ref, mv_ref,
                   norm_g_ref, lb_ref, hg_g_ref, rt_g_ref, fin_g_ref,
                   y_ref, s_out_ref, r_out_ref, w_in_b_slab_ref, w_out_b_slab_ref,
                   proj_ref, xn_ref, gc_ref, mixed_ref, st_ref, w_in_ref, w_out_ref):
    n_streams = x_ref.shape[0]
    t = pl.program_id(0)
    j = t - WEIGHT_PREP_STEPS
    running = t >= WEIGHT_PREP_STEPS

    @pl.when(jnp.logical_not(running))
    def _():
        for slab_ref, full_ref in ((w_in_slab_ref, w_in_ref), (w_out_slab_ref, w_out_ref)):
            n = slab_ref.shape[0]
            full_ref[pl.ds(pl.multiple_of(t * n, n), n), :] = slab_ref[...].astype(BF16)

    @pl.when(running & (j == 0))
    def _():
        st_ref[...] = jnp.zeros_like(st_ref)
        r_out_ref[...] = jnp.zeros_like(r_out_ref)

    def set_st(s, h, val):
        st_ref[s, h] = val

    def set_r(s, p, val):
        r_out_ref[s, p] = val

    def get_mem(s, h):
        return mk_ref[s, :, _cols(0, h, XA_DH)], mv_ref[s, :, _cols(0, h, XA_DH)]

    def token_block(i, carry):
        rows = pl.ds(pl.multiple_of(i * TOKEN_BLOCK, TOKEN_BLOCK), TOKEN_BLOCK)
        x_blk = _StackedRows([x_ref.at[s, rows] for s in range(n_streams)])
        y_blk = _StackedRows([y_ref.at[s, rows] for s in range(n_streams)])
        cos_t = jnp.concatenate([cos_ref[rows, :]] * n_streams, axis=0)
        sin_t = jnp.concatenate([sin_ref[rows, :]] * n_streams, axis=0)
        proj = _Projection(x_blk[...], norm_g_ref, w_in_ref, xn_ref, proj_ref)
        _mix(
            n_streams, TOKEN_BLOCK, x_blk, y_blk, cos_t, sin_t,
            mdec_ref, qdec_ref, kdec_ref, rdec_ref,
            get_mem, lb_ref, hg_g_ref, rt_g_ref, w_out_ref, fin_g_ref,
            proj_ref, gc_ref, mixed_ref,
            lambda s, h: st_ref[s, h], set_st, lambda s, p: r_out_ref[s, p], set_r, proj)
        return carry

    @pl.when(running)
    def _():
        slab_id = jnp.minimum(j, WEIGHT_PREP_STEPS - 1)
        for out_ref, full_ref in ((w_in_b_slab_ref, w_in_ref), (w_out_b_slab_ref, w_out_ref)):
            n = out_ref.shape[0]
            out_ref[...] = full_ref[pl.ds(pl.multiple_of(slab_id * n, n), n), :]
        lax.fori_loop(0, x_ref.shape[1] // TOKEN_BLOCK, token_block, 0)

    @pl.when(running & (j == steps_per_stream - 1))
    def _():
        for s in range(n_streams):
            for h in range(HG_H):
                s_out_ref[s, h] = st_ref[s, h].T


def _sample_kernel(x_ref, cos_ref, sin_ref, mdec_ref, qdec_ref, kdec_ref, rdec_ref, mk_ref, mv_ref,
                   s_in_ref, r_in_ref,
                   norm_g_ref, w_in_ref, lb_ref, hg_g_ref, rt_g_ref, w_out_ref, fin_g_ref,
                   y_ref, s_out_ref, r_out_ref, proj_ref, xn_ref, gc_ref, mixed_ref):
    n_streams = TOKEN_BLOCK // CHUNK

    def set_st(s, h, val):
        s_out_ref[s, h] = val.T

    def set_r(s, p, val):
        r_out_ref[s, p] = val

    def get_mem(s, h):
        head_rows = pl.ds(h, N_MEM, stride=XA_H)
        return mk_ref[s, head_rows, :].astype(BF16), mv_ref[s, head_rows, :].astype(BF16)

    cos_t = jnp.concatenate([cos_ref[...]] * n_streams, axis=0)
    sin_t = jnp.concatenate([sin_ref[...]] * n_streams, axis=0)
    proj = _Projection(x_ref[...], norm_g_ref, w_in_ref, xn_ref, proj_ref)
    _mix(
        n_streams, CHUNK, x_ref, y_ref, cos_t, sin_t, mdec_ref, qdec_ref, kdec_ref, rdec_ref,
        get_mem, lb_ref, hg_g_ref, rt_g_ref, w_out_ref, fin_g_ref,
        proj_ref, gc_ref, mixed_ref,
        lambda s, h: s_in_ref[s, h].T, set_st, lambda s, p: r_in_ref[s, p], set_r, proj)


def _const_spec(shape):
    zeros = (0,) * len(shape)
    return pl.BlockSpec(shape, lambda *_: zeros, pipeline_mode=pl.Buffered(1))


def _layer_scratch(rows):
    return [pltpu.VMEM((rows, D_IN), F32),
            pltpu.VMEM((rows, D_MODEL), BF16),
            pltpu.VMEM((rows, HG_W), F32),
            pltpu.VMEM((rows, D_MIX), BF16)]


def _memory_kv(mem, g, wk, wv):
    b = mem.shape[0]
    out_f = jax.ShapeDtypeStruct((b, N_MEM * XA_H, XA_DH), F32)
    out_b = jax.ShapeDtypeStruct((b, N_MEM, XA_W), BF16)
    blk_f = pl.BlockSpec((1, N_MEM * XA_H, XA_DH), lambda i: (i, 0, 0))
    blk_b = pl.BlockSpec((1, N_MEM, XA_W), lambda i: (i, 0, 0))
    return pl.pallas_call(
        _memory_kv_kernel,
        grid=(b,),
        in_specs=[pl.BlockSpec((1, N_MEM, D_MODEL), lambda i: (i, 0, 0)),
                  pl.BlockSpec((1, D_MODEL), lambda i: (0, 0)),
                  pl.BlockSpec((D_MODEL, XA_W), lambda i: (0, 0)),
                  pl.BlockSpec((D_MODEL, XA_W), lambda i: (0, 0))],
        out_specs=[blk_f, blk_f, blk_b, blk_b],
        out_shape=[out_f, out_f, out_b, out_b],
        name="memory_kv",
    )(mem, g.reshape(1, D_MODEL), wk, wv)


def _prompt_layer(x, mk_b, mv_b, w_in, w_out, params):
    b, t, _ = x.shape
    tb = PROMPT_BLOCKS_PER_STEP * TOKEN_BLOCK
    spp = t // tb
    prep = WEIGHT_PREP_STEPS
    assert spp >= prep
    cos_t, sin_t = _rope_tables(np.arange(t))
    mdec, qdec, kdec, rdec = _retention_tables(TOKEN_BLOCK)
    consts = (mdec, qdec, kdec, rdec)
    in_rows, out_rows = D_MODEL // prep, D_MIX // prep

    def slab(step):
        return jnp.minimum(step, prep - 1)

    def run(step):
        return jnp.maximum(step - prep, 0)

    in_specs = ([pl.BlockSpec((in_rows, D_IN), lambda s: (slab(s), 0)),
                 pl.BlockSpec((out_rows, D_MODEL), lambda s: (slab(s), 0)),
                 pl.BlockSpec((b, tb, D_MODEL), lambda s: (0, run(s), 0)),
                 pl.BlockSpec((tb, LANES), lambda s: (run(s), 0)),
                 pl.BlockSpec((tb, LANES), lambda s: (run(s), 0))]
                + [_const_spec(c.shape) for c in consts]
                + [_const_spec(mk_b.shape), _const_spec(mv_b.shape)]
                + [_const_spec(p.shape) for p in params])
    out_shape = [jax.ShapeDtypeStruct((b, t, D_MODEL), F32),
                 jax.ShapeDtypeStruct((b, HG_H, HG_DK, HG_DV), F32),
                 jax.ShapeDtypeStruct((b, RT_H // 2, 2 * RT_DK, RT_DV), F32),
                 jax.ShapeDtypeStruct((D_MODEL, D_IN), BF16),
                 jax.ShapeDtypeStruct((D_MIX, D_MODEL), BF16)]
    out_specs = [pl.BlockSpec((b, tb, D_MODEL), lambda s: (0, run(s), 0)),
                 pl.BlockSpec((b, HG_H, HG_DK, HG_DV), lambda s: (0, 0, 0, 0)),
                 pl.BlockSpec((b, RT_H // 2, 2 * RT_DK, RT_DV), lambda s: (0, 0, 0, 0)),
                 pl.BlockSpec((in_rows, D_IN), lambda s: (slab(run(s)), 0)),
                 pl.BlockSpec((out_rows, D_MODEL), lambda s: (slab(run(s)), 0))]
    return pl.pallas_call(
        functools.partial(_prompt_kernel, spp),
        grid=(prep + spp,),
        in_specs=in_specs,
        out_specs=out_specs,
        out_shape=out_shape,
        scratch_shapes=_layer_scratch(b * TOKEN_BLOCK) + [
                                           pltpu.VMEM((b, HG_H, HG_DV, HG_DK), F32),
                                           pltpu.VMEM((D_MODEL, D_IN), BF16),
                                           pltpu.VMEM((D_MIX, D_MODEL), BF16)],
        compiler_params=pltpu.CompilerParams(dimension_semantics=("arbitrary",),
                                             vmem_limit_bytes=VMEM_LIMIT_BYTES),
        name="prompt_layer",
    )(w_in, w_out, x, cos_t, sin_t, *consts, mk_b, mv_b, *params)


def _sample_layer(x, s_hg, s_rt, mem_k, mem_v, weights):
    b, t, _ = x.shape
    assert t == CHUNK
    tb = TOKEN_BLOCK
    nsb = tb // t
    cos_t, sin_t = _rope_tables(PAST_LEN + np.arange(t))
    mdec, qdec, kdec, rdec = _retention_tables(t)
    consts = (mdec, qdec, kdec, rdec)
    s_rt2 = s_rt.reshape(b, RT_H // 2, 2 * RT_DK, RT_DV)
    mem_spec = pl.BlockSpec((nsb, N_MEM * XA_H, XA_DH), lambda i: (i, 0, 0))
    in_specs = ([pl.BlockSpec((tb, D_MODEL), lambda i: (i, 0)),
                 _const_spec(cos_t.shape), _const_spec(sin_t.shape)]
                + [_const_spec(c.shape) for c in consts]
                + [mem_spec, mem_spec,
                   pl.BlockSpec((nsb, HG_H, HG_DK, HG_DV), lambda i: (i, 0, 0, 0)),
                   pl.BlockSpec((nsb, RT_H // 2, 2 * RT_DK, RT_DV), lambda i: (i, 0, 0, 0))]
                + [_const_spec(w.shape) for w in weights])
    out_shape = [jax.ShapeDtypeStruct((b * t, D_MODEL), F32),
                 jax.ShapeDtypeStruct((b, HG_H, HG_DK, HG_DV), F32),
                 jax.ShapeDtypeStruct((b, RT_H // 2, 2 * RT_DK, RT_DV), F32)]
    out_specs = [pl.BlockSpec((tb, D_MODEL), lambda i: (i, 0)),
                 pl.BlockSpec((nsb, HG_H, HG_DK, HG_DV), lambda i: (i, 0, 0, 0)),
                 pl.BlockSpec((nsb, RT_H // 2, 2 * RT_DK, RT_DV), lambda i: (i, 0, 0, 0))]
    y, s_new, r_new = pl.pallas_call(
        _sample_kernel,
        grid=(b // nsb,),
        in_specs=in_specs,
        out_specs=out_specs,
        out_shape=out_shape,
        scratch_shapes=_layer_scratch(tb),
        compiler_params=pltpu.CompilerParams(dimension_semantics=("arbitrary",),
                                             vmem_limit_bytes=VMEM_LIMIT_BYTES),
        name="sample_layer",
    )(x.reshape(b * t, D_MODEL), cos_t, sin_t, *consts,
      mem_k.reshape(b, N_MEM * XA_H, XA_DH), mem_v.reshape(b, N_MEM * XA_H, XA_DH), s_hg, s_rt2, *weights)
    return y.reshape(b, t, D_MODEL), s_new, r_new


def kernel(x_prompt, x_sample, mem_prompt, state_hgrn, state_ret, cache_mem_k, cache_mem_v, norm_g, w_in,
           lb_logits, hg_norm_g, rt_norm_g, mem_norm_g, w_mem_k, w_mem_v, w_out, final_norm_g):
    depth = w_in.shape[0]
    assert depth == 1, "single-layer step"
    bp = x_prompt.shape[0]
    bs = x_sample.shape[0]
    g_in, g_hg, g_rt = norm_g[0].reshape(1, D_MODEL), hg_norm_g[0].reshape(1, HG_W), rt_norm_g[0].reshape(1, RT_W)
    g_fin = final_norm_g.reshape(1, D_MODEL)
    mk, mv, mk_b, mv_b = _memory_kv(mem_prompt, mem_norm_g[0], w_mem_k[0], w_mem_v[0])
    y_p, s_p, r_p, w_in_b, w_out_b = _prompt_layer(x_prompt, mk_b, mv_b, w_in[0], w_out[0],
                                                   (g_in, lb_logits, g_hg, g_rt, g_fin))
    y_s, s_s, r_s = _sample_layer(x_sample, state_hgrn[0], state_ret[0], cache_mem_k[0], cache_mem_v[0],
                                  (g_in, w_in_b, lb_logits, g_hg, g_rt, w_out_b, g_fin))
    return (y_p, y_s,
            s_p.reshape(1, bp, HG_H, HG_DK, HG_DV),
            r_p.reshape(1, bp, RT_H, RT_DK, RT_DV),
            mk.reshape(1, bp, N_MEM, XA_H, XA_DH),
            mv.reshape(1, bp, N_MEM, XA_H, XA_DH),
            s_s.reshape(1, bs, HG_H, HG_DK, HG_DV),
            r_s.reshape(1, bs, RT_H, RT_DK, RT_DV))
```

```python
import functools

import numpy as np
import jax
import jax.numpy as jnp
from jax import lax
from jax.experimental import pallas as pl
from jax.experimental.pallas import tpu as pltpu

F32 = jnp.float32
BF16 = jnp.bfloat16

D_MODEL = 1024
PAST_LEN = 4096
CHUNK = 64
N_MEM = 256
EPS = 1e-6
ROPE_BASE = 10000.0
HG_H, HG_DK, HG_DV = 8, 128, 128
HG_W = HG_H * HG_DV
RT_H, RT_DK, RT_DV = 4, 64, 128
RT_QK = RT_H * RT_DK
RT_W = RT_H * RT_DV
XA_H, XA_DH = 4, 128
XA_W = XA_H * XA_DH
D_MIX = HG_W + RT_W + XA_W
OFF_HG_Q = 0
OFF_HG_F = OFF_HG_Q + HG_W
OFF_HG_I = OFF_HG_F + HG_W
OFF_HG_GATE = OFF_HG_I + HG_W
OFF_RT_Q = OFF_HG_GATE + HG_W
OFF_RT_K = OFF_RT_Q + RT_QK
OFF_RT_V = OFF_RT_K + RT_QK
OFF_RT_GATE = OFF_RT_V + RT_W
OFF_XA_Q = OFF_RT_GATE + RT_W
OFF_XA_GATE = OFF_XA_Q + XA_W
D_IN = OFF_XA_GATE + XA_W

LANES = 128
SUBLANES = 8
TOKEN_BLOCK = 256
PROMPT_BLOCKS_PER_STEP = 1
WEIGHT_PREP_STEPS = 8
PROJ_N_TILE = 512
PROJ_TILES_PER_GATE_GROUP = 1
PROJ_TILES_PER_HGRN_STAGE = 1
HGRN_STREAMS_PER_STAGE = 4
OUT_N_TILE = 512
VMEM_LIMIT_BYTES = 60 * 1024 * 1024
_NT = (((1,), (1,)), ((), ()))
_TN = (((0,), (0,)), ((), ()))


def _silu(x):
    h = 0.5 * x
    return h + h * jnp.tanh(h)


def _rms(x, g):
    return x * lax.rsqrt(jnp.mean(x * x, axis=-1, keepdims=True) + EPS) * g


def _cols(off, i, width):
    return slice(off + i * width, off + (i + 1) * width)


def _rope_tables(pos):
    half = RT_DK // 2
    inv_freq = ROPE_BASE ** (-np.arange(half, dtype=np.float64) / half)
    ang = np.asarray(pos, np.float64)[:, None] * inv_freq[None, :]
    cos, sin = np.cos(ang), np.sin(ang)
    cos_t = np.concatenate([cos, cos, cos, cos], axis=-1)
    sin_t = np.concatenate([-sin, sin, -sin, sin], axis=-1)
    return jnp.asarray(cos_t, F32), jnp.asarray(sin_t, F32)


def _retention_tables(ts):
    h = np.arange(RT_H, dtype=np.float64)
    log_gamma = np.log1p(-np.exp2(-5.0 - h))
    n = np.arange(ts, dtype=np.float64)
    scale = RT_DK ** -0.5
    dist = np.abs(n[:, None] - n[None, :])
    visible = (n[None, :] // CHUNK) <= (n[:, None] // CHUNK)
    mdec = scale * np.exp(log_gamma[:, None, None] * dist[None]) * visible[None]
    qd = np.exp(log_gamma[:, None] * (n[None, :] + 1.0))
    kd = scale * np.exp(log_gamma[:, None] * (ts - 1.0 - n[None, :]))
    rd = np.exp(log_gamma * ts)

    def pair_lanes(a):
        a = np.repeat(a[:, :, None], RT_DK, axis=2)
        return np.concatenate([a[0::2], a[1::2]], axis=2)

    rdec = np.repeat(np.repeat(rd[:, None], RT_DK, axis=1).reshape(RT_H // 2, 2 * RT_DK, 1), LANES, axis=2)
    return (jnp.asarray(mdec, F32), jnp.asarray(pair_lanes(qd), F32), jnp.asarray(pair_lanes(kd), F32),
            jnp.asarray(rdec, F32))


def _hgrn_chunks(proj_ref, gc_ref, mixed_ref, hg_g_ref, chunks, score_mask, get_st, set_st, proj):
    half = CHUNK // 2
    zero_half = jnp.zeros((half, HG_DK), BF16)
    insts = [(s, h) for s, _ in chunks for h in range(HG_H)]
    rows = dict(chunks)
    q_rel, k_rel, q_abs, k_end, v_b, dec = {}, {}, {}, {}, {}, {}
    for (s, h) in insts:
        q = _silu(proj_ref[rows[s], _cols(OFF_HG_Q, h, HG_DK)])
        k = proj_ref[rows[s], _cols(OFF_HG_F, h, HG_DK)]
        gc = gc_ref[rows[s], _cols(0, h, HG_DK)]
        g_end = gc[CHUNK - 1:CHUNK, :]
        (q1, q2), (k1, k2), (g1, g2) = ((a[:half], a[half:]) for a in (q, k, gc))
        g_m1, g_m2 = gc[half // 2 - 1:half // 2, :], gc[half + half // 2 - 1:half + half // 2, :]
        g_b = gc[half - 1:half, :]
        qd1, kd1 = (q1 * jnp.exp2(g1 - g_m1)).astype(BF16), (k1 * jnp.exp2(g_m1 - g1)).astype(BF16)
        qd2, kd2 = (q2 * jnp.exp2(g2 - g_m2)).astype(BF16), (k2 * jnp.exp2(g_m2 - g2)).astype(BF16)
        qo, ko = (q2 * jnp.exp2(g2 - g_b)).astype(BF16), (k1 * jnp.exp2(g_b - g1)).astype(BF16)
        q_rel[s, h] = jnp.concatenate([jnp.concatenate([qd1, qd2], axis=0),
                                       jnp.concatenate([zero_half, qo], axis=0)], axis=1)
        k_rel[s, h] = jnp.concatenate([jnp.concatenate([kd1, kd2, zero_half, zero_half], axis=0),
                                       jnp.concatenate([zero_half, zero_half, ko, zero_half], axis=0)],
                                      axis=1)
        q_abs[s, h] = (q * jnp.exp2(gc)).astype(BF16)
        k_end[s, h] = (k * jnp.exp2(g_end - gc)).astype(BF16)
        v_b[s, h] = proj_ref[rows[s], _cols(OFF_HG_I, h, HG_DV)].astype(BF16)
        dec[s, h] = jnp.exp2(g_end)
    proj.emit(PROJ_TILES_PER_HGRN_STAGE)
    a = {k: lax.dot_general(q_rel[k], k_rel[k], _NT, preferred_element_type=F32) for k in insts}
    st = {k: get_st(*k) for k in insts}
    v_t = {k: jnp.concatenate([v_b[k], v_b[k][:half], zero_half], axis=0).T for k in insts}
    kv = {k: jnp.dot(v_t[k][:, :CHUNK], k_end[k], preferred_element_type=F32) for k in insts}
    for k in insts:
        set_st(*k, dec[k] * st[k] + kv[k])
    a_m = {k: jnp.where(score_mask, a[k], 0.0).astype(BF16) for k in insts}
    o = {k: lax.dot_general(jnp.concatenate([q_abs[k], a_m[k]], axis=1),
                            jnp.concatenate([st[k].astype(BF16), v_t[k]], axis=1), _NT, preferred_element_type=F32)
         for k in insts}
    ms = {k: jnp.mean(o[k] * o[k], axis=-1, keepdims=True) for k in insts}
    for (s, h) in insts:
        gate = _silu(proj_ref[rows[s], _cols(OFF_HG_GATE, h, HG_DV)].astype(BF16))
        on = o[s, h] * lax.rsqrt(ms[s, h] + EPS) * hg_g_ref[:, _cols(0, h, HG_DV)]
        mixed_ref[rows[s], _cols(0, h, HG_DV)] = on.astype(BF16) * gate


def _retention(n_streams, ts, proj_ref, mixed_ref, cos_t, sin_t, mdec_ref, qdec_ref, kdec_ref, rdec_ref,
               rt_g_ref, get_r, set_r):
    lane = lax.broadcasted_iota(jnp.int32, (ts, LANES), 1)
    first_half = (lane % RT_DK) < (RT_DK // 2)
    low_head = lane < RT_DK
    low_rows = lax.broadcasted_iota(jnp.int32, (2 * RT_DK, RT_DV), 0) < RT_DK

    def rope(xs, cs, sn):
        swapped = jnp.where(first_half, pltpu.roll(xs, LANES - RT_DK // 2, 1), pltpu.roll(xs, RT_DK // 2, 1))
        return xs * cs + swapped * sn

    pairs = [(s, p) for s in range(n_streams) for p in range(RT_H // 2)]
    insts = [(s, p, i) for (s, p) in pairs for i in range(2)]
    rows = {s: slice(s * ts, (s + 1) * ts) for s in range(n_streams)}
    qp, kp_b, qd, kd_b, r_pair, v_pair = {}, {}, {}, {}, {}, {}
    for (s, p) in pairs:
        cs, sn = cos_t[rows[s], :], sin_t[rows[s], :]
        q = rope(proj_ref[rows[s], _cols(OFF_RT_Q, p, LANES)], cs, sn)
        k = rope(proj_ref[rows[s], _cols(OFF_RT_K, p, LANES)], cs, sn)
        qp[s, p] = q
        qd[s, p] = q * qdec_ref[p]
        kp_b[s, p] = k.astype(BF16)
        kd_b[s, p] = (k * kdec_ref[p]).astype(BF16)
        r_pair[s, p] = get_r(s, p)
        v_pair[s, p] = proj_ref[rows[s], _cols(OFF_RT_V, p, 2 * RT_DV)].astype(BF16)
    yield
    sel = {0: low_head, 1: jnp.logical_not(low_head)}
    q_h = {(s, p, i): jnp.where(sel[i], qp[s, p], 0.0).astype(BF16) for (s, p, i) in insts}
    qd_h = {(s, p, i): jnp.where(sel[i], qd[s, p], 0.0).astype(BF16) for (s, p, i) in insts}
    sc = {(s, p, i): lax.dot_general(q_h[s, p, i], kp_b[s, p], _NT, preferred_element_type=F32)
          for (s, p, i) in insts}
    inter = {(s, p, i): jnp.dot(qd_h[s, p, i], r_pair[s, p].astype(BF16), preferred_element_type=F32)
             for (s, p, i) in insts}
    kv = {(s, p): lax.dot_general(kd_b[s, p], v_pair[s, p], _TN, preferred_element_type=F32) for (s, p) in pairs}
    yield
    for (s, p) in pairs:
        dr = jnp.where(low_rows, kv[s, p][:, :RT_DV], kv[s, p][:, RT_DV:])
        set_r(s, p, rdec_ref[p] * r_pair[s, p] + dr)
    sc_b = {(s, p, i): (sc[s, p, i] * mdec_ref[2 * p + i]).astype(BF16) for (s, p, i) in insts}
    yield
    o = {(s, p, i): jnp.dot(sc_b[s, p, i], v_pair[s, p][:, i * RT_DV:(i + 1) * RT_DV],
                            preferred_element_type=F32) + inter[s, p, i] for (s, p, i) in insts}
    yield
    mu = {k: jnp.mean(o[k], axis=-1, keepdims=True) for k in insts}
    cen = {k: o[k] - mu[k] for k in insts}
    var = {k: jnp.mean(cen[k] * cen[k], axis=-1, keepdims=True) for k in insts}
    yield
    for (s, p, i) in insts:
        h = 2 * p + i
        gate = _silu(proj_ref[rows[s], _cols(OFF_RT_GATE, h, RT_DV)].astype(BF16))
        on = cen[s, p, i] * lax.rsqrt(var[s, p, i] + EPS) * rt_g_ref[:, _cols(0, h, RT_DV)]
        mixed_ref[rows[s], _cols(HG_W, h, RT_DV)] = on.astype(BF16) * gate


def _cross_attention(n_streams, ts, proj_ref, mixed_ref, get_mem):
    xa_scale = XA_DH ** -0.5
    insts = [(s, h) for s in range(n_streams) for h in range(XA_H)]
    rows = {s: slice(s * ts, (s + 1) * ts) for s in range(n_streams)}
    mem = {k: get_mem(*k) for k in insts}
    sc = {(s, h): lax.dot_general(proj_ref[rows[s], _cols(OFF_XA_Q, h, XA_DH)].astype(BF16), mem[s, h][0], _NT,
                                  preferred_element_type=F32) * xa_scale for (s, h) in insts}
    yield
    mx = {k: jnp.max(sc[k], axis=-1, keepdims=True) for k in insts}
    yield
    pr = {k: jnp.exp(sc[k] - mx[k]) for k in insts}
    den = {k: jnp.sum(pr[k], axis=-1, keepdims=True) for k in insts}
    yield
    o = {k: jnp.dot(pr[k].astype(BF16), mem[k][1], preferred_element_type=F32) for k in insts}
    yield
    for (s, h) in insts:
        gate = _silu(proj_ref[rows[s], _cols(OFF_XA_GATE, h, XA_DH)].astype(BF16))
        mixed_ref[rows[s], _cols(HG_W + RT_W, h, XA_DH)] = (o[s, h] / den[s, h]).astype(BF16) * gate


def _chunk_cumsum(g):
    tokens, c = g.shape
    x = g.reshape(tokens // SUBLANES, SUBLANES, c)
    sub = lax.broadcasted_iota(jnp.int32, x.shape, 1)
    shift = 1
    while shift < SUBLANES:
        x = x + jnp.where(sub >= shift, pltpu.roll(x, shift, 1), 0.0)
        shift *= 2
    tiles = CHUNK // SUBLANES
    x = x.reshape(tokens // CHUNK, tiles, SUBLANES, c)
    out = [x[:, 0]]
    for j in range(1, tiles):
        total = out[-1][:, SUBLANES - 1:SUBLANES, :]
        out.append(x[:, j] + total)
    return jnp.stack(out, axis=1).reshape(tokens, c)


class _Projection:
    SEGMENTS = ((OFF_HG_F, OFF_HG_I), (OFF_HG_Q, OFF_HG_F), (OFF_HG_I, OFF_RT_Q), (OFF_RT_Q, D_IN))
    TILES_GATES = HG_W // PROJ_N_TILE
    TILES_HGRN = 4 * HG_W // PROJ_N_TILE
    TILES_ALL = D_IN // PROJ_N_TILE

    def __init__(self, x, norm_g_ref, w_in_ref, xn_ref, proj_ref):
        xn_ref[...] = _rms(x, norm_g_ref[...]).astype(BF16)
        self._refs = (xn_ref, w_in_ref, proj_ref)
        self._pending = [n0 for lo, hi in self.SEGMENTS for n0 in range(lo, hi, PROJ_N_TILE)]
        assert len(self._pending) == self.TILES_ALL
        self._done = 0

    def emit(self, n=1):
        xn_ref, w_in_ref, proj_ref = self._refs
        for _ in range(min(n, len(self._pending))):
            cols = slice(self._pending[0], self._pending.pop(0) + PROJ_N_TILE)
            proj_ref[:, cols] = jnp.dot(xn_ref[...], w_in_ref[:, cols], preferred_element_type=F32)
            self._done += 1

    def require(self, n_tiles):
        self.emit(n_tiles - self._done)


def _mix(n_streams, ts, x_ref, y_ref, cos_t, sin_t, mdec_ref, qdec_ref, kdec_ref, rdec_ref,
         get_mem, lb_ref, hg_g_ref, rt_g_ref, w_out_ref, fin_g_ref,
         proj_ref, gc_ref, mixed_ref, get_st, set_st, get_r, set_r, proj):
    proj.require(proj.TILES_GATES)
    lg = lb_ref[...]
    e = jnp.exp(lg - jnp.max(lg, axis=0, keepdims=True))
    lb = e[0:1, :] / jnp.sum(e, axis=0, keepdims=True)
    for c0 in range(0, HG_W, 2 * LANES):
        cols = slice(OFF_HG_F + c0, OFF_HG_F + c0 + 2 * LANES)
        lbc = lb[:, c0:c0 + 2 * LANES]
        half_span = 0.5 * (1.0 - lbc)
        bt = half_span * jnp.tanh(0.5 * proj_ref[:, cols])
        g = jnp.log2(lbc + (half_span + bt))
        proj_ref[:, cols] = half_span - bt
        gc_ref[:, c0:c0 + 2 * LANES] = _chunk_cumsum(g)
        proj.emit(PROJ_TILES_PER_GATE_GROUP)

    proj.require(proj.TILES_HGRN)
    row = lax.broadcasted_iota(jnp.int32, (CHUNK, LANES), 0)
    col = lax.broadcasted_iota(jnp.int32, (CHUNK, LANES), 1)
    half = CHUNK // 2
    causal = (((col <= row) & (col // half == row // half))
              | ((col >= CHUNK) & (col < CHUNK + half) & (row >= half)))
    for c in range(ts // CHUNK):
        chunks = [(s, slice(s * ts + c * CHUNK, s * ts + (c + 1) * CHUNK)) for s in range(n_streams)]
        for i in range(0, n_streams, HGRN_STREAMS_PER_STAGE):
            _hgrn_chunks(proj_ref, gc_ref, mixed_ref, hg_g_ref, chunks[i:i + HGRN_STREAMS_PER_STAGE], causal,
                         get_st, set_st, proj)

    proj.require(proj.TILES_ALL)

    def cross_attention_and_its_rows():
        yield from _cross_attention(n_streams, ts, proj_ref, mixed_ref, get_mem)
        yield from _out_projection(mixed_ref, w_out_ref, y_ref, HG_W + RT_W, D_MIX, y_ref)

    _run_interleaved(
        _retention(n_streams, ts, proj_ref, mixed_ref, cos_t, sin_t, mdec_ref, qdec_ref, kdec_ref, rdec_ref,
                   rt_g_ref, get_r, set_r),
        cross_attention_and_its_rows(),
        _out_projection(mixed_ref, w_out_ref, y_ref, 0, HG_W, x_ref))
    sumsq = []
    _run_interleaved(_out_projection(mixed_ref, w_out_ref, y_ref, HG_W, HG_W + RT_W, y_ref, sumsq))

    inv = lax.rsqrt(sum(sumsq) * (1.0 / D_MODEL) + EPS)
    y_ref[...] = y_ref[...] * inv * fin_g_ref[...]


def _out_projection(mixed_ref, w_out_ref, acc_ref, k_lo, k_hi, base_ref, sumsq=None):
    for n0 in range(0, D_MODEL, OUT_N_TILE):
        cols = slice(n0, n0 + OUT_N_TILE)
        out = base_ref[:, cols] + jnp.dot(mixed_ref[:, k_lo:k_hi], w_out_ref[k_lo:k_hi, cols],
                                          preferred_element_type=F32)
        acc_ref[:, cols] = out
        if sumsq is not None:
            sumsq.append(jnp.sum(out * out, axis=-1, keepdims=True))
        yield


class _StackedRows:
    def __init__(self, refs):
        self._refs = refs
        self._rows = refs[0].shape[0]

    def __getitem__(self, idx):
        cols = idx[1] if isinstance(idx, tuple) else slice(None)
        return jnp.concatenate([r[:, cols] for r in self._refs], axis=0)

    def __setitem__(self, idx, value):
        cols = idx[1] if isinstance(idx, tuple) else slice(None)
        for s, r in enumerate(self._refs):
            r[:, cols] = value[s * self._rows:(s + 1) * self._rows]


def _run_interleaved(*stage_generators):
    pending = list(stage_generators)
    while pending:
        for g in list(pending):
            try:
                next(g)
            except StopIteration:
                pending.remove(g)


def _memory_kv_kernel(mem_ref, g_ref, wk_ref, wv_ref, k_ref, v_ref, kb_ref, vb_ref):
    m = _rms(mem_ref[0], g_ref[...]).astype(BF16)
    k = jnp.dot(m, wk_ref[...].astype(BF16), preferred_element_type=F32)
    v = jnp.dot(m, wv_ref[...].astype(BF16), preferred_element_type=F32)
    for h in range(XA_H):
        k_ref[0, pl.ds(h, N_MEM, stride=XA_H), :] = k[:, _cols(0, h, XA_DH)]
        v_ref[0, pl.ds(h, N_MEM, stride=XA_H), :] = v[:, _cols(0, h, XA_DH)]
    kb_ref[0] = k.astype(BF16)
    vb_ref[0] = v.astype(BF16)


def _prompt_kernel(steps_per_stream, w_in_slab_ref, w_out_slab_ref,
                   x_ref, cos_ref, sin_ref, mdec_ref, qdec_ref, kdec_ref, rdec_ref, mk_ref, mv_ref,
                   norm_g_ref, lb_ref, hg_g_ref, rt_g_ref, fin_g_ref,
                   y_ref, s_out_ref, r_out_ref, w_in_b_slab_ref, w_out_b_slab_ref,
                   proj_ref, xn_ref, gc_ref, mixed_ref, st_ref, w_in_ref, w_out_ref):
    n_streams = x_ref.shape[0]
    t = pl.program_id(0)
    j = t - WEIGHT_PREP_STEPS
    running = t >= WEIGHT_PREP_STEPS

    @pl.when(jnp.logical_not(running))
    def _():
        for slab_ref, full_ref in ((w_in_slab_ref, w_in_ref), (w_out_slab_ref, w_out_ref)):
            n = slab_ref.shape[0]
            full_ref[pl.ds(pl.multiple_of(t * n, n), n), :] = slab_ref[...].astype(BF16)

    @pl.when(running & (j == 0))
    def _():
        st_ref[...] = jnp.zeros_like(st_ref)
        r_out_ref[...] = jnp.zeros_like(r_out_ref)

    def set_st(s, h, val):
        st_ref[s, h] = val

    def set_r(s, p, val):
        r_out_ref[s, p] = val

    def get_mem(s, h):
        return mk_ref[s, :, _cols(0, h, XA_DH)], mv_ref[s, :, _cols(0, h, XA_DH)]

    def token_block(i, carry):
        rows = pl.ds(pl.multiple_of(i * TOKEN_BLOCK, TOKEN_BLOCK), TOKEN_BLOCK)
        x_blk = _StackedRows([x_ref.at[s, rows] for s in range(n_streams)])
        y_blk = _StackedRows([y_ref.at[s, rows] for s in range(n_streams)])
        cos_t = jnp.concatenate([cos_ref[rows, :]] * n_streams, axis=0)
        sin_t = jnp.concatenate([sin_ref[rows, :]] * n_streams, axis=0)
        proj = _Projection(x_blk[...], norm_g_ref, w_in_ref, xn_ref, proj_ref)
        _mix(
            n_streams, TOKEN_BLOCK, x_blk, y_blk, cos_t, sin_t,
            mdec_ref, qdec_ref, kdec_ref, rdec_ref,
            get_mem, lb_ref, hg_g_ref, rt_g_ref, w_out_ref, fin_g_ref,
            proj_ref, gc_ref, mixed_ref,
            lambda s, h: st_ref[s, h], set_st, lambda s, p: r_out_ref[s, p], set_r, proj)
        return carry

    @pl.when(running & (j < WEIGHT_PREP_STEPS))
    def _():
        for out_ref, full_ref in ((w_in_b_slab_ref, w_in_ref), (w_out_b_slab_ref, w_out_ref)):
            n = out_ref.shape[0]
            out_ref[...] = full_ref[pl.ds(pl.multiple_of(j * n, n), n), :]

    @pl.when(running)
    def _():
        lax.fori_loop(0, x_ref.shape[1] // TOKEN_BLOCK, token_block, 0)

    @pl.when(running & (j == steps_per_stream - 1))
    def _():
        for s in range(n_streams):
            for h in range(HG_H):
                s_out_ref[s, h] = st_ref[s, h].T


def _sample_kernel(x_ref, cos_ref, sin_ref, mdec_ref, qdec_ref, kdec_ref, rdec_ref, mk_ref, mv_ref,
                   s_in_ref, r_in_ref,
                   norm_g_ref, w_in_ref, lb_ref, hg_g_ref, rt_g_ref, w_out_ref, fin_g_ref,
                   y_ref, s_out_ref, r_out_ref, proj_ref, xn_ref, gc_ref, mixed_ref):
    n_streams = TOKEN_BLOCK // CHUNK

    def set_st(s, h, val):
        s_out_ref[s, h] = val.T

    def set_r(s, p, val):
        r_out_ref[s, p] = val

    def get_mem(s, h):
        head_rows = pl.ds(h, N_MEM, stride=XA_H)
        return mk_ref[s, head_rows, :].astype(BF16), mv_ref[s, head_rows, :].astype(BF16)

    cos_t = jnp.concatenate([cos_ref[...]] * n_streams, axis=0)
    sin_t = jnp.concatenate([sin_ref[...]] * n_streams, axis=0)
    proj = _Projection(x_ref[...], norm_g_ref, w_in_ref, xn_ref, proj_ref)
    _mix(
        n_streams, CHUNK, x_ref, y_ref, cos_t, sin_t, mdec_ref, qdec_ref, kdec_ref, rdec_ref,
        get_mem, lb_ref, hg_g_ref, rt_g_ref, w_out_ref, fin_g_ref,
        proj_ref, gc_ref, mixed_ref,
        lambda s, h: s_in_ref[s, h].T, set_st, lambda s, p: r_in_ref[s, p], set_r, proj)


def _const_spec(shape):
    zeros = (0,) * len(shape)
    return pl.BlockSpec(shape, lambda *_: zeros, pipeline_mode=pl.Buffered(1))


def _layer_scratch(rows):
    return [pltpu.VMEM((rows, D_IN), F32),
            pltpu.VMEM((rows, D_MODEL), BF16),
            pltpu.VMEM((rows, HG_W), F32),
            pltpu.VMEM((rows, D_MIX), BF16)]


def _memory_kv(mem, g, wk, wv):
    b = mem.shape[0]
    out_f = jax.ShapeDtypeStruct((b, N_MEM * XA_H, XA_DH), F32)
    out_b = jax.ShapeDtypeStruct((b, N_MEM, XA_W), BF16)
    blk_f = pl.BlockSpec((1, N_MEM * XA_H, XA_DH), lambda i: (i, 0, 0))
    blk_b = pl.BlockSpec((1, N_MEM, XA_W), lambda i: (i, 0, 0))
    return pl.pallas_call(
        _memory_kv_kernel,
        grid=(b,),
        in_specs=[pl.BlockSpec((1, N_MEM, D_MODEL), lambda i: (i, 0, 0)),
                  pl.BlockSpec((1, D_MODEL), lambda i: (0, 0)),
                  pl.BlockSpec((D_MODEL, XA_W), lambda i: (0, 0)),
                  pl.BlockSpec((D_MODEL, XA_W), lambda i: (0, 0))],
        out_specs=[blk_f, blk_f, blk_b, blk_b],
        out_shape=[out_f, out_f, out_b, out_b],
        name="memory_kv",
    )(mem, g.reshape(1, D_MODEL), wk, wv)


def _prompt_layer(x, mk_b, mv_b, w_in, w_out, params):
    b, t, _ = x.shape
    tb = PROMPT_BLOCKS_PER_STEP * TOKEN_BLOCK
    spp = t // tb
    prep = WEIGHT_PREP_STEPS
    assert spp >= prep
    cos_t, sin_t = _rope_tables(np.arange(t))
    mdec, qdec, kdec, rdec = _retention_tables(TOKEN_BLOCK)
    consts = (mdec, qdec, kdec, rdec)
    in_rows, out_rows = D_MODEL // prep, D_MIX // prep

    def slab(step):
        return jnp.minimum(step, prep - 1)

    def run(step):
        return jnp.maximum(step - prep, 0)

    in_specs = ([pl.BlockSpec((in_rows, D_IN), lambda s: (slab(s), 0)),
                 pl.BlockSpec((out_rows, D_MODEL), lambda s: (slab(s), 0)),
                 pl.BlockSpec((b, tb, D_MODEL), lambda s: (0, run(s), 0)),
                 pl.BlockSpec((tb, LANES), lambda s: (run(s), 0)),
                 pl.BlockSpec((tb, LANES), lambda s: (run(s), 0))]
                + [_const_spec(c.shape) for c in consts]
                + [_const_spec(mk_b.shape), _const_spec(mv_b.shape)]
                + [_const_spec(p.shape) for p in params])
    out_shape = [jax.ShapeDtypeStruct((b, t, D_MODEL), F32),
                 jax.ShapeDtypeStruct((b, HG_H, HG_DK, HG_DV), F32),
                 jax.ShapeDtypeStruct((b, RT_H // 2, 2 * RT_DK, RT_DV), F32),
                 jax.ShapeDtypeStruct((D_MODEL, D_IN), BF16),
                 jax.ShapeDtypeStruct((D_MIX, D_MODEL), BF16)]
    out_specs = [pl.BlockSpec((b, tb, D_MODEL), lambda s: (0, run(s), 0)),
                 pl.BlockSpec((b, HG_H, HG_DK, HG_DV), lambda s: (0, 0, 0, 0)),
                 pl.BlockSpec((b, RT_H // 2, 2 * RT_DK, RT_DV), lambda s: (0, 0, 0, 0)),
                 pl.BlockSpec((in_rows, D_IN), lambda s: (slab(run(s)), 0)),
                 pl.BlockSpec((out_rows, D_MODEL), lambda s: (slab(run(s)), 0))]
    return pl.pallas_call(
        functools.partial(_prompt_kernel, spp),
        grid=(prep + spp,),
        in_specs=in_specs,
        out_specs=out_specs,
        out_shape=out_shape,
        scratch_shapes=_layer_scratch(b * TOKEN_BLOCK) + [
                                           pltpu.VMEM((b, HG_H, HG_DV, HG_DK), F32),
                                           pltpu.VMEM((D_MODEL, D_IN), BF16),
                                           pltpu.VMEM((D_MIX, D_MODEL), BF16)],
        compiler_params=pltpu.CompilerParams(dimension_semantics=("arbitrary",),
                                             vmem_limit_bytes=VMEM_LIMIT_BYTES),
        name="prompt_layer",
    )(w_in, w_out, x, cos_t, sin_t, *consts, mk_b, mv_b, *params)


def _sample_layer(x, s_hg, s_rt, mem_k, mem_v, weights):
    b, t, _ = x.shape
    assert t == CHUNK
    tb = TOKEN_BLOCK
    nsb = tb // t
    cos_t, sin_t = _rope_tables(PAST_LEN + np.arange(t))
    mdec, qdec, kdec, rdec = _retention_tables(t)
    consts = (mdec, qdec, kdec, rdec)
    s_rt2 = s_rt.reshape(b, RT_H // 2, 2 * RT_DK, RT_DV)
    mem_spec = pl.BlockSpec((nsb, N_MEM * XA_H, XA_DH), lambda i: (i, 0, 0))
    in_specs = ([pl.BlockSpec((tb, D_MODEL), lambda i: (i, 0)),
                 _const_spec(cos_t.shape), _const_spec(sin_t.shape)]
                + [_const_spec(c.shape) for c in consts]
                + [mem_spec, mem_spec,
                   pl.BlockSpec((nsb, HG_H, HG_DK, HG_DV), lambda i: (i, 0, 0, 0)),
                   pl.BlockSpec((nsb, RT_H // 2, 2 * RT_DK, RT_DV), lambda i: (i, 0, 0, 0))]
                + [_const_spec(w.shape) for w in weights])
    out_shape = [jax.ShapeDtypeStruct((b * t, D_MODEL), F32),
                 jax.ShapeDtypeStruct((b, HG_H, HG_DK, HG_DV), F32),
                 jax.ShapeDtypeStruct((b, RT_H // 2, 2 * RT_DK, RT_DV), F32)]
    out_specs = [pl.BlockSpec((tb, D_MODEL), lambda i: (i, 0)),
                 pl.BlockSpec((nsb, HG_H, HG_DK, HG_DV), lambda i: (i, 0, 0, 0)),
                 pl.BlockSpec((nsb, RT_H // 2, 2 * RT_DK, RT_DV), lambda i: (i, 0, 0, 0))]
    y, s_new, r_new = pl.pallas_call(
        _sample_kernel,
        grid=(b // nsb,),
        in_specs=in_specs,
        out_specs=out_specs,
        out_shape=out_shape,
        scratch_shapes=_layer_scratch(tb),
        compiler_params=pltpu.CompilerParams(dimension_semantics=("arbitrary",),
                                             vmem_limit_bytes=VMEM_LIMIT_BYTES),
        name="sample_layer",
    )(x.reshape(b * t, D_MODEL), cos_t, sin_t, *consts,
      mem_k.reshape(b, N_MEM * XA_H, XA_DH), mem_v.reshape(b, N_MEM * XA_H, XA_DH), s_hg, s_rt2, *weights)
    return y.reshape(b, t, D_MODEL), s_new, r_new


def kernel(x_prompt, x_sample, mem_prompt, state_hgrn, state_ret, cache_mem_k, cache_mem_v, norm_g, w_in,
           lb_logits, hg_norm_g, rt_norm_g, mem_norm_g, w_mem_k, w_mem_v, w_out, final_norm_g):
    depth = w_in.shape[0]
    assert depth == 1, "single-layer step"
    bp = x_prompt.shape[0]
    bs = x_sample.shape[0]
    g_in, g_hg, g_rt = norm_g[0].reshape(1, D_MODEL), hg_norm_g[0].reshape(1, HG_W), rt_norm_g[0].reshape(1, RT_W)
    g_fin = final_norm_g.reshape(1, D_MODEL)
    mk, mv, mk_b, mv_b = _memory_kv(mem_prompt, mem_norm_g[0], w_mem_k[0], w_mem_v[0])
    y_p, s_p, r_p, w_in_b, w_out_b = _prompt_layer(x_prompt, mk_b, mv_b, w_in[0], w_out[0],
                                                   (g_in, lb_logits, g_hg, g_rt, g_fin))
    y_s, s_s, r_s = _sample_layer(x_sample, state_hgrn[0], state_ret[0], cache_mem_k[0], cache_mem_v[0],
                                  (g_in, w_in_b, lb_logits, g_hg, g_rt, w_out_b, g_fin))
    return (y_p, y_s,
            s_p.reshape(1, bp, HG_H, HG_DK, HG_DV),
            r_p.reshape(1, bp, RT_H, RT_DK, RT_DV),
            mk.reshape(1, bp, N_MEM, XA_H, XA_DH),
            mv.reshape(1, bp, N_MEM, XA_H, XA_DH),
            s_s.reshape(1, bs, HG_H, HG_DK, HG_DV),
            r_s.reshape(1, bs, RT_H, RT_DK, RT_DV))
```
